```python
import math
import jax, jax.numpy as jnp
from jax import lax
import numpy as np

D_MODEL = 1024
BATCH = 2
SEQ = 8192
DEPTH = 4
DEC_BATCH = 32
DEC_SEQ = 1
PAST_LEN = 8192
PAGE_SIZE = 128

N_MIXERS = 2
N_RWKV = (DEPTH + 1) // 2
N_ATTN = DEPTH // 2
RWKV_HEAD = 64
RWKV_HEADS = D_MODEL // RWKV_HEAD
D_DECAY_LORA = 64
D_AAA_LORA = 64
D_MV_LORA = 32
D_GATE_LORA = 128
ATTN_HEAD = 64
ATTN_HEADS = D_MODEL // (2 * ATTN_HEAD)
MEM_TOKENS = 256
MEM_HEADS = 4
MEM_HEAD = 128
MEM_WIDTH = MEM_HEADS * MEM_HEAD
IN_WIDTH = 3 * D_MODEL + MEM_WIDTH
MIX_WIDTH = D_MODEL + MEM_WIDTH
D_FF = ((8 * D_MODEL + 3 * 256 - 1) // (3 * 256)) * 256
ROPE_THETA = 10000.0
Q_BLOCK = 128
NORM_EPS = 1e-6
SUBLN_EPS = 1e-5
GN_EPS = 1e-5 * RWKV_HEAD

kernel_name = 'rwkv7_diffattn_memory_hybrid_step'


def rmsnorm(x, g):
    xf = x.astype(jnp.float32)
    y = xf * lax.rsqrt(jnp.mean(xf * xf, axis=-1, keepdims=True) + NORM_EPS)
    return (y * g.astype(jnp.float32)).astype(x.dtype)


def rope(x, pos):
    d = x.shape[-1]
    half = d // 2
    inv = jnp.power(ROPE_THETA, -jnp.arange(half, dtype=jnp.float32) * 2.0 / d)
    ang = pos.astype(jnp.float32)[:, None] * inv[None, :]
    cos = jnp.cos(ang)[:, None, :]
    sin = jnp.sin(ang)[:, None, :]
    xf = x.astype(jnp.float32)
    x1, x2 = xf[..., :half], xf[..., half:]
    return jnp.concatenate([x1 * cos - x2 * sin, x2 * cos + x1 * sin], axis=-1).astype(x.dtype)


def wkv_scan(S0, r, w_log, k, v, a, b):
    tm = lambda z: jnp.swapaxes(z.astype(jnp.float32), 0, 1)
    decay = jnp.exp(-jnp.exp(w_log.astype(jnp.float32)))

    def step(S, inp):
        rt, wt, kt, vt, at, bt = inp
        sa = jnp.einsum('bhvk,bhk->bhv', S, at)
        S = S * wt[:, :, None, :] + sa[..., None] * bt[:, :, None, :] + vt[..., None] * kt[:, :, None, :]
        return S, jnp.einsum('bhvk,bhk->bhv', S, rt)

    S, ys = lax.scan(step, S0.astype(jnp.float32), (tm(r), tm(decay), tm(k), tm(v), tm(a), tm(b)))
    return jnp.swapaxes(ys, 0, 1), S


def rwkv_mixer(xn, shift_row, S0, v_first, w_in_l, i, rwkv):
    (mu, w0, w1, w2, a0, a1, a2, v0, v1, v2, g1, g2, k_k, k_a, r_k, lnx_g, lnx_b) = rwkv
    B, T, D = xn.shape
    x_prev = jnp.concatenate([shift_row[:, None].astype(xn.dtype), xn[:, :-1]], axis=1)
    xx = x_prev - xn
    xr, xw, xk, xv, xa, xg = (xn + xx * mu[i, j] for j in range(6))
    r = xr @ w_in_l[:, :D]
    k = xk @ w_in_l[:, D:2 * D]
    v = xv @ w_in_l[:, 2 * D:3 * D]
    w_log = -jax.nn.softplus(-(w0[i] + jnp.tanh(xw @ w1[i]) @ w2[i])) - 0.5
    if i == 0:
        v_first = v
    else:
        v = v + (v_first - v) * jax.nn.sigmoid(v0[i - 1] + (xv @ v1[i - 1]) @ v2[i - 1])
    a = jax.nn.sigmoid(a0[i] + (xa @ a1[i]) @ a2[i])
    g = jax.nn.sigmoid(xg @ g1[i]) @ g2[i]
    hs = lambda z: z.reshape(B, T, RWKV_HEADS, RWKV_HEAD)
    kk = hs(k * k_k[i]).astype(jnp.float32)
    kk = kk / jnp.maximum(jnp.sqrt(jnp.sum(kk * kk, axis=-1, keepdims=True)), 1e-12)
    k = k * (1 + (a - 1) * k_a[i])
    ah = hs(a).astype(jnp.float32)
    y, S = wkv_scan(S0, hs(r), hs(w_log), hs(k), hs(v), -kk, kk * ah)
    mean = jnp.mean(y, axis=-1, keepdims=True)
    var = jnp.mean(jnp.square(y - mean), axis=-1, keepdims=True)
    y = ((y - mean) * lax.rsqrt(var + GN_EPS)).reshape(B, T, D) * lnx_g[i].astype(jnp.float32) + lnx_b[i].astype(jnp.float32)
    bonus = jnp.sum(hs(r).astype(jnp.float32) * hs(k).astype(jnp.float32) * r_k[i].astype(jnp.float32), axis=-1, keepdims=True) * hs(v).astype(jnp.float32)
    y = (y + bonus.reshape(B, T, D)).astype(xn.dtype) * g
    return y, S.astype(S0.dtype), v_first


def diff_core(q, k, v, q_pos, k_pos, lam):
    qf = q.astype(jnp.float32) * (ATTN_HEAD ** -0.5)
    kf = k.astype(jnp.float32)
    mask = k_pos[None, :] <= q_pos[:, None]

    def attn_map(qh, kh):
        s = jnp.einsum('bqhd,bkhd->bhqk', qh, kh)
        return jax.nn.softmax(jnp.where(mask, s, -jnp.inf), axis=-1)

    p = attn_map(qf[..., :ATTN_HEAD], kf[..., :ATTN_HEAD]) - lam * attn_map(qf[..., ATTN_HEAD:], kf[..., ATTN_HEAD:])
    return jnp.einsum('bhqk,bkhe->bqhe', p, v.astype(jnp.float32))


def diff_mixer(xn, pos, w_in_l, past, i, layer, diff):
    lam_q1, lam_k1, lam_q2, lam_k2, subln_g = diff
    B, T, D = xn.shape
    lam_init = 0.8 - 0.6 * math.exp(-0.3 * layer)
    f32 = jnp.float32
    lam = (jnp.exp(jnp.sum(lam_q1[i].astype(f32) * lam_k1[i].astype(f32)))
           - jnp.exp(jnp.sum(lam_q2[i].astype(f32) * lam_k2[i].astype(f32))) + lam_init)
    q, k, v = jnp.split(xn @ w_in_l[:, :3 * D], 3, axis=-1)
    rot = lambda z: rope(z.reshape(B, T, 2 * ATTN_HEADS, ATTN_HEAD), pos).reshape(B, T, ATTN_HEADS, 2 * ATTN_HEAD)
    q, k = rot(q), rot(k)
    v = v.reshape(B, T, ATTN_HEADS, 2 * ATTN_HEAD)
    if past is None:
        nb = T // Q_BLOCK
        qb = jnp.swapaxes(q.reshape(B, nb, Q_BLOCK, ATTN_HEADS, 2 * ATTN_HEAD), 0, 1)
        ob = lax.map(lambda qp: diff_core(qp[0], k, v, qp[1], pos, lam), (qb, pos.reshape(nb, Q_BLOCK)))
        o = jnp.swapaxes(ob, 0, 1).reshape(B, T, ATTN_HEADS, 2 * ATTN_HEAD)
    else:
        cache_k, cache_v, page_table = past
        kp = cache_k[i, page_table].reshape(B, -1, ATTN_HEADS, 2 * ATTN_HEAD).astype(k.dtype)
        vp = cache_v[i, page_table].reshape(B, -1, ATTN_HEADS, 2 * ATTN_HEAD).astype(v.dtype)
        k_all = jnp.concatenate([kp, k], axis=1)
        v_all = jnp.concatenate([vp, v], axis=1)
        k_pos = jnp.arange(k_all.shape[1], dtype=jnp.int32)
        o = diff_core(q, k_all, v_all, pos, k_pos, lam)
    o = o * lax.rsqrt(jnp.mean(o * o, axis=-1, keepdims=True) + SUBLN_EPS) * subln_g[i].astype(f32) * (1.0 - lam_init)
    return o.reshape(B, T, D).astype(xn.dtype), k, v


def mem_kv(mem, g, wk, wv):
    B = mem.shape[0]
    mn = rmsnorm(mem, g)
    return ((mn @ wk).reshape(B, -1, MEM_HEADS, MEM_HEAD),
            (mn @ wv).reshape(B, -1, MEM_HEADS, MEM_HEAD))


def mem_attend(xn, w_q, mk, mv):
    B, T, _ = xn.shape
    q = (xn @ w_q).reshape(B, T, MEM_HEADS, MEM_HEAD).astype(jnp.float32) * (MEM_HEAD ** -0.5)
    s = jnp.einsum('bqhd,bkhd->bhqk', q, mk.astype(jnp.float32))
    p = jax.nn.softmax(s, axis=-1)
    o = jnp.einsum('bhqk,bkhd->bqhd', p, mv.astype(jnp.float32))
    return o.reshape(B, T, MEM_WIDTH).astype(xn.dtype)


def layer_stack(x, pos, shift0, wkv0, mem_k, mem_v, past,
                w_in, w_out, mix_norm_g, ffn_norm_g, w_gate, w_up, w_down, final_norm_g,
                rwkv, diff):
    shifts, states, ks, vs = [], [], [], []
    v_first = None
    for l in range(DEPTH):
        xn = rmsnorm(x, mix_norm_g[l])
        wl = w_in[l]
        idx = l // N_MIXERS
        if l % N_MIXERS == 0:
            y_tok, S, v_first = rwkv_mixer(xn, shift0[idx], wkv0[idx], v_first, wl, idx, rwkv)
            shifts.append(xn[:, -1])
            states.append(S)
        else:
            y_tok, k, v = diff_mixer(xn, pos, wl, past, idx, l, diff)
            ks.append(k)
            vs.append(v)
        y_mem = mem_attend(xn, wl[:, 3 * D_MODEL:], mem_k[l], mem_v[l])
        x = x + jnp.concatenate([y_tok, y_mem], axis=-1) @ w_out[l]
        h = rmsnorm(x, ffn_norm_g[l])
        x = x + (jax.nn.silu(h @ w_gate[l]) * (h @ w_up[l])) @ w_down[l]
    return rmsnorm(x, final_norm_g), jnp.stack(shifts), jnp.stack(states), jnp.stack(ks), jnp.stack(vs)


def setup_inputs(seed: int = 0) -> dict:
    key = jax.random.key(seed)
    ks = iter(jax.random.split(key, 64))
    nk = lambda: next(ks)
    nrm = lambda shape, scale: jax.random.normal(nk(), shape, jnp.float32) * scale
    gain = lambda shape: 1.0 + nrm(shape, 0.02)
    D = D_MODEL
    H2 = 2 * ATTN_HEAD
    n_pages = PAST_LEN // PAGE_SIZE
    n_phys = (5 * DEC_BATCH * n_pages + 3) // 4
    out = {}
    out['x_prompt'] = nrm((BATCH, SEQ, D), 1.0)
    out['x_sample'] = nrm((DEC_BATCH, DEC_SEQ, D), 1.0)
    out['cache_k'] = nrm((N_ATTN, n_phys, PAGE_SIZE, ATTN_HEADS, H2), 1.0)
    out['cache_v'] = nrm((N_ATTN, n_phys, PAGE_SIZE, ATTN_HEADS, H2), 1.0)
    out['cache_mem_k'] = nrm((DEPTH, DEC_BATCH, MEM_TOKENS, MEM_HEADS, MEM_HEAD), 1.0)
    out['cache_mem_v'] = nrm((DEPTH, DEC_BATCH, MEM_TOKENS, MEM_HEADS, MEM_HEAD), 1.0)
    out['state_rwkv_wkv'] = nrm((N_RWKV, DEC_BATCH, RWKV_HEADS, RWKV_HEAD, RWKV_HEAD), 0.3)
    out['state_rwkv_shift'] = nrm((N_RWKV, DEC_BATCH, D), 1.0)
    out['page_table'] = jax.random.permutation(nk(), n_phys)[:DEC_BATCH * n_pages].reshape(DEC_BATCH, n_pages).astype(jnp.int32)
    out['mem_prompt'] = nrm((BATCH, MEM_TOKENS, D), 1.0)
    out['w_in'] = nrm((DEPTH, D, IN_WIDTH), D ** -0.5)
    out['w_out'] = nrm((DEPTH, MIX_WIDTH, D), 0.5 * MIX_WIDTH ** -0.5)
    out['mix_norm_g'] = gain((DEPTH, D))
    out['ffn_norm_g'] = gain((DEPTH, D))
    out['w_gate'] = nrm((DEPTH, D, D_FF), D ** -0.5)
    out['w_up'] = nrm((DEPTH, D, D_FF), D ** -0.5)
    out['w_down'] = nrm((DEPTH, D_FF, D), 0.5 * D_FF ** -0.5)
    out['final_norm_g'] = gain((D,))
    out['mem_norm_g'] = gain((DEPTH, D))
    out['w_mem_k'] = nrm((DEPTH, D, MEM_WIDTH), D ** -0.5)
    out['w_mem_v'] = nrm((DEPTH, D, MEM_WIDTH), D ** -0.5)
    out['rwkv_mu'] = jax.random.uniform(nk(), (N_RWKV, 6, D), jnp.float32)
    out['rwkv_w0'] = nrm((N_RWKV, D), 0.5) - 0.5
    out['rwkv_w1'] = nrm((N_RWKV, D, D_DECAY_LORA), D ** -0.5)
    out['rwkv_w2'] = nrm((N_RWKV, D_DECAY_LORA, D), 0.1 * D_DECAY_LORA ** -0.5)
    out['rwkv_a0'] = nrm((N_RWKV, D), 0.1)
    out['rwkv_a1'] = nrm((N_RWKV, D, D_AAA_LORA), D ** -0.5)
    out['rwkv_a2'] = nrm((N_RWKV, D_AAA_LORA, D), 0.1 * D_AAA_LORA ** -0.5)
    out['rwkv_v0'] = nrm((N_RWKV - 1, D), 0.1)
    out['rwkv_v1'] = nrm((N_RWKV - 1, D, D_MV_LORA), D ** -0.5)
    out['rwkv_v2'] = nrm((N_RWKV - 1, D_MV_LORA, D), 0.1 * D_MV_LORA ** -0.5)
    out['rwkv_g1'] = nrm((N_RWKV, D, D_GATE_LORA), D ** -0.5)
    out['rwkv_g2'] = nrm((N_RWKV, D_GATE_LORA, D), D_GATE_LORA ** -0.5)
    out['rwkv_k_k'] = 0.85 + nrm((N_RWKV, D), 0.02)
    out['rwkv_k_a'] = gain((N_RWKV, D))
    out['rwkv_r_k'] = nrm((N_RWKV, RWKV_HEADS, RWKV_HEAD), 0.1)
    out['rwkv_lnx_g'] = gain((N_RWKV, D))
    out['rwkv_lnx_b'] = nrm((N_RWKV, D), 0.01)
    out['diff_lam_q1'] = nrm((N_ATTN, ATTN_HEAD), 0.1)
    out['diff_lam_k1'] = nrm((N_ATTN, ATTN_HEAD), 0.1)
    out['diff_lam_q2'] = nrm((N_ATTN, ATTN_HEAD), 0.1)
    out['diff_lam_k2'] = nrm((N_ATTN, ATTN_HEAD), 0.1)
    out['diff_subln_g'] = gain((N_ATTN, 2 * ATTN_HEAD))
    return out


def reference(x_prompt, x_sample, cache_k, cache_v, cache_mem_k, cache_mem_v,
              state_rwkv_wkv, state_rwkv_shift, page_table, mem_prompt,
              w_in, w_out, mix_norm_g, ffn_norm_g, w_gate, w_up, w_down, final_norm_g,
              mem_norm_g, w_mem_k, w_mem_v,
              rwkv_mu, rwkv_w0, rwkv_w1, rwkv_w2, rwkv_a0, rwkv_a1, rwkv_a2,
              rwkv_v0, rwkv_v1, rwkv_v2, rwkv_g1, rwkv_g2, rwkv_k_k, rwkv_k_a, rwkv_r_k,
              rwkv_lnx_g, rwkv_lnx_b,
              diff_lam_q1, diff_lam_k1, diff_lam_q2, diff_lam_k2, diff_subln_g):
    rwkv = (rwkv_mu, rwkv_w0, rwkv_w1, rwkv_w2, rwkv_a0, rwkv_a1, rwkv_a2,
            rwkv_v0, rwkv_v1, rwkv_v2, rwkv_g1, rwkv_g2, rwkv_k_k, rwkv_k_a, rwkv_r_k,
            rwkv_lnx_g, rwkv_lnx_b)
    diff = (diff_lam_q1, diff_lam_k1, diff_lam_q2, diff_lam_k2, diff_subln_g)

    mks, mvs = [], []
    for l in range(DEPTH):
        mk, mv = mem_kv(mem_prompt, mem_norm_g[l], w_mem_k[l], w_mem_v[l])
        mks.append(mk)
        mvs.append(mv)
    p_mem_k = jnp.stack(mks)
    p_mem_v = jnp.stack(mvs)

    B, T, _ = x_prompt.shape
    pos_p = jnp.arange(T, dtype=jnp.int32)
    shift0 = jnp.zeros((N_RWKV, B, D_MODEL), x_prompt.dtype)
    wkv0 = jnp.zeros((N_RWKV, B, RWKV_HEADS, RWKV_HEAD, RWKV_HEAD), jnp.float32)
    y_prompt, p_shift, p_wkv, p_k, p_v = layer_stack(
        x_prompt, pos_p, shift0, wkv0, p_mem_k, p_mem_v, None,
        w_in, w_out, mix_norm_g, ffn_norm_g, w_gate, w_up, w_down, final_norm_g, rwkv, diff)

    past_len = page_table.shape[1] * PAGE_SIZE
    pos_s = past_len + jnp.arange(x_sample.shape[1], dtype=jnp.int32)
    y_sample, s_shift, s_wkv, s_k, s_v = layer_stack(
        x_sample, pos_s, state_rwkv_shift, state_rwkv_wkv, cache_mem_k, cache_mem_v,
        (cache_k, cache_v, page_table),
        w_in, w_out, mix_norm_g, ffn_norm_g, w_gate, w_up, w_down, final_norm_g, rwkv, diff)

    return (y_prompt, y_sample, p_wkv, p_shift, p_k, p_v, p_mem_k, p_mem_v, s_wkv, s_shift, s_k, s_v)
```

```python
import functools
import math

import jax
import jax.numpy as jnp
from jax import lax
from jax.experimental import pallas as pl
from jax.experimental.pallas import tpu as pltpu

F32 = jnp.float32
BF16 = jnp.bfloat16

D_MODEL = 1024
DEPTH = 4
PAGE_SIZE = 128
RWKV_HEAD = 64
ATTN_HEAD = 64
ATTN_HEADS = D_MODEL // (2 * ATTN_HEAD)
MEM_HEADS = 4
MEM_HEAD = 128
MEM_WIDTH = MEM_HEADS * MEM_HEAD
D_FF = 2816
ROPE_THETA = 10000.0
NORM_EPS = 1e-6
SUBLN_EPS = 1e-5
GN_EPS = 1e-5 * RWKV_HEAD

LANES = 128
LORA_PAD = 128
WKV_CHUNK = 64
VMEM_LIMIT = 56 * 1024 * 1024

_NT = (((1,), (1,)), ((), ()))
_TN = (((0,), (0,)), ((), ()))
_NEG = -1e30


def _dot(a, b):
    return jnp.dot(a, b, preferred_element_type=F32)


def _dot_nt(a, b):
    return lax.dot_general(a, b, _NT, preferred_element_type=F32)


def _dot_tn(a, b):
    return lax.dot_general(a, b, _TN, preferred_element_type=F32)


def _split2(x):
    hi = x.astype(BF16)
    lo = (x - hi.astype(F32)).astype(BF16)
    return hi, lo


def _split3(x):
    h1 = x.astype(BF16)
    r1 = x - h1.astype(F32)
    h2 = r1.astype(BF16)
    h3 = (r1 - h2.astype(F32)).astype(BF16)
    return h1, h2, h3


def _dot_exact_rhs(x, m_bf16):
    h1, h2, h3 = _split3(x)
    return _dot(h1, m_bf16) + _dot(h2, m_bf16) + _dot(h3, m_bf16)


def _dot_exact_lhs(m_bf16, x):
    h1, h2, h3 = _split3(x)
    return _dot(m_bf16, h1) + _dot(m_bf16, h2) + _dot(m_bf16, h3)


def _mm(a, b, kind, passes):
    f = {"nn": _dot, "nt": _dot_nt, "tn": _dot_tn}[kind]
    if passes == 1:
        return f(a.astype(BF16), b.astype(BF16))
    ah, al = _split2(a)
    bh, bl = _split2(b)
    return f(ah, bh) + f(ah, bl) + f(al, bh)


def _rms(x, g):
    ms = jnp.mean(x * x, axis=-1, keepdims=True)
    return x * lax.rsqrt(ms + NORM_EPS) * g


def _sigmoid(x):
    return 1.0 / (1.0 + jnp.exp(-x))


def _softplus(x):
    return jnp.maximum(x, 0.0) + jnp.log(1.0 + jnp.exp(-jnp.abs(x)))


def _seg_ones(width):
    r = lax.broadcasted_iota(jnp.int32, (LANES, LANES), 0) // width
    c = lax.broadcasted_iota(jnp.int32, (LANES, LANES), 1) // width
    return jnp.where(r == c, 1.0, 0.0).astype(BF16)


def _resident(shape):
    nd = len(shape)
    return pl.BlockSpec(shape, lambda *_: (0,) * nd, pipeline_mode=pl.Buffered(1))


def _params(sem):
    return pltpu.CompilerParams(dimension_semantics=sem, vmem_limit_bytes=VMEM_LIMIT)


def _mem_kv_kernel(mem_ref, g_ref, wk_ref, wv_ref, k_ref, v_ref):
    mn = _rms(mem_ref[...], g_ref[0]).astype(BF16)
    k_ref[0] = _dot(mn, wk_ref[0])
    v_ref[0] = _dot(mn, wv_ref[0])


def _mem_kv(mem2d, g, wk, wv):
    rows = mem2d.shape[0]
    out = jax.ShapeDtypeStruct((DEPTH, rows, MEM_WIDTH), F32)
    return pl.pallas_call(
        _mem_kv_kernel,
        grid=(DEPTH,),
        in_specs=[
            pl.BlockSpec((rows, D_MODEL), lambda l: (0, 0)),
            pl.BlockSpec((1, 1, D_MODEL), lambda l: (l, 0, 0)),
            pl.BlockSpec((1, D_MODEL, MEM_WIDTH), lambda l: (l, 0, 0)),
            pl.BlockSpec((1, D_MODEL, MEM_WIDTH), lambda l: (l, 0, 0)),
        ],
        out_specs=[pl.BlockSpec((1, rows, MEM_WIDTH), lambda l: (l, 0, 0))] * 2,
        out_shape=[out, out],
        compiler_params=_params(("arbitrary",)),
        name="mem_kv",
    )(mem2d, g, wk, wv)


def _mem_attend_kernel(q_ref, k_ref, v_ref, o_ref):
    q = q_ref[0]
    k = k_ref[0].astype(BF16)
    v = v_ref[0].astype(BF16)
    outs = []
    for h in range(MEM_HEADS):
        sl = slice(h * MEM_HEAD, (h + 1) * MEM_HEAD)
        s = _dot_nt(q[:, sl], k[:, sl])
        m = jnp.max(s, axis=-1, keepdims=True)
        p = jnp.exp(s - m)
        l = jnp.sum(p, axis=-1, keepdims=True)
        outs.append(_dot(p.astype(BF16), v[:, sl]) / l)
    o_ref[0] = jnp.concatenate(outs, axis=-1).astype(o_ref.dtype)


def _mem_attend(q, mk, mv, tq):
    B, T, _ = q.shape
    M = mk.shape[1]
    return pl.pallas_call(
        _mem_attend_kernel,
        grid=(B, T // tq),
        in_specs=[
            pl.BlockSpec((1, tq, MEM_WIDTH), lambda b, i: (b, i, 0)),
            pl.BlockSpec((1, M, MEM_WIDTH), lambda b, i: (b, 0, 0)),
            pl.BlockSpec((1, M, MEM_WIDTH), lambda b, i: (b, 0, 0)),
        ],
        out_specs=pl.BlockSpec((1, tq, MEM_WIDTH), lambda b, i: (b, i, 0)),
        out_shape=jax.ShapeDtypeStruct((B, T, MEM_WIDTH), BF16),
        compiler_params=_params(("arbitrary", "arbitrary")),
        name="mem_attend",
    )(q, mk, mv)


def _out_ffn_kernel(x_ref, yt_ref, ym_ref, wo_ref, g_ref, wg_ref, wu_ref, wd_ref, fg_ref, o_ref,
                    *, final, ft):
    x1 = x_ref[0] + (_dot(yt_ref[0], wo_ref[:D_MODEL, :]) + _dot(ym_ref[0], wo_ref[D_MODEL:, :]))
    h = _rms(x1, g_ref[...]).astype(BF16)
    acc = jnp.zeros_like(x1)
    for f in range(0, D_FF, ft):
        gt = _dot(h, wg_ref[:, f:f + ft])
        up = _dot(h, wu_ref[:, f:f + ft])
        act = (gt * _sigmoid(gt) * up).astype(BF16)
        acc = acc + _dot(act, wd_ref[f:f + ft, :])
    acc = x1 + acc
    if final:
        acc = _rms(acc, fg_ref[...])
    o_ref[0] = acc


def _out_ffn(x, ytok, ymem, wo, g, wg, wu, wd, fg, tm, final):
    B, T, _ = x.shape
    row = lambda w: pl.BlockSpec((1, tm, w), lambda b, i: (b, i, 0))
    return pl.pallas_call(
        functools.partial(_out_ffn_kernel, final=final, ft=256),
        grid=(B, T // tm),
        in_specs=[
            row(D_MODEL), row(D_MODEL), row(MEM_WIDTH),
            _resident((D_MODEL + MEM_WIDTH, D_MODEL)),
            _resident((1, D_MODEL)),
            _resident((D_MODEL, D_FF)), _resident((D_MODEL, D_FF)), _resident((D_FF, D_MODEL)),
            _resident((1, D_MODEL)),
        ],
        out_specs=row(D_MODEL),
        out_shape=jax.ShapeDtypeStruct((B, T, D_MODEL), F32),
        compiler_params=_params(("arbitrary", "arbitrary")),
        name="out_ffn",
    )(x, ytok, ymem, wo, g, wg, wu, wd, fg)


def _diff_proj_kernel(x_ref, g_ref, w_ref, cos_ref, sin_ref,
                      q_ref, k_ref, kb_ref, v_ref, vb_ref, mq_ref):
    xn = _rms(x_ref[0], g_ref[...]).astype(BF16)
    tm = xn.shape[0]
    cos = cos_ref[...]
    sin = sin_ref[...]
    lane = lax.broadcasted_iota(jnp.int32, (tm, LANES), 1)
    first = (lane % ATTN_HEAD) < (ATTN_HEAD // 2)

    def rope(z):
        rot = jnp.where(first, pltpu.roll(z, LANES - ATTN_HEAD // 2, 1), pltpu.roll(z, ATTN_HEAD // 2, 1))
        return z * cos + rot * sin

    half = D_MODEL // 2
    for c in range(2):
        z = _dot(xn, w_ref[:, c * half:(c + 1) * half])
        for j in range(half // LANES):
            col = c * half + j * LANES
            q_ref[0, :, col:col + LANES] = (rope(z[:, j * LANES:(j + 1) * LANES]) * (ATTN_HEAD ** -0.5)).astype(BF16)
    for c in range(2):
        z = _dot(xn, w_ref[:, D_MODEL + c * half:D_MODEL + (c + 1) * half])
        for j in range(half // LANES):
            col = c * half + j * LANES
            kr = rope(z[:, j * LANES:(j + 1) * LANES])
            k_ref[0, :, col:col + LANES] = kr
            kb_ref[0, :, col:col + LANES] = kr.astype(BF16)
    for c in range(2):
        z = _dot(xn, w_ref[:, 2 * D_MODEL + c * half:2 * D_MODEL + (c + 1) * half])
        v_ref[0, :, c * half:(c + 1) * half] = z
        vb_ref[0, :, c * half:(c + 1) * half] = z.astype(BF16)
    z = _dot(xn, w_ref[:, 3 * D_MODEL:])
    mq_ref[0] = (z * (MEM_HEAD ** -0.5)).astype(BF16)


def _diff_proj(x, g, w, cos_t, sin_t, tm):
    B, T, _ = x.shape
    row = lambda w_: pl.BlockSpec((1, tm, w_), lambda b, i: (b, i, 0))
    sd = lambda w_, dt: jax.ShapeDtypeStruct((B, T, w_), dt)
    return pl.pallas_call(
        _diff_proj_kernel,
        grid=(B, T // tm),
        in_specs=[
            row(D_MODEL), _resident((1, D_MODEL)), _resident((D_MODEL, 3 * D_MODEL + MEM_WIDTH)),
            pl.BlockSpec((tm, LANES), lambda b, i: (i, 0)),
            pl.BlockSpec((tm, LANES), lambda b, i: (i, 0)),
        ],
        out_specs=[row(D_MODEL)] * 5 + [row(MEM_WIDTH)],
        out_shape=[sd(D_MODEL, BF16), sd(D_MODEL, F32), sd(D_MODEL, BF16), sd(D_MODEL, F32),
                   sd(D_MODEL, BF16), sd(MEM_WIDTH, BF16)],
        compiler_params=_params(("arbitrary", "arbitrary")),
        name="diff_proj",
    )(x, g, w, cos_t, sin_t)


def _lambda(lam_ref, lam_init):
    lp = lam_ref[...]
    s1 = jnp.sum(lp[0:1] * lp[1:2], axis=-1, keepdims=True)
    s2 = jnp.sum(lp[2:3] * lp[3:4], axis=-1, keepdims=True)
    return jnp.exp(s1) - jnp.exp(s2) + lam_init


def _subln(o, sg, lam_init):
    return o * lax.rsqrt(jnp.mean(o * o, axis=-1, keepdims=True) + SUBLN_EPS) * sg * (1.0 - lam_init)


def _flash_kernel(q_ref, k_ref, v_ref, lam_ref, sg_ref, o_ref, *, tq, tk, lam_init):
    qi = pl.program_id(2)
    q = q_ref[0]
    lane = lax.broadcasted_iota(jnp.int32, (tq, LANES), 1)
    zero = jnp.zeros_like(q)
    qs = jnp.concatenate([jnp.where(lane < ATTN_HEAD, q, zero), jnp.where(lane < ATTN_HEAD, zero, q)], axis=0)
    rows = 2 * tq

    def step(ki, carry, masked):
        m, l, acc = carry
        start = pl.multiple_of(ki * tk, tk)
        kb = k_ref[0, pl.ds(start, tk), :]
        vb = v_ref[0, pl.ds(start, tk), :]
        s = _dot_nt(qs, kb)
        if masked:
            r = lax.broadcasted_iota(jnp.int32, (rows, tk), 0)
            qpos = jnp.where(r >= tq, r - tq, r) + qi * tq
            kpos = lax.broadcasted_iota(jnp.int32, (rows, tk), 1) + ki * tk
            s = jnp.where(kpos <= qpos, s, _NEG)
        m_new = jnp.maximum(m, jnp.max(s, axis=-1, keepdims=True))
        alpha = jnp.exp(m - m_new)
        p = jnp.exp(s - m_new)
        l = alpha * l + jnp.sum(p, axis=-1, keepdims=True)
        acc = alpha * acc + _dot(p.astype(BF16), vb)
        return m_new, l, acc

    init = (jnp.full((rows, 1), _NEG, F32), jnp.zeros((rows, 1), F32), jnp.zeros((rows, LANES), F32))
    nfull = (qi * tq) // tk
    carry = lax.fori_loop(0, nfull, lambda ki, c: step(ki, c, False), init)
    m, l, acc = step(nfull, carry, True)
    on = acc / l
    lam = _lambda(lam_ref, lam_init)
    o = on[:tq] - lam * on[tq:]
    o_ref[0] = _subln(o, sg_ref[...], lam_init).astype(o_ref.dtype)


def _flash(q, k, v, lam_p, sg, lam_init, tq, tk):
    B, T, _ = q.shape
    return pl.pallas_call(
        functools.partial(_flash_kernel, tq=tq, tk=tk, lam_init=lam_init),
        grid=(B, ATTN_HEADS, T // tq),
        in_specs=[
            pl.BlockSpec((1, tq, LANES), lambda b, h, i: (b, i, h)),
            pl.BlockSpec((1, T, LANES), lambda b, h, i: (b, 0, h)),
            pl.BlockSpec((1, T, LANES), lambda b, h, i: (b, 0, h)),
            _resident((4, ATTN_HEAD)), _resident((1, LANES)),
        ],
        out_specs=pl.BlockSpec((1, tq, LANES), lambda b, h, i: (b, i, h)),
        out_shape=jax.ShapeDtypeStruct((B, T, D_MODEL), BF16),
        compiler_params=_params(("arbitrary", "arbitrary", "arbitrary")),
        name="diff_flash",
    )(q, k, v, lam_p, sg)


def _dec_attn_kernel(pt_ref, q_ref, k_ref, v_ref, kn_ref, vn_ref, lam_ref, sg_ref, o_ref,
                     m_sc, l_sc, acc_sc, *, lam_init):
    j = pl.program_id(1)
    H = ATTN_HEADS
    G = 2 * H

    @pl.when(j == 0)
    def _():
        m_sc[...] = jnp.full(m_sc.shape, _NEG, F32)
        l_sc[...] = jnp.zeros(l_sc.shape, F32)
        acc_sc[...] = jnp.zeros(acc_sc.shape, F32)

    q8 = q_ref[0]
    lane = lax.broadcasted_iota(jnp.int32, (H, LANES), 1)
    zero = jnp.zeros_like(q8)
    q16 = jnp.concatenate([jnp.where(lane < ATTN_HEAD, q8, zero), jnp.where(lane < ATTN_HEAD, zero, q8)], axis=0)

    n = PAGE_SIZE * H
    k2 = k_ref[...].reshape(n, LANES).astype(BF16)
    v2 = v_ref[...].reshape(n, LANES).astype(BF16)
    s = _dot_nt(q16, k2)
    rg = lax.broadcasted_iota(jnp.int32, (G, n), 0) % H
    ch = lax.broadcasted_iota(jnp.int32, (G, n), 1) % H
    s = jnp.where(rg == ch, s, _NEG)
    m_old = m_sc[...]
    m_new = jnp.maximum(m_old, jnp.max(s, axis=-1, keepdims=True))
    alpha = jnp.exp(m_old - m_new)
    p = jnp.exp(s - m_new)
    l_sc[...] = alpha * l_sc[...] + jnp.sum(p, axis=-1, keepdims=True)
    acc_sc[...] = alpha * acc_sc[...] + _dot(p.astype(BF16), v2)
    m_sc[...] = m_new

    @pl.when(j == pl.num_programs(1) - 1)
    def _():
        kn = kn_ref[0]
        vn = vn_ref[0]
        kn16 = jnp.concatenate([kn, kn], axis=0)
        vn16 = jnp.concatenate([vn, vn], axis=0)
        sn = jnp.sum(q16.astype(F32) * kn16, axis=-1, keepdims=True)
        m0 = m_sc[...]
        m1 = jnp.maximum(m0, sn)
        a0 = jnp.exp(m0 - m1)
        pn = jnp.exp(sn - m1)
        l1 = a0 * l_sc[...] + pn
        acc1 = a0 * acc_sc[...] + pn * vn16
        on = acc1 / l1
        lam = _lambda(lam_ref, lam_init)
        o = on[:H] - lam * on[H:]
        o_ref[0] = _subln(o, sg_ref[...], lam_init).astype(o_ref.dtype)


def _dec_attn(page_table, q, cache_k, cache_v, kn, vn, lam_p, sg, lam_init, layer):
    B, n_pages = page_table.shape
    H = ATTN_HEADS
    head = lambda: pl.BlockSpec((1, H, LANES), lambda b, j, pt: (b, 0, 0))
    page = lambda: pl.BlockSpec((None, None, PAGE_SIZE, H, LANES), lambda b, j, pt: (layer, pt[b, j], 0, 0, 0))
    grid_spec = pltpu.PrefetchScalarGridSpec(
        num_scalar_prefetch=1,
        grid=(B, n_pages),
        in_specs=[head(), page(), page(), head(), head(), _resident((4, ATTN_HEAD)), _resident((1, LANES))],
        out_specs=pl.BlockSpec((1, H, LANES), lambda b, j, pt: (b, 0, 0)),
        scratch_shapes=[pltpu.VMEM((2 * H, 1), F32), pltpu.VMEM((2 * H, 1), F32), pltpu.VMEM((2 * H, LANES), F32)],
    )
    return pl.pallas_call(
        functools.partial(_dec_attn_kernel, lam_init=lam_init),
        grid_spec=grid_spec,
        out_shape=jax.ShapeDtypeStruct((B, H, LANES), BF16),
        compiler_params=_params(("arbitrary", "arbitrary")),
        name="paged_diff_attn",
    )(page_table, q, cache_k, cache_v, kn, vn, lam_p, sg)


_V_MU, _V_W0, _V_A0, _V_V0, _V_KK, _V_KA, _V_G = 0, 6, 7, 8, 9, 10, 11
_N_VEC = 16


def _rwkv_proj_kernel(*refs, shifted, has_vfirst):
    it = iter(refs)
    x_ref = next(it)
    prev_ref = next(it)
    shift_ref = next(it) if shifted else None
    vf_ref = next(it) if has_vfirst else None
    vec_ref, w_ref, w1_ref, w2_ref, a1_ref, a2_ref = (next(it) for _ in range(6))
    v1_ref, v2_ref = (next(it), next(it)) if has_vfirst else (None, None)
    g1_ref, g2_ref = next(it), next(it)
    r_ref, lw_ref, k_ref, v_ref, a_ref, b_ref, g_ref, mq_ref, xs_ref = (next(it) for _ in range(9))

    vec = vec_ref[...]
    row = lambda i: vec[i:i + 1]
    gain = row(_V_G)
    xn = _rms(x_ref[0], gain)
    tm = xn.shape[0]
    if shifted:
        pr = _rms(prev_ref[0][7:8], gain)
        pr = jnp.where(pl.program_id(1) == 0, shift_ref[0], pr)
        ridx = lax.broadcasted_iota(jnp.int32, (tm, 1), 0)
        xprev = jnp.where(ridx == 0, pr, pltpu.roll(xn, 1, 0))
        xs_ref[0] = xn[tm - 1:tm]
    else:
        xprev = prev_ref[0]
        xs_ref[0] = xn
    xx = xprev - xn
    mix = lambda j: (xn + xx * row(_V_MU + j)).astype(BF16)
    xr, xw, xk, xv, xa, xg = (mix(j) for j in range(6))

    D = D_MODEL
    r = _dot(xr, w_ref[:, :D])
    k = _dot(xk, w_ref[:, D:2 * D])
    v = _dot(xv, w_ref[:, 2 * D:3 * D])
    mq_ref[0] = (_dot(xn.astype(BF16), w_ref[:, 3 * D:]) * (MEM_HEAD ** -0.5)).astype(BF16)

    w_in = row(_V_W0) + _dot(jnp.tanh(_dot(xw, w1_ref[...])).astype(BF16), w2_ref[...])
    w_log = -_softplus(-w_in) - 0.5
    lw_ref[0] = -jnp.exp(w_log)
    if has_vfirst:
        gate = _sigmoid(row(_V_V0) + _dot(_dot(xv, v1_ref[...]).astype(BF16), v2_ref[...]))
        v = v + (vf_ref[0] - v) * gate
    a = _sigmoid(row(_V_A0) + _dot(_dot(xa, a1_ref[...]).astype(BF16), a2_ref[...]))
    g_ref[0] = _dot(_sigmoid(_dot(xg, g1_ref[...])).astype(BF16), g2_ref[...])

    seg = _seg_ones(RWKV_HEAD)
    kk = k * row(_V_KK)
    sq = kk * kk
    n2 = jnp.concatenate(
        [sum(_dot(p_, seg) for p_ in _split2(sq[:, c:c + LANES])) for c in range(0, D, LANES)], axis=-1)
    kk = kk / jnp.maximum(jnp.sqrt(n2), 1e-12)
    r_ref[0] = r
    k_ref[0] = k * (1.0 + (a - 1.0) * row(_V_KA))
    v_ref[0] = v
    a_ref[0] = -kk
    b_ref[0] = kk * a


def _rwkv_proj(x, prev, shift, vfirst, vec, w, loras, tm):
    B, T, _ = x.shape
    shifted = shift is not None
    has_vfirst = vfirst is not None
    row = lambda w_: pl.BlockSpec((1, tm, w_), lambda b, i: (b, i, 0))
    ins, specs = [x], [row(D_MODEL)]
    if shifted:
        ins += [prev, shift]
        specs += [pl.BlockSpec((1, 8, D_MODEL), lambda b, i: (b, jnp.maximum(i * (tm // 8) - 1, 0), 0)),
                  pl.BlockSpec((1, 1, D_MODEL), lambda b, i: (b, 0, 0))]
    else:
        ins += [prev]
        specs += [row(D_MODEL)]
    if has_vfirst:
        ins.append(vfirst)
        specs.append(row(D_MODEL))
    w1, w2, a1, a2, v1, v2, g1, g2 = loras
    small = [vec, w, w1, w2, a1, a2] + ([v1, v2] if has_vfirst else []) + [g1, g2]
    ins += small
    specs += [_resident(s.shape) for s in small]
    sd = lambda dt, w_=D_MODEL, t_=T: jax.ShapeDtypeStruct((B, t_, w_), dt)
    xs_rows = 1 if shifted else T
    xs_spec = (pl.BlockSpec((1, 1, D_MODEL), lambda b, i: (b, 0, 0)) if shifted else row(D_MODEL))
    return pl.pallas_call(
        functools.partial(_rwkv_proj_kernel, shifted=shifted, has_vfirst=has_vfirst),
        grid=(B, T // tm),
        in_specs=specs,
        out_specs=[row(D_MODEL)] * 7 + [row(MEM_WIDTH), xs_spec],
        out_shape=[sd(F32)] * 7 + [sd(BF16, MEM_WIDTH), sd(F32, D_MODEL, xs_rows)],
        compiler_params=_params(("arbitrary", "arbitrary")),
        name="rwkv_proj",
    )(*ins)


def _wkv_kernel(r_ref, lw_ref, k_ref, v_ref, a_ref, b_ref, g_ref, vec_ref, s0_ref, y_ref, so_ref, s_sc,
                *, L, P, passes):
    c = pl.program_id(2)

    @pl.when(c == 0)
    def _():
        s_sc[...] = s0_ref[0]

    L2 = 2 * L
    tri = jnp.where(lax.broadcasted_iota(jnp.int32, (L, L), 0) >= lax.broadcasted_iota(jnp.int32, (L, L), 1),
                    1.0, 0.0).astype(BF16)
    r2 = lax.broadcasted_iota(jnp.int32, (L2, L2), 0)
    c2 = lax.broadcasted_iota(jnp.int32, (L2, L2), 1)
    strict = (r2 % L) > (c2 % L)
    incl = (r2 % L) >= (c2 % L)
    eye = jnp.where(r2 == c2, 1.0, 0.0)
    m0 = lax.broadcasted_iota(jnp.int32, (L, LANES), 1) < RWKV_HEAD
    seg = _seg_ones(RWKV_HEAD)
    mm = functools.partial(_mm, passes=passes)

    def stack(z):
        return jnp.concatenate([jnp.where(m0, z, 0.0), jnp.where(m0, 0.0, z)], axis=0)

    def segsum(z):
        hi, lo = _split2(z)
        return _dot(hi, seg) + _dot(lo, seg)

    for p in range(P):
        sl = slice(p * LANES, (p + 1) * LANES)
        r = r_ref[0, :, sl]
        lw = lw_ref[0, :, sl]
        k = k_ref[0, :, sl]
        v = v_ref[0, :, sl]
        a = a_ref[0, :, sl]
        b = b_ref[0, :, sl]
        cum = _dot_exact_lhs(tri, lw)
        c_end = cum[L - 1:L, :]
        e_neg = jnp.exp(-cum)
        e_end = jnp.exp(c_end - cum)
        at_s = stack(a * jnp.exp(cum - lw))
        rt_s = stack(r * jnp.exp(cum))
        bt_s = stack(b * e_neg)
        kt_s = stack(k * e_neg)
        v_s = stack(v)
        S = s_sc[p]

        n_ab = jnp.where(strict, mm(at_s, bt_s, "nt"), 0.0)
        a_ak = jnp.where(strict, mm(at_s, kt_s, "nt"), 0.0)
        a_rb = jnp.where(incl, mm(rt_s, bt_s, "nt"), 0.0)
        a_rk = jnp.where(incl, mm(rt_s, kt_s, "nt"), 0.0)

        t_inv = eye + n_ab
        pw = n_ab
        for _ in range(int(math.log2(L)) - 1):
            pw = mm(pw, pw, "nn")
            t_inv = t_inv + mm(t_inv, pw, "nn")

        x_s = mm(at_s, S, "nt") + mm(a_ak, v_s, "nn")
        u_s = mm(t_inv, x_s, "nn")
        y_s = mm(rt_s, S, "nt") + mm(a_rb, u_s, "nn") + mm(a_rk, v_s, "nn")
        s_sc[p] = S * jnp.exp(c_end) + mm(u_s, stack(b * e_end), "tn") + mm(v_s, stack(k * e_end), "tn")

        y = y_s[:L] + y_s[L:]
        mean = segsum(y) * (1.0 / RWKV_HEAD)
        yc = y - mean
        var = segsum(yc * yc) * (1.0 / RWKV_HEAD)
        vec = vec_ref[:, sl]
        yn = yc * lax.rsqrt(var + GN_EPS) * vec[0:1] + vec[1:2]
        bonus = segsum(r * k * vec[2:3]) * v
        y_ref[0, :, sl] = ((yn + bonus) * g_ref[0, :, sl]).astype(y_ref.dtype)

    @pl.when(c == pl.num_programs(2) - 1)
    def _():
        so_ref[0] = s_sc[...]


def _wkv(r, lw, k, v, a, b, g, vec, s0, L, P, passes):
    B, T, _ = r.shape
    npair = D_MODEL // LANES
    width = P * LANES
    tile = lambda: pl.BlockSpec((1, L, width), lambda bb, pg, c: (bb, c, pg))
    state = lambda: pl.BlockSpec((1, P, LANES, LANES), lambda bb, pg, c: (bb, pg, 0, 0))
    return pl.pallas_call(
        functools.partial(_wkv_kernel, L=L, P=P, passes=passes),
        grid=(B, npair // P, T // L),
        in_specs=[tile() for _ in range(7)] + [pl.BlockSpec((8, width), lambda bb, pg, c: (0, pg)), state()],
        out_specs=[tile(), state()],
        out_shape=[jax.ShapeDtypeStruct((B, T, D_MODEL), BF16),
                   jax.ShapeDtypeStruct((B, npair, LANES, LANES), F32)],
        scratch_shapes=[pltpu.VMEM((P, LANES, LANES), F32)],
        compiler_params=_params(("arbitrary", "arbitrary", "arbitrary")),
        name="wkv_chunked",
    )(r, lw, k, v, a, b, g, vec, s0)


def _state_to_pairs(s):
    B = s.shape[0]
    s = s.reshape(B, -1, 2, RWKV_HEAD, RWKV_HEAD)
    z = jnp.zeros_like(s[:, :, 0])
    top = jnp.concatenate([s[:, :, 0], z], axis=-1)
    bot = jnp.concatenate([z, s[:, :, 1]], axis=-1)
    return jnp.concatenate([top, bot], axis=-2)


def _state_from_pairs(sp):
    B = sp.shape[0]
    h = RWKV_HEAD
    return jnp.stack([sp[:, :, :h, :h], sp[:, :, h:, h:]], axis=2).reshape(B, -1, h, h)


def _rope_tables(pos):
    half = ATTN_HEAD // 2
    inv = jnp.power(ROPE_THETA, -jnp.arange(half, dtype=F32) * 2.0 / ATTN_HEAD)
    ang = pos.astype(F32)[:, None] * inv[None, :]
    cos = jnp.cos(ang)
    sin = jnp.sin(ang)
    reps = LANES // ATTN_HEAD
    return jnp.tile(cos, (1, 2 * reps)), jnp.tile(jnp.concatenate([-sin, sin], axis=1), (1, reps))


def _pad_cols(w):
    return jnp.pad(w, ((0, 0), (0, LORA_PAD - w.shape[1])))


def _pad_rows(w):
    return jnp.pad(w, ((0, LORA_PAD - w.shape[0]), (0, 0)))


def _pad_tokens(z, t):
    return jnp.pad(z, ((0, 0), (0, t - z.shape[1]), (0, 0)))


def _trunk(x, pos, decode, shift0, wkv0, mem_k, mem_v, past, W):
    B, T, _ = x.shape
    if decode:
        xf = x.reshape(1, B, D_MODEL)
        tm = B
    else:
        xf = x
        tm = 512
    cos_t, sin_t = _rope_tables(pos if not decode else jnp.broadcast_to(pos, (B,)))
    shifts, states, ks, vs = [], [], [], []
    v_first = None
    for l in range(DEPTH):
        idx = l // 2
        if l % 2 == 0:
            vec = jnp.concatenate([
                W["mu"][idx], W["w0"][idx][None], W["a0"][idx][None],
                (W["v0"][idx - 1] if idx > 0 else jnp.zeros((D_MODEL,), F32))[None],
                W["k_k"][idx][None], W["k_a"][idx][None], W["mix_g"][l][None],
                jnp.zeros((_N_VEC - 12, D_MODEL), F32)], axis=0)
            loras = (W["w1"][idx], W["w2"][idx], W["a1"][idx], W["a2"][idx],
                     W["v1"][idx - 1] if idx > 0 else None, W["v2"][idx - 1] if idx > 0 else None,
                     W["g1"][idx], W["g2"][idx])
            if decode:
                outs = _rwkv_proj(xf, shift0[idx][None], None, v_first, vec, W["w_in"][l], loras, tm)
            else:
                outs = _rwkv_proj(xf, xf, shift0[idx][:, None], v_first, vec, W["w_in"][l], loras, tm // 2)
            r, lw, k, v, a, b, g, mq, xs = outs
            if idx == 0:
                v_first = v
            shifts.append(xs.reshape(B, D_MODEL))
            scan_in = [r, lw, k, v, a, b, g]
            if decode:
                scan_in = [_pad_tokens(z.reshape(B, 1, D_MODEL), WKV_CHUNK) for z in scan_in]
            vec2 = jnp.concatenate([W["lnx_g"][idx][None], W["lnx_b"][idx][None], W["r_k"][idx][None],
                                    jnp.zeros((5, D_MODEL), F32)], axis=0)
            y, s_new = _wkv(*scan_in, vec2, _state_to_pairs(wkv0[idx]), WKV_CHUNK, 4, 1)
            states.append(_state_from_pairs(s_new))
            y_tok = y[:, :1].reshape(1, B, D_MODEL) if decode else y
        else:
            lam_init = 0.8 - 0.6 * math.exp(-0.3 * l)
            q, kf, kb, vf, vb, mq = _diff_proj(xf, W["mix_g"][l][None], W["w_in"][l], cos_t, sin_t, tm)
            lam_p = jnp.stack([W["lam_q1"][idx], W["lam_k1"][idx], W["lam_q2"][idx], W["lam_k2"][idx]])
            sg = W["subln_g"][idx][None]
            if decode:
                cache_k, cache_v, page_table = past
                hd = lambda z: z.reshape(B, ATTN_HEADS, LANES)
                o = _dec_attn(page_table, hd(q), cache_k, cache_v, hd(kf), hd(vf), lam_p, sg, lam_init, idx)
                y_tok = o.reshape(1, B, D_MODEL)
            else:
                y_tok = _flash(q, kb, vb, lam_p, sg, lam_init, 256, 512)
            ks.append(kf.reshape(B, T, ATTN_HEADS, LANES))
            vs.append(vf.reshape(B, T, ATTN_HEADS, LANES))
        if decode:
            mq8 = _pad_tokens(mq.reshape(B, 1, MEM_WIDTH), 8)
            y_mem = _mem_attend(mq8, mem_k[l], mem_v[l], 8)[:, :1].reshape(1, B, MEM_WIDTH)
        else:
            y_mem = _mem_attend(mq, mem_k[l], mem_v[l], tm)
        xf = _out_ffn(xf, y_tok, y_mem, W["w_out"][l], W["ffn_g"][l][None], W["w_gate"][l], W["w_up"][l],
                      W["w_down"][l], W["final_g"][None], tm, l == DEPTH - 1)
    return xf.reshape(B, T, D_MODEL), jnp.stack(shifts), jnp.stack(states), jnp.stack(ks), jnp.stack(vs)


def kernel(x_prompt, x_sample, cache_k, cache_v, cache_mem_k, cache_mem_v, state_rwkv_wkv, state_rwkv_shift, page_table, mem_prompt, w_in, w_out, mix_norm_g, ffn_norm_g, w_gate, w_up, w_down, final_norm_g, mem_norm_g, w_mem_k, w_mem_v, rwkv_mu, rwkv_w0, rwkv_w1, rwkv_w2, rwkv_a0, rwkv_a1, rwkv_a2, rwkv_v0, rwkv_v1, rwkv_v2, rwkv_g1, rwkv_g2, rwkv_k_k, rwkv_k_a, rwkv_r_k, rwkv_lnx_g, rwkv_lnx_b, diff_lam_q1, diff_lam_k1, diff_lam_q2, diff_lam_k2, diff_subln_g):
    bf = lambda z: z.astype(BF16)
    n_rwkv = rwkv_mu.shape[0]
    W = dict(
        w_in=bf(w_in), w_out=bf(w_out), mix_g=mix_norm_g, ffn_g=ffn_norm_g,
        w_gate=bf(w_gate), w_up=bf(w_up), w_down=bf(w_down), final_g=final_norm_g,
        mu=rwkv_mu, w0=rwkv_w0, a0=rwkv_a0, v0=rwkv_v0, k_k=rwkv_k_k, k_a=rwkv_k_a,
        w1=[bf(_pad_cols(rwkv_w1[i])) for i in range(n_rwkv)],
        w2=[bf(_pad_rows(rwkv_w2[i])) for i in range(n_rwkv)],
        a1=[bf(_pad_cols(rwkv_a1[i])) for i in range(n_rwkv)],
        a2=[bf(_pad_rows(rwkv_a2[i])) for i in range(n_rwkv)],
        v1=[bf(_pad_cols(rwkv_v1[i])) for i in range(n_rwkv - 1)],
        v2=[bf(_pad_rows(rwkv_v2[i])) for i in range(n_rwkv - 1)],
        g1=bf(rwkv_g1), g2=bf(rwkv_g2),
        r_k=rwkv_r_k.reshape(n_rwkv, D_MODEL), lnx_g=rwkv_lnx_g, lnx_b=rwkv_lnx_b,
        lam_q1=diff_lam_q1, lam_k1=diff_lam_k1, lam_q2=diff_lam_q2, lam_k2=diff_lam_k2,
        subln_g=diff_subln_g,
    )

    B, T, _ = x_prompt.shape
    M = mem_prompt.shape[1]
    mk, mv = _mem_kv(mem_prompt.reshape(B * M, D_MODEL), mem_norm_g[:, None], bf(w_mem_k), bf(w_mem_v))
    mk = mk.reshape(DEPTH, B, M, MEM_WIDTH)
    mv = mv.reshape(DEPTH, B, M, MEM_WIDTH)
    p_mem_k = mk.reshape(DEPTH, B, M, MEM_HEADS, MEM_HEAD)
    p_mem_v = mv.reshape(DEPTH, B, M, MEM_HEADS, MEM_HEAD)

    pos_p = jnp.arange(T, dtype=jnp.int32)
    shift0 = jnp.zeros((n_rwkv, B, D_MODEL), F32)
    wkv0 = jnp.zeros((n_rwkv, B, D_MODEL // RWKV_HEAD, RWKV_HEAD, RWKV_HEAD), F32)
    y_prompt, p_shift, p_wkv, p_k, p_v = _trunk(x_prompt, pos_p, False, shift0, wkv0, mk, mv, None, W)

    Bs = x_sample.shape[0]
    past_len = page_table.shape[1] * PAGE_SIZE
    pos_s = past_len + jnp.arange(x_sample.shape[1], dtype=jnp.int32)
    smk = cache_mem_k.reshape(DEPTH, Bs, -1, MEM_WIDTH)
    smv = cache_mem_v.reshape(DEPTH, Bs, -1, MEM_WIDTH)
    y_sample, s_shift, s_wkv, s_k, s_v = _trunk(
        x_sample, pos_s, True, state_rwkv_shift, state_rwkv_wkv, smk, smv, (cache_k, cache_v, page_table), W)

    return (y_prompt, y_sample, p_wkv, p_shift, p_k, p_v, p_mem_k, p_mem_v, s_wkv, s_shift, s_k, s_v)
```

```python
import functools
import math

import jax
import jax.numpy as jnp
from jax import lax
from jax.experimental import pallas as pl
from jax.experimental.pallas import tpu as pltpu

F32 = jnp.float32
BF16 = jnp.bfloat16

D_MODEL = 1024
DEPTH = 4
PAGE_SIZE = 128
RWKV_HEAD = 64
ATTN_HEAD = 64
ATTN_HEADS = D_MODEL // (2 * ATTN_HEAD)
MEM_HEADS = 4
MEM_HEAD = 128
MEM_WIDTH = MEM_HEADS * MEM_HEAD
D_FF = 2816
ROPE_THETA = 10000.0
NORM_EPS = 1e-6
SUBLN_EPS = 1e-5
GN_EPS = 1e-5 * RWKV_HEAD

Q_SCALE = ATTN_HEAD ** -0.5 * math.log2(math.e)
LANES = 128
LORA_PAD = 128
WKV_CHUNK = 64
WKV_PAIRS = 8
DEC_PAGES = 8
VMEM_LIMIT = 56 * 1024 * 1024

_NT = (((1,), (1,)), ((), ()))
_TN = (((0,), (0,)), ((), ()))
_NEG = -1e30


def _dot(a, b):
    return jnp.dot(a, b, preferred_element_type=F32)


def _dot_nt(a, b):
    return lax.dot_general(a, b, _NT, preferred_element_type=F32)


def _dot_tn(a, b):
    return lax.dot_general(a, b, _TN, preferred_element_type=F32)


def _split2(x):
    hi = x.astype(BF16)
    lo = (x - hi.astype(F32)).astype(BF16)
    return hi, lo


def _split3(x):
    h1 = x.astype(BF16)
    r1 = x - h1.astype(F32)
    h2 = r1.astype(BF16)
    h3 = (r1 - h2.astype(F32)).astype(BF16)
    return h1, h2, h3


def _dot_exact_rhs(x, m_bf16):
    h1, h2, h3 = _split3(x)
    return _dot(h1, m_bf16) + _dot(h2, m_bf16) + _dot(h3, m_bf16)


def _dot_exact_lhs(m_bf16, x):
    h1, h2, h3 = _split3(x)
    return _dot(m_bf16, h1) + _dot(m_bf16, h2) + _dot(m_bf16, h3)


def _mm(a, b, kind, passes):
    f = {"nn": _dot, "nt": _dot_nt, "tn": _dot_tn}[kind]
    if passes == 1:
        return f(a.astype(BF16), b.astype(BF16))
    ah, al = _split2(a)
    bh, bl = _split2(b)
    return f(ah, bh) + f(ah, bl) + f(al, bh)


def _rms(x, g):
    ms = jnp.mean(x * x, axis=-1, keepdims=True)
    return x * lax.rsqrt(ms + NORM_EPS) * g


def _sigmoid(x):
    return 1.0 / (1.0 + jnp.exp(-x))


def _softplus(x):
    return jnp.maximum(x, 0.0) + jnp.log(1.0 + jnp.exp(-jnp.abs(x)))


def _seg_ones(width):
    r = lax.broadcasted_iota(jnp.int32, (LANES, LANES), 0) // width
    c = lax.broadcasted_iota(jnp.int32, (LANES, LANES), 1) // width
    return jnp.where(r == c, 1.0, 0.0).astype(BF16)


def _resident(shape):
    nd = len(shape)
    return pl.BlockSpec(shape, lambda *_: (0,) * nd, pipeline_mode=pl.Buffered(1))


def _params(sem):
    return pltpu.CompilerParams(dimension_semantics=sem, vmem_limit_bytes=VMEM_LIMIT)


def _mem_kv_kernel(mem_ref, g_ref, wk_ref, wv_ref, k_ref, v_ref):
    mn = _rms(mem_ref[...], g_ref[0]).astype(BF16)
    k_ref[0] = _dot(mn, wk_ref[0])
    v_ref[0] = _dot(mn, wv_ref[0])


def _mem_kv(mem2d, g, wk, wv):
    rows = mem2d.shape[0]
    out = jax.ShapeDtypeStruct((DEPTH, rows, MEM_WIDTH), F32)
    return pl.pallas_call(
        _mem_kv_kernel,
        grid=(DEPTH,),
        in_specs=[
            pl.BlockSpec((rows, D_MODEL), lambda l: (0, 0)),
            pl.BlockSpec((1, 1, D_MODEL), lambda l: (l, 0, 0)),
            pl.BlockSpec((1, D_MODEL, MEM_WIDTH), lambda l: (l, 0, 0)),
            pl.BlockSpec((1, D_MODEL, MEM_WIDTH), lambda l: (l, 0, 0)),
        ],
        out_specs=[pl.BlockSpec((1, rows, MEM_WIDTH), lambda l: (l, 0, 0))] * 2,
        out_shape=[out, out],
        compiler_params=_params(("arbitrary",)),
        name="mem_kv",
    )(mem2d, g, wk, wv)


def _mem_attend_kernel(q_ref, k_ref, v_ref, o_ref):
    q = q_ref[0]
    k = k_ref[0].astype(BF16)
    v = v_ref[0].astype(BF16)
    outs = []
    for h in range(MEM_HEADS):
        sl = slice(h * MEM_HEAD, (h + 1) * MEM_HEAD)
        s = _dot_nt(q[:, sl], k[:, sl])
        m = jnp.max(s, axis=-1, keepdims=True)
        p = jnp.exp(s - m)
        l = jnp.sum(p, axis=-1, keepdims=True)
        outs.append(_dot(p.astype(BF16), v[:, sl]) / l)
    o_ref[0] = jnp.concatenate(outs, axis=-1).astype(o_ref.dtype)


def _mem_attend(q, mk, mv, tq):
    B, T, _ = q.shape
    M = mk.shape[1]
    return pl.pallas_call(
        _mem_attend_kernel,
        grid=(B, T // tq),
        in_specs=[
            pl.BlockSpec((1, tq, MEM_WIDTH), lambda b, i: (b, i, 0)),
            pl.BlockSpec((1, M, MEM_WIDTH), lambda b, i: (b, 0, 0)),
            pl.BlockSpec((1, M, MEM_WIDTH), lambda b, i: (b, 0, 0)),
        ],
        out_specs=pl.BlockSpec((1, tq, MEM_WIDTH), lambda b, i: (b, i, 0)),
        out_shape=jax.ShapeDtypeStruct((B, T, MEM_WIDTH), BF16),
        compiler_params=_params(("arbitrary", "arbitrary")),
        name="mem_attend",
    )(q, mk, mv)


def _out_ffn_kernel(x_ref, yt_ref, ym_ref, wo_ref, g_ref, wg_ref, wu_ref, wd_ref, fg_ref, o_ref,
                    *, final, ft):
    x1 = x_ref[0] + (_dot(yt_ref[0], wo_ref[:D_MODEL, :]) + _dot(ym_ref[0], wo_ref[D_MODEL:, :]))
    h = _rms(x1, g_ref[...]).astype(BF16)
    acc = jnp.zeros_like(x1)
    for f in range(0, D_FF, ft):
        gt = _dot(h, wg_ref[:, f:f + ft])
        up = _dot(h, wu_ref[:, f:f + ft])
        act = (gt * _sigmoid(gt) * up).astype(BF16)
        acc = acc + _dot(act, wd_ref[f:f + ft, :])
    acc = x1 + acc
    if final:
        acc = _rms(acc, fg_ref[...])
    o_ref[0] = acc


def _out_ffn(x, ytok, ymem, wo, g, wg, wu, wd, fg, tm, final):
    B, T, _ = x.shape
    row = lambda w: pl.BlockSpec((1, tm, w), lambda b, i: (b, i, 0))
    return pl.pallas_call(
        functools.partial(_out_ffn_kernel, final=final, ft=256),
        grid=(B, T // tm),
        in_specs=[
            row(D_MODEL), row(D_MODEL), row(MEM_WIDTH),
            _resident((D_MODEL + MEM_WIDTH, D_MODEL)),
            _resident((1, D_MODEL)),
            _resident((D_MODEL, D_FF)), _resident((D_MODEL, D_FF)), _resident((D_FF, D_MODEL)),
            _resident((1, D_MODEL)),
        ],
        out_specs=row(D_MODEL),
        out_shape=jax.ShapeDtypeStruct((B, T, D_MODEL), F32),
        compiler_params=_params(("arbitrary", "arbitrary")),
        name="out_ffn",
    )(x, ytok, ymem, wo, g, wg, wu, wd, fg)


def _diff_proj_kernel(x_ref, g_ref, w_ref, cos_ref, sin_ref,
                      q_ref, k_ref, kb_ref, v_ref, vb_ref, mq_ref):
    xn = _rms(x_ref[0], g_ref[...]).astype(BF16)
    tm = xn.shape[0]
    cos = cos_ref[...]
    sin = sin_ref[...]
    lane = lax.broadcasted_iota(jnp.int32, (tm, LANES), 1)
    first = (lane % ATTN_HEAD) < (ATTN_HEAD // 2)

    def rope(z):
        rot = jnp.where(first, pltpu.roll(z, LANES - ATTN_HEAD // 2, 1), pltpu.roll(z, ATTN_HEAD // 2, 1))
        return z * cos + rot * sin

    half = D_MODEL // 2
    for c in range(2):
        z = _dot(xn, w_ref[:, c * half:(c + 1) * half])
        for j in range(half // LANES):
            col = c * half + j * LANES
            q_ref[0, :, col:col + LANES] = (rope(z[:, j * LANES:(j + 1) * LANES]) * Q_SCALE).astype(BF16)
    for c in range(2):
        z = _dot(xn, w_ref[:, D_MODEL + c * half:D_MODEL + (c + 1) * half])
        for j in range(half // LANES):
            col = c * half + j * LANES
            kr = rope(z[:, j * LANES:(j + 1) * LANES])
            k_ref[0, :, col:col + LANES] = kr
            kb_ref[0, :, col:col + LANES] = kr.astype(BF16)
    for c in range(2):
        z = _dot(xn, w_ref[:, 2 * D_MODEL + c * half:2 * D_MODEL + (c + 1) * half])
        v_ref[0, :, c * half:(c + 1) * half] = z
        vb_ref[0, :, c * half:(c + 1) * half] = z.astype(BF16)
    z = _dot(xn, w_ref[:, 3 * D_MODEL:])
    mq_ref[0] = (z * (MEM_HEAD ** -0.5)).astype(BF16)


def _diff_proj(x, g, w, cos_t, sin_t, tm):
    B, T, _ = x.shape
    row = lambda w_: pl.BlockSpec((1, tm, w_), lambda b, i: (b, i, 0))
    sd = lambda w_, dt: jax.ShapeDtypeStruct((B, T, w_), dt)
    return pl.pallas_call(
        _diff_proj_kernel,
        grid=(B, T // tm),
        in_specs=[
            row(D_MODEL), _resident((1, D_MODEL)), _resident((D_MODEL, 3 * D_MODEL + MEM_WIDTH)),
            pl.BlockSpec((tm, LANES), lambda b, i: (i, 0)),
            pl.BlockSpec((tm, LANES), lambda b, i: (i, 0)),
        ],
        out_specs=[row(D_MODEL)] * 5 + [row(MEM_WIDTH)],
        out_shape=[sd(D_MODEL, BF16), sd(D_MODEL, F32), sd(D_MODEL, BF16), sd(D_MODEL, F32),
                   sd(D_MODEL, BF16), sd(MEM_WIDTH, BF16)],
        compiler_params=_params(("arbitrary", "arbitrary")),
        name="diff_proj",
    )(x, g, w, cos_t, sin_t)


def _lambda(lam_ref, lam_init):
    lp = lam_ref[...]
    s1 = jnp.sum(lp[0:1] * lp[1:2], axis=-1, keepdims=True)
    s2 = jnp.sum(lp[2:3] * lp[3:4], axis=-1, keepdims=True)
    return jnp.exp(s1) - jnp.exp(s2) + lam_init


def _subln(o, sg, lam_init):
    return o * lax.rsqrt(jnp.mean(o * o, axis=-1, keepdims=True) + SUBLN_EPS) * sg * (1.0 - lam_init)


def _flash_kernel(q_ref, k_ref, v_ref, lam_ref, sg_ref, o_ref, s_sc, p_sc, m_sc, l_sc, acc_sc,
                  *, tq, tk, lam_init):
    qi = pl.program_id(2)
    q = q_ref[0]
    lane = lax.broadcasted_iota(jnp.int32, (tq, LANES), 1)
    zero = jnp.zeros_like(q)
    qs = jnp.concatenate([jnp.where(lane < ATTN_HEAD, q, zero), jnp.where(lane < ATTN_HEAD, zero, q)], axis=0)
    rows = 2 * tq
    nfull = (qi * tq) // tk

    def kv(ref, i):
        return ref[0, pl.ds(pl.multiple_of(i * tk, tk), tk), :]

    def softmax_update(s, pv):
        m_old = m_sc[...]
        m_new = jnp.maximum(m_old, jnp.max(s, axis=-1, keepdims=True))
        alpha = jnp.exp2(m_old - m_new)
        p = jnp.exp2(s - m_new)
        l_sc[...] = alpha * l_sc[...] + jnp.sum(p, axis=-1, keepdims=True)
        acc_sc[...] = alpha * (acc_sc[...] + pv)
        m_sc[...] = m_new
        return p.astype(BF16)

    m_sc[...] = jnp.full(m_sc.shape, _NEG, F32)
    l_sc[...] = jnp.zeros(l_sc.shape, F32)
    acc_sc[...] = jnp.zeros(acc_sc.shape, F32)
    p_sc[1] = jnp.zeros(p_sc.shape[1:], BF16)
    s_sc[0] = _dot_nt(qs, kv(k_ref, 0))

    def stage(cur, k):
        s_sc[1 - cur] = _dot_nt(qs, kv(k_ref, k + 1))
        pv = _dot(p_sc[1 - cur], kv(v_ref, jnp.maximum(k - 1, 0)))
        p_sc[cur] = softmax_update(s_sc[cur], pv)

    def body(j, carry):
        stage(0, 2 * j)
        stage(1, 2 * j + 1)
        return carry

    lax.fori_loop(0, nfull // 2, body, 0)

    def finish(cur):
        pv = _dot(p_sc[1 - cur], kv(v_ref, jnp.maximum(nfull - 1, 0)))
        r = lax.broadcasted_iota(jnp.int32, (rows, tk), 0)
        qpos = jnp.where(r >= tq, r - tq, r) + qi * tq
        kpos = lax.broadcasted_iota(jnp.int32, (rows, tk), 1) + nfull * tk
        p = softmax_update(jnp.where(kpos <= qpos, s_sc[cur], _NEG), pv)
        on = (acc_sc[...] + _dot(p, kv(v_ref, nfull))) / l_sc[...]
        lam = _lambda(lam_ref, lam_init)
        o = on[:tq] - lam * on[tq:]
        o_ref[0] = _subln(o, sg_ref[...], lam_init).astype(o_ref.dtype)

    @pl.when(nfull % 2 == 0)
    def _():
        finish(0)

    @pl.when(nfull % 2 == 1)
    def _():
        stage(0, nfull - 1)
        finish(1)


def _flash(q, k, v, lam_p, sg, lam_init, tq, tk):
    B, T, _ = q.shape
    rows = 2 * tq
    return pl.pallas_call(
        functools.partial(_flash_kernel, tq=tq, tk=tk, lam_init=lam_init),
        grid=(B, ATTN_HEADS, T // tq),
        in_specs=[
            pl.BlockSpec((1, tq, LANES), lambda b, h, i: (b, i, h)),
            pl.BlockSpec((1, T, LANES), lambda b, h, i: (b, 0, h)),
            pl.BlockSpec((1, T, LANES), lambda b, h, i: (b, 0, h)),
            _resident((4, ATTN_HEAD)), _resident((1, LANES)),
        ],
        out_specs=pl.BlockSpec((1, tq, LANES), lambda b, h, i: (b, i, h)),
        out_shape=jax.ShapeDtypeStruct((B, T, D_MODEL), BF16),
        scratch_shapes=[pltpu.VMEM((2, rows, tk), F32), pltpu.VMEM((2, rows, tk), BF16),
                        pltpu.VMEM((rows, 1), F32), pltpu.VMEM((rows, 1), F32), pltpu.VMEM((rows, LANES), F32)],
        compiler_params=_params(("arbitrary", "arbitrary", "arbitrary")),
        name="diff_flash",
    )(q, k, v, lam_p, sg)


def _dec_attn_kernel(*refs, lam_init, n_pg):
    pt_ref, q_ref = refs[0], refs[1]
    k_refs = refs[2:2 + n_pg]
    v_refs = refs[2 + n_pg:2 + 2 * n_pg]
    kn_ref, vn_ref, lam_ref, sg_ref, o_ref, m_sc, l_sc, acc_sc = refs[2 + 2 * n_pg:]
    del pt_ref
    j = pl.program_id(1)
    H = ATTN_HEADS
    G = 2 * H

    @pl.when(j == 0)
    def _():
        m_sc[...] = jnp.full(m_sc.shape, _NEG, F32)
        l_sc[...] = jnp.zeros(l_sc.shape, F32)
        acc_sc[...] = jnp.zeros(acc_sc.shape, F32)

    q8 = q_ref[0]
    lane = lax.broadcasted_iota(jnp.int32, (H, LANES), 1)
    zero = jnp.zeros_like(q8)
    q16 = jnp.concatenate([jnp.where(lane < ATTN_HEAD, q8, zero), jnp.where(lane < ATTN_HEAD, zero, q8)], axis=0)

    n = PAGE_SIZE * H
    same_head = (lax.broadcasted_iota(jnp.int32, (G, n), 0) % H) == (lax.broadcasted_iota(jnp.int32, (G, n), 1) % H)
    s = [jnp.where(same_head, _dot_nt(q16, kr[...].reshape(n, LANES).astype(BF16)), _NEG) for kr in k_refs]
    m_old = m_sc[...]
    m_new = m_old
    for z in s:
        m_new = jnp.maximum(m_new, jnp.max(z, axis=-1, keepdims=True))
    alpha = jnp.exp2(m_old - m_new)
    p = [jnp.exp2(z - m_new) for z in s]
    l_sc[...] = alpha * l_sc[...] + sum(jnp.sum(z, axis=-1, keepdims=True) for z in p)
    pv = sum(_dot(z.astype(BF16), vr[...].reshape(n, LANES).astype(BF16)) for z, vr in zip(p, v_refs))
    acc_sc[...] = alpha * acc_sc[...] + pv
    m_sc[...] = m_new

    @pl.when(j == pl.num_programs(1) - 1)
    def _():
        kn = kn_ref[0]
        vn = vn_ref[0]
        kn16 = jnp.concatenate([kn, kn], axis=0)
        vn16 = jnp.concatenate([vn, vn], axis=0)
        sn = jnp.sum(q16.astype(F32) * kn16, axis=-1, keepdims=True)
        m0 = m_sc[...]
        m1 = jnp.maximum(m0, sn)
        a0 = jnp.exp2(m0 - m1)
        pn = jnp.exp2(sn - m1)
        l1 = a0 * l_sc[...] + pn
        acc1 = a0 * acc_sc[...] + pn * vn16
        on = acc1 / l1
        lam = _lambda(lam_ref, lam_init)
        o = on[:H] - lam * on[H:]
        o_ref[0] = _subln(o, sg_ref[...], lam_init).astype(o_ref.dtype)


def _dec_attn(page_table, q, cache_k, cache_v, kn, vn, lam_p, sg, lam_init, layer):
    B, n_pages = page_table.shape
    H = ATTN_HEADS
    n_pg = math.gcd(DEC_PAGES, n_pages)
    head = lambda: pl.BlockSpec((1, H, LANES), lambda b, j, pt: (b, 0, 0))
    page = lambda i: pl.BlockSpec((None, None, PAGE_SIZE, H, LANES),
                                  lambda b, j, pt: (layer, pt[b, j * n_pg + i], 0, 0, 0))
    pages = [page(i) for i in range(n_pg)]
    grid_spec = pltpu.PrefetchScalarGridSpec(
        num_scalar_prefetch=1,
        grid=(B, n_pages // n_pg),
        in_specs=[head()] + pages + pages + [head(), head(), _resident((4, ATTN_HEAD)), _resident((1, LANES))],
        out_specs=pl.BlockSpec((1, H, LANES), lambda b, j, pt: (b, 0, 0)),
        scratch_shapes=[pltpu.VMEM((2 * H, 1), F32), pltpu.VMEM((2 * H, 1), F32), pltpu.VMEM((2 * H, LANES), F32)],
    )
    return pl.pallas_call(
        functools.partial(_dec_attn_kernel, lam_init=lam_init, n_pg=n_pg),
        grid_spec=grid_spec,
        out_shape=jax.ShapeDtypeStruct((B, H, LANES), BF16),
        compiler_params=_params(("arbitrary", "arbitrary")),
        name="paged_diff_attn",
    )(page_table, q, *([cache_k] * n_pg), *([cache_v] * n_pg), kn, vn, lam_p, sg)


_V_MU, _V_W0, _V_A0, _V_V0, _V_KK, _V_KA, _V_G = 0, 6, 7, 8, 9, 10, 11
_N_VEC = 16


def _rwkv_proj_kernel(*refs, shifted, has_vfirst):
    it = iter(refs)
    x_ref = next(it)
    prev_ref = next(it)
    shift_ref = next(it) if shifted else None
    vf_ref = next(it) if has_vfirst else None
    vec_ref, w_ref, w1_ref, w2_ref, a1_ref, a2_ref = (next(it) for _ in range(6))
    v1_ref, v2_ref = (next(it), next(it)) if has_vfirst else (None, None)
    g1_ref, g2_ref = next(it), next(it)
    r_ref, lw_ref, k_ref, v_ref, a_ref, b_ref, g_ref, mq_ref, xs_ref = (next(it) for _ in range(9))

    vec = vec_ref[...]
    row = lambda i: vec[i:i + 1]
    gain = row(_V_G)
    xn = _rms(x_ref[0], gain)
    tm = xn.shape[0]
    if shifted:
        pr = _rms(prev_ref[0][7:8], gain)
        pr = jnp.where(pl.program_id(1) == 0, shift_ref[0], pr)
        ridx = lax.broadcasted_iota(jnp.int32, (tm, 1), 0)
        xprev = jnp.where(ridx == 0, pr, pltpu.roll(xn, 1, 0))
        xs_ref[0] = xn[tm - 1:tm]
    else:
        xprev = prev_ref[0]
        xs_ref[0] = xn
    xx = xprev - xn
    mix = lambda j: (xn + xx * row(_V_MU + j)).astype(BF16)
    xr, xw, xk, xv, xa, xg = (mix(j) for j in range(6))

    D = D_MODEL
    r = _dot(xr, w_ref[:, :D])
    k = _dot(xk, w_ref[:, D:2 * D])
    v = _dot(xv, w_ref[:, 2 * D:3 * D])
    mq_ref[0] = (_dot(xn.astype(BF16), w_ref[:, 3 * D:]) * (MEM_HEAD ** -0.5)).astype(BF16)

    w_in = row(_V_W0) + _dot(jnp.tanh(_dot(xw, w1_ref[...])).astype(BF16), w2_ref[...])
    w_log = -_softplus(-w_in) - 0.5
    lw_ref[0] = -jnp.exp(w_log)
    if has_vfirst:
        gate = _sigmoid(row(_V_V0) + _dot(_dot(xv, v1_ref[...]).astype(BF16), v2_ref[...]))
        v = v + (vf_ref[0] - v) * gate
    a = _sigmoid(row(_V_A0) + _dot(_dot(xa, a1_ref[...]).astype(BF16), a2_ref[...]))
    g_ref[0] = _dot(_sigmoid(_dot(xg, g1_ref[...])).astype(BF16), g2_ref[...])

    seg = _seg_ones(RWKV_HEAD)
    kk = k * row(_V_KK)
    sq = kk * kk
    n2 = jnp.concatenate(
        [sum(_dot(p_, seg) for p_ in _split2(sq[:, c:c + LANES])) for c in range(0, D, LANES)], axis=-1)
    kk = kk / jnp.maximum(jnp.sqrt(n2), 1e-12)
    r_ref[0] = r
    k_ref[0] = k * (1.0 + (a - 1.0) * row(_V_KA))
    v_ref[0] = v
    a_ref[0] = -kk
    b_ref[0] = kk * a


def _rwkv_proj(x, prev, shift, vfirst, vec, w, loras, tm):
    B, T, _ = x.shape
    shifted = shift is not None
    has_vfirst = vfirst is not None
    row = lambda w_: pl.BlockSpec((1, tm, w_), lambda b, i: (b, i, 0))
    ins, specs = [x], [row(D_MODEL)]
    if shifted:
        ins += [prev, shift]
        specs += [pl.BlockSpec((1, 8, D_MODEL), lambda b, i: (b, jnp.maximum(i * (tm // 8) - 1, 0), 0)),
                  pl.BlockSpec((1, 1, D_MODEL), lambda b, i: (b, 0, 0))]
    else:
        ins += [prev]
        specs += [row(D_MODEL)]
    if has_vfirst:
        ins.append(vfirst)
        specs.append(row(D_MODEL))
    w1, w2, a1, a2, v1, v2, g1, g2 = loras
    small = [vec, w, w1, w2, a1, a2] + ([v1, v2] if has_vfirst else []) + [g1, g2]
    ins += small
    specs += [_resident(s.shape) for s in small]
    sd = lambda dt, w_=D_MODEL, t_=T: jax.ShapeDtypeStruct((B, t_, w_), dt)
    xs_rows = 1 if shifted else T
    xs_spec = (pl.BlockSpec((1, 1, D_MODEL), lambda b, i: (b, 0, 0)) if shifted else row(D_MODEL))
    return pl.pallas_call(
        functools.partial(_rwkv_proj_kernel, shifted=shifted, has_vfirst=has_vfirst),
        grid=(B, T // tm),
        in_specs=specs,
        out_specs=[row(D_MODEL)] * 7 + [row(MEM_WIDTH), xs_spec],
        out_shape=[sd(F32)] * 7 + [sd(BF16, MEM_WIDTH), sd(F32, D_MODEL, xs_rows)],
        compiler_params=_params(("arbitrary", "arbitrary")),
        name="rwkv_proj",
    )(*ins)


def _wkv_kernel(r_ref, lw_ref, k_ref, v_ref, a_ref, b_ref, g_ref, vec_ref, s0_ref, y_ref, so_ref, s_sc,
                *, L, P, passes):
    c = pl.program_id(2)

    @pl.when(c == 0)
    def _():
        s_sc[...] = s0_ref[0]

    L2 = 2 * L
    tri = jnp.where(lax.broadcasted_iota(jnp.int32, (L, L), 0) >= lax.broadcasted_iota(jnp.int32, (L, L), 1),
                    1.0, 0.0).astype(BF16)
    r2 = lax.broadcasted_iota(jnp.int32, (L2, L2), 0)
    c2 = lax.broadcasted_iota(jnp.int32, (L2, L2), 1)
    strict = (r2 % L) > (c2 % L)
    incl = (r2 % L) >= (c2 % L)
    eye = jnp.where(r2 == c2, 1.0, 0.0)
    m0 = lax.broadcasted_iota(jnp.int32, (L, LANES), 1) < RWKV_HEAD
    seg = _seg_ones(RWKV_HEAD)
    mm = functools.partial(_mm, passes=passes)

    def stack(z):
        return jnp.concatenate([jnp.where(m0, z, 0.0), jnp.where(m0, 0.0, z)], axis=0)

    def segsum(z):
        hi, lo = _split2(z)
        return _dot(hi, seg) + _dot(lo, seg)

    pairs = range(P)
    cols = [slice(p * LANES, (p + 1) * LANES) for p in pairs]
    r = [r_ref[0, :, sl] for sl in cols]
    lw = [lw_ref[0, :, sl] for sl in cols]
    k = [k_ref[0, :, sl] for sl in cols]
    v = [v_ref[0, :, sl] for sl in cols]
    a = [a_ref[0, :, sl] for sl in cols]
    b = [b_ref[0, :, sl] for sl in cols]
    cum = [_dot_exact_lhs(tri, z) for z in lw]
    c_end = [z[L - 1:L, :] for z in cum]
    e_neg = [jnp.exp(-z) for z in cum]
    at_s = [stack(a[p] * jnp.exp(cum[p] - lw[p])) for p in pairs]
    rt_s = [stack(r[p] * jnp.exp(cum[p])) for p in pairs]
    bt_s = [stack(b[p] * e_neg[p]) for p in pairs]
    kt_s = [stack(k[p] * e_neg[p]) for p in pairs]
    v_s = [stack(z) for z in v]
    S = [s_sc[p] for p in pairs]

    n_ab = [jnp.where(strict, mm(at_s[p], bt_s[p], "nt"), 0.0) for p in pairs]
    a_ak = [jnp.where(strict, mm(at_s[p], kt_s[p], "nt"), 0.0) for p in pairs]
    a_rb = [jnp.where(incl, mm(rt_s[p], bt_s[p], "nt"), 0.0) for p in pairs]
    a_rk = [jnp.where(incl, mm(rt_s[p], kt_s[p], "nt"), 0.0) for p in pairs]

    t_inv = [eye + z for z in n_ab]
    pw = n_ab
    for _ in range(int(math.log2(L)) - 1):
        pw = [mm(z, z, "nn") for z in pw]
        t_inv = [t_inv[p] + mm(t_inv[p], pw[p], "nn") for p in pairs]

    x_s = [mm(at_s[p], S[p], "nt") + mm(a_ak[p], v_s[p], "nn") for p in pairs]
    u_s = [mm(t_inv[p], x_s[p], "nn") for p in pairs]
    y_s = [mm(rt_s[p], S[p], "nt") + mm(a_rb[p], u_s[p], "nn") + mm(a_rk[p], v_s[p], "nn") for p in pairs]
    for p in pairs:
        e_end = jnp.exp(c_end[p] - cum[p])
        s_sc[p] = (S[p] * jnp.exp(c_end[p]) + mm(u_s[p], stack(b[p] * e_end), "tn")
                   + mm(v_s[p], stack(k[p] * e_end), "tn"))

    y = [z[:L] + z[L:] for z in y_s]
    yc = [z - segsum(z) * (1.0 / RWKV_HEAD) for z in y]
    var = [segsum(z * z) * (1.0 / RWKV_HEAD) for z in yc]
    for p in pairs:
        vec = vec_ref[:, cols[p]]
        yn = yc[p] * lax.rsqrt(var[p] + GN_EPS) * vec[0:1] + vec[1:2]
        bonus = segsum(r[p] * k[p] * vec[2:3]) * v[p]
        y_ref[0, :, cols[p]] = ((yn + bonus) * g_ref[0, :, cols[p]]).astype(y_ref.dtype)

    @pl.when(c == pl.num_programs(2) - 1)
    def _():
        so_ref[0] = s_sc[...]


def _wkv(r, lw, k, v, a, b, g, vec, s0, L, P, passes):
    B, T, _ = r.shape
    npair = D_MODEL // LANES
    width = P * LANES
    tile = lambda: pl.BlockSpec((1, L, width), lambda bb, pg, c: (bb, c, pg))
    state = lambda: pl.BlockSpec((1, P, LANES, LANES), lambda bb, pg, c: (bb, pg, 0, 0))
    return pl.pallas_call(
        functools.partial(_wkv_kernel, L=L, P=P, passes=passes),
        grid=(B, npair // P, T // L),
        in_specs=[tile() for _ in range(7)] + [pl.BlockSpec((8, width), lambda bb, pg, c: (0, pg)), state()],
        out_specs=[tile(), state()],
        out_shape=[jax.ShapeDtypeStruct((B, T, D_MODEL), BF16),
                   jax.ShapeDtypeStruct((B, npair, LANES, LANES), F32)],
        scratch_shapes=[pltpu.VMEM((P, LANES, LANES), F32)],
        compiler_params=_params(("arbitrary", "arbitrary", "arbitrary")),
        name="wkv_chunked",
    )(r, lw, k, v, a, b, g, vec, s0)


def _state_to_pairs(s):
    B = s.shape[0]
    s = s.reshape(B, -1, 2, RWKV_HEAD, RWKV_HEAD)
    z = jnp.zeros_like(s[:, :, 0])
    top = jnp.concatenate([s[:, :, 0], z], axis=-1)
    bot = jnp.concatenate([z, s[:, :, 1]], axis=-1)
    return jnp.concatenate([top, bot], axis=-2)


def _state_from_pairs(sp):
    B = sp.shape[0]
    h = RWKV_HEAD
    return jnp.stack([sp[:, :, :h, :h], sp[:, :, h:, h:]], axis=2).reshape(B, -1, h, h)


def _rope_tables(pos):
    half = ATTN_HEAD // 2
    inv = jnp.power(ROPE_THETA, -jnp.arange(half, dtype=F32) * 2.0 / ATTN_HEAD)
    ang = pos.astype(F32)[:, None] * inv[None, :]
    cos = jnp.cos(ang)
    sin = jnp.sin(ang)
    reps = LANES // ATTN_HEAD
    return jnp.tile(cos, (1, 2 * reps)), jnp.tile(jnp.concatenate([-sin, sin], axis=1), (1, reps))


def _pad_cols(w):
    return jnp.pad(w, ((0, 0), (0, LORA_PAD - w.shape[1])))


def _pad_rows(w):
    return jnp.pad(w, ((0, LORA_PAD - w.shape[0]), (0, 0)))


def _pad_tokens(z, t):
    return jnp.pad(z, ((0, 0), (0, t - z.shape[1]), (0, 0)))


def _trunk(x, pos, decode, shift0, wkv0, mem_k, mem_v, past, W):
    B, T, _ = x.shape
    if decode:
        xf = x.reshape(1, B, D_MODEL)
        tm = B
    else:
        xf = x
        tm = 512
    cos_t, sin_t = _rope_tables(pos if not decode else jnp.broadcast_to(pos, (B,)))
    shifts, states, ks, vs = [], [], [], []
    v_first = None
    for l in range(DEPTH):
        idx = l // 2
        if l % 2 == 0:
            vec = jnp.concatenate([
                W["mu"][idx], W["w0"][idx][None], W["a0"][idx][None],
                (W["v0"][idx - 1] if idx > 0 else jnp.zeros((D_MODEL,), F32))[None],
                W["k_k"][idx][None], W["k_a"][idx][None], W["mix_g"][l][None],
                jnp.zeros((_N_VEC - 12, D_MODEL), F32)], axis=0)
            loras = (W["w1"][idx], W["w2"][idx], W["a1"][idx], W["a2"][idx],
                     W["v1"][idx - 1] if idx > 0 else None, W["v2"][idx - 1] if idx > 0 else None,
                     W["g1"][idx], W["g2"][idx])
            if decode:
                outs = _rwkv_proj(xf, shift0[idx][None], None, v_first, vec, W["w_in"][l], loras, tm)
            else:
                outs = _rwkv_proj(xf, xf, shift0[idx][:, None], v_first, vec, W["w_in"][l], loras, tm // 2)
            r, lw, k, v, a, b, g, mq, xs = outs
            if idx == 0:
                v_first = v
            shifts.append(xs.reshape(B, D_MODEL))
            scan_in = [r, lw, k, v, a, b, g]
            if decode:
                scan_in = [_pad_tokens(z.reshape(B, 1, D_MODEL), WKV_CHUNK) for z in scan_in]
            vec2 = jnp.concatenate([W["lnx_g"][idx][None], W["lnx_b"][idx][None], W["r_k"][idx][None],
                                    jnp.zeros((5, D_MODEL), F32)], axis=0)
            y, s_new = _wkv(*scan_in, vec2, _state_to_pairs(wkv0[idx]), WKV_CHUNK, WKV_PAIRS, 1)
            states.append(_state_from_pairs(s_new))
            y_tok = y[:, :1].reshape(1, B, D_MODEL) if decode else y
        else:
            lam_init = 0.8 - 0.6 * math.exp(-0.3 * l)
            q, kf, kb, vf, vb, mq = _diff_proj(xf, W["mix_g"][l][None], W["w_in"][l], cos_t, sin_t, tm)
            lam_p = jnp.stack([W["lam_q1"][idx], W["lam_k1"][idx], W["lam_q2"][idx], W["lam_k2"][idx]])
            sg = W["subln_g"][idx][None]
            if decode:
                cache_k, cache_v, page_table = past
                hd = lambda z: z.reshape(B, ATTN_HEADS, LANES)
                o = _dec_attn(page_table, hd(q), cache_k, cache_v, hd(kf), hd(vf), lam_p, sg, lam_init, idx)
                y_tok = o.reshape(1, B, D_MODEL)
            else:
                y_tok = _flash(q, kb, vb, lam_p, sg, lam_init, 512, 512)
            ks.append(kf.reshape(B, T, ATTN_HEADS, LANES))
            vs.append(vf.reshape(B, T, ATTN_HEADS, LANES))
        if decode:
            mq8 = _pad_tokens(mq.reshape(B, 1, MEM_WIDTH), 8)
            y_mem = _mem_attend(mq8, mem_k[l], mem_v[l], 8)[:, :1].reshape(1, B, MEM_WIDTH)
        else:
            y_mem = _mem_attend(mq, mem_k[l], mem_v[l], tm)
        xf = _out_ffn(xf, y_tok, y_mem, W["w_out"][l], W["ffn_g"][l][None], W["w_gate"][l], W["w_up"][l],
                      W["w_down"][l], W["final_g"][None], tm, l == DEPTH - 1)
    return xf.reshape(B, T, D_MODEL), jnp.stack(shifts), jnp.stack(states), jnp.stack(ks), jnp.stack(vs)


def kernel(x_prompt, x_sample, cache_k, cache_v, cache_mem_k, cache_mem_v, state_rwkv_wkv, state_rwkv_shift, page_table, mem_prompt, w_in, w_out, mix_norm_g, ffn_norm_g, w_gate, w_up, w_down, final_norm_g, mem_norm_g, w_mem_k, w_mem_v, rwkv_mu, rwkv_w0, rwkv_w1, rwkv_w2, rwkv_a0, rwkv_a1, rwkv_a2, rwkv_v0, rwkv_v1, rwkv_v2, rwkv_g1, rwkv_g2, rwkv_k_k, rwkv_k_a, rwkv_r_k, rwkv_lnx_g, rwkv_lnx_b, diff_lam_q1, diff_lam_k1, diff_lam_q2, diff_lam_k2, diff_subln_g):
    bf = lambda z: z.astype(BF16)
    n_rwkv = rwkv_mu.shape[0]
    W = dict(
        w_in=bf(w_in), w_out=bf(w_out), mix_g=mix_norm_g, ffn_g=ffn_norm_g,
        w_gate=bf(w_gate), w_up=bf(w_up), w_down=bf(w_down), final_g=final_norm_g,
        mu=rwkv_mu, w0=rwkv_w0, a0=rwkv_a0, v0=rwkv_v0, k_k=rwkv_k_k, k_a=rwkv_k_a,
        w1=[bf(_pad_cols(rwkv_w1[i])) for i in range(n_rwkv)],
        w2=[bf(_pad_rows(rwkv_w2[i])) for i in range(n_rwkv)],
        a1=[bf(_pad_cols(rwkv_a1[i])) for i in range(n_rwkv)],
        a2=[bf(_pad_rows(rwkv_a2[i])) for i in range(n_rwkv)],
        v1=[bf(_pad_cols(rwkv_v1[i])) for i in range(n_rwkv - 1)],
        v2=[bf(_pad_rows(rwkv_v2[i])) for i in range(n_rwkv - 1)],
        g1=bf(rwkv_g1), g2=bf(rwkv_g2),
        r_k=rwkv_r_k.reshape(n_rwkv, D_MODEL), lnx_g=rwkv_lnx_g, lnx_b=rwkv_lnx_b,
        lam_q1=diff_lam_q1, lam_k1=diff_lam_k1, lam_q2=diff_lam_q2, lam_k2=diff_lam_k2,
        subln_g=diff_subln_g,
    )

    B, T, _ = x_prompt.shape
    M = mem_prompt.shape[1]
    mk, mv = _mem_kv(mem_prompt.reshape(B * M, D_MODEL), mem_norm_g[:, None], bf(w_mem_k), bf(w_mem_v))
    mk = mk.reshape(DEPTH, B, M, MEM_WIDTH)
    mv = mv.reshape(DEPTH, B, M, MEM_WIDTH)
    p_mem_k = mk.reshape(DEPTH, B, M, MEM_HEADS, MEM_HEAD)
    p_mem_v = mv.reshape(DEPTH, B, M, MEM_HEADS, MEM_HEAD)

    pos_p = jnp.arange(T, dtype=jnp.int32)
    shift0 = jnp.zeros((n_rwkv, B, D_MODEL), F32)
    wkv0 = jnp.zeros((n_rwkv, B, D_MODEL // RWKV_HEAD, RWKV_HEAD, RWKV_HEAD), F32)
    y_prompt, p_shift, p_wkv, p_k, p_v = _trunk(x_prompt, pos_p, False, shift0, wkv0, mk, mv, None, W)

    Bs = x_sample.shape[0]
    past_len = page_table.shape[1] * PAGE_SIZE
    pos_s = past_len + jnp.arange(x_sample.shape[1], dtype=jnp.int32)
    smk = cache_mem_k.reshape(DEPTH, Bs, -1, MEM_WIDTH)
    smv = cache_mem_v.reshape(DEPTH, Bs, -1, MEM_WIDTH)
    y_sample, s_shift, s_wkv, s_k, s_v = _trunk(
        x_sample, pos_s, True, state_rwkv_shift, state_rwkv_wkv, smk, smv, (cache_k, cache_v, page_table), W)

    return (y_prompt, y_sample, p_wkv, p_shift, p_k, p_v, p_mem_k, p_mem_v, s_wkv, s_shift, s_k, s_v)
```

```python
import functools
import math

import jax
import jax.numpy as jnp
from jax import lax
from jax.experimental import pallas as pl
from jax.experimental.pallas import tpu as pltpu

F32 = jnp.float32
BF16 = jnp.bfloat16

D_MODEL = 1024
DEPTH = 4
PAGE_SIZE = 128
RWKV_HEAD = 64
ATTN_HEAD = 64
ATTN_HEADS = D_MODEL // (2 * ATTN_HEAD)
MEM_HEADS = 4
MEM_HEAD = 128
MEM_WIDTH = MEM_HEADS * MEM_HEAD
D_FF = 2816
ROPE_THETA = 10000.0
NORM_EPS = 1e-6
SUBLN_EPS = 1e-5
GN_EPS = 1e-5 * RWKV_HEAD

Q_SCALE = ATTN_HEAD ** -0.5 * math.log2(math.e)
LANES = 128
LORA_PAD = 128
WKV_CHUNK = 64
WKV_PAIRS = 8
DEC_PAGES = 8
VMEM_LIMIT = 56 * 1024 * 1024

_NT = (((1,), (1,)), ((), ()))
_TN = (((0,), (0,)), ((), ()))
_NEG = -1e30


def _dot(a, b):
    return jnp.dot(a, b, preferred_element_type=F32)


def _dot_nt(a, b):
    return lax.dot_general(a, b, _NT, preferred_element_type=F32)


def _dot_tn(a, b):
    return lax.dot_general(a, b, _TN, preferred_element_type=F32)


def _split2(x):
    hi = x.astype(BF16)
    lo = (x - hi.astype(F32)).astype(BF16)
    return hi, lo


def _split3(x):
    h1 = x.astype(BF16)
    r1 = x - h1.astype(F32)
    h2 = r1.astype(BF16)
    h3 = (r1 - h2.astype(F32)).astype(BF16)
    return h1, h2, h3


def _dot_exact_rhs(x, m_bf16):
    h1, h2, h3 = _split3(x)
    return _dot(h1, m_bf16) + _dot(h2, m_bf16) + _dot(h3, m_bf16)


def _dot_exact_lhs(m_bf16, x):
    h1, h2, h3 = _split3(x)
    return _dot(m_bf16, h1) + _dot(m_bf16, h2) + _dot(m_bf16, h3)


def _mm(a, b, kind, passes):
    f = {"nn": _dot, "nt": _dot_nt, "tn": _dot_tn}[kind]
    if passes == 1:
        return f(a.astype(BF16), b.astype(BF16))
    ah, al = _split2(a)
    bh, bl = _split2(b)
    return f(ah, bh) + f(ah, bl) + f(al, bh)


def _rms(x, g):
    ms = jnp.mean(x * x, axis=-1, keepdims=True)
    return x * lax.rsqrt(ms + NORM_EPS) * g


def _sigmoid(x):
    return 1.0 / (1.0 + jnp.exp(-x))


def _softplus(x):
    return jnp.maximum(x, 0.0) + jnp.log(1.0 + jnp.exp(-jnp.abs(x)))


def _seg_ones(width):
    r = lax.broadcasted_iota(jnp.int32, (LANES, LANES), 0) // width
    c = lax.broadcasted_iota(jnp.int32, (LANES, LANES), 1) // width
    return jnp.where(r == c, 1.0, 0.0).astype(BF16)


def _resident(shape):
    nd = len(shape)
    return pl.BlockSpec(shape, lambda *_: (0,) * nd, pipeline_mode=pl.Buffered(1))


def _params(sem):
    return pltpu.CompilerParams(dimension_semantics=sem, vmem_limit_bytes=VMEM_LIMIT)


def _mem_kv_kernel(mem_ref, g_ref, wk_ref, wv_ref, k_ref, v_ref):
    mn = _rms(mem_ref[...], g_ref[0]).astype(BF16)
    k_ref[0] = _dot(mn, wk_ref[0])
    v_ref[0] = _dot(mn, wv_ref[0])


def _mem_kv(mem2d, g, wk, wv):
    rows = mem2d.shape[0]
    out = jax.ShapeDtypeStruct((DEPTH, rows, MEM_WIDTH), F32)
    return pl.pallas_call(
        _mem_kv_kernel,
        grid=(DEPTH,),
        in_specs=[
            pl.BlockSpec((rows, D_MODEL), lambda l: (0, 0)),
            pl.BlockSpec((1, 1, D_MODEL), lambda l: (l, 0, 0)),
            pl.BlockSpec((1, D_MODEL, MEM_WIDTH), lambda l: (l, 0, 0)),
            pl.BlockSpec((1, D_MODEL, MEM_WIDTH), lambda l: (l, 0, 0)),
        ],
        out_specs=[pl.BlockSpec((1, rows, MEM_WIDTH), lambda l: (l, 0, 0))] * 2,
        out_shape=[out, out],
        compiler_params=_params(("arbitrary",)),
        name="mem_kv",
    )(mem2d, g, wk, wv)


def _mem_attend_kernel(q_ref, k_ref, v_ref, o_ref):
    q = q_ref[0]
    k = k_ref[0].astype(BF16)
    v = v_ref[0].astype(BF16)
    outs = []
    for h in range(MEM_HEADS):
        sl = slice(h * MEM_HEAD, (h + 1) * MEM_HEAD)
        s = _dot_nt(q[:, sl], k[:, sl])
        m = jnp.max(s, axis=-1, keepdims=True)
        p = jnp.exp(s - m)
        l = jnp.sum(p, axis=-1, keepdims=True)
        outs.append(_dot(p.astype(BF16), v[:, sl]) / l)
    o_ref[0] = jnp.concatenate(outs, axis=-1).astype(o_ref.dtype)


def _mem_attend(q, mk, mv, tq):
    B, T, _ = q.shape
    M = mk.shape[1]
    return pl.pallas_call(
        _mem_attend_kernel,
        grid=(B, T // tq),
        in_specs=[
            pl.BlockSpec((1, tq, MEM_WIDTH), lambda b, i: (b, i, 0)),
            pl.BlockSpec((1, M, MEM_WIDTH), lambda b, i: (b, 0, 0)),
            pl.BlockSpec((1, M, MEM_WIDTH), lambda b, i: (b, 0, 0)),
        ],
        out_specs=pl.BlockSpec((1, tq, MEM_WIDTH), lambda b, i: (b, i, 0)),
        out_shape=jax.ShapeDtypeStruct((B, T, MEM_WIDTH), BF16),
        compiler_params=_params(("arbitrary", "arbitrary")),
        name="mem_attend",
    )(q, mk, mv)


def _out_ffn_kernel(x_ref, yt_ref, ym_ref, wo_ref, g_ref, wg_ref, wu_ref, wd_ref, fg_ref, o_ref,
                    *, final, ft):
    x1 = x_ref[0] + (_dot(yt_ref[0], wo_ref[:D_MODEL, :]) + _dot(ym_ref[0], wo_ref[D_MODEL:, :]))
    h = _rms(x1, g_ref[...]).astype(BF16)
    acc = jnp.zeros_like(x1)
    for f in range(0, D_FF, ft):
        gt = _dot(h, wg_ref[:, f:f + ft])
        up = _dot(h, wu_ref[:, f:f + ft])
        act = (gt * _sigmoid(gt) * up).astype(BF16)
        acc = acc + _dot(act, wd_ref[f:f + ft, :])
    acc = x1 + acc
    if final:
        acc = _rms(acc, fg_ref[...])
    o_ref[0] = acc


def _out_ffn(x, ytok, ymem, wo, g, wg, wu, wd, fg, tm, final):
    B, T, _ = x.shape
    row = lambda w: pl.BlockSpec((1, tm, w), lambda b, i: (b, i, 0))
    return pl.pallas_call(
        functools.partial(_out_ffn_kernel, final=final, ft=256),
        grid=(B, T // tm),
        in_specs=[
            row(D_MODEL), row(D_MODEL), row(MEM_WIDTH),
            _resident((D_MODEL + MEM_WIDTH, D_MODEL)),
            _resident((1, D_MODEL)),
            _resident((D_MODEL, D_FF)), _resident((D_MODEL, D_FF)), _resident((D_FF, D_MODEL)),
            _resident((1, D_MODEL)),
        ],
        out_specs=row(D_MODEL),
        out_shape=jax.ShapeDtypeStruct((B, T, D_MODEL), F32),
        compiler_params=_params(("arbitrary", "arbitrary")),
        name="out_ffn",
    )(x, ytok, ymem, wo, g, wg, wu, wd, fg)


def _diff_proj_kernel(x_ref, g_ref, w_ref, cos_ref, sin_ref,
                      q_ref, k_ref, kb_ref, v_ref, vb_ref, mq_ref):
    xn = _rms(x_ref[0], g_ref[...]).astype(BF16)
    tm = xn.shape[0]
    cos = cos_ref[...]
    sin = sin_ref[...]
    lane = lax.broadcasted_iota(jnp.int32, (tm, LANES), 1)
    first = (lane % ATTN_HEAD) < (ATTN_HEAD // 2)

    def rope(z):
        rot = jnp.where(first, pltpu.roll(z, LANES - ATTN_HEAD // 2, 1), pltpu.roll(z, ATTN_HEAD // 2, 1))
        return z * cos + rot * sin

    half = D_MODEL // 2
    for c in range(2):
        z = _dot(xn, w_ref[:, c * half:(c + 1) * half])
        for j in range(half // LANES):
            col = c * half + j * LANES
            q_ref[0, :, col:col + LANES] = (rope(z[:, j * LANES:(j + 1) * LANES]) * Q_SCALE).astype(BF16)
    for c in range(2):
        z = _dot(xn, w_ref[:, D_MODEL + c * half:D_MODEL + (c + 1) * half])
        for j in range(half // LANES):
            col = c * half + j * LANES
            kr = rope(z[:, j * LANES:(j + 1) * LANES])
            k_ref[0, :, col:col + LANES] = kr
            kb_ref[0, :, col:col + LANES] = kr.astype(BF16)
    for c in range(2):
        z = _dot(xn, w_ref[:, 2 * D_MODEL + c * half:2 * D_MODEL + (c + 1) * half])
        v_ref[0, :, c * half:(c + 1) * half] = z
        vb_ref[0, :, c * half:(c + 1) * half] = z.astype(BF16)
    z = _dot(xn, w_ref[:, 3 * D_MODEL:])
    mq_ref[0] = (z * (MEM_HEAD ** -0.5)).astype(BF16)


def _diff_proj(x, g, w, cos_t, sin_t, tm):
    B, T, _ = x.shape
    row = lambda w_: pl.BlockSpec((1, tm, w_), lambda b, i: (b, i, 0))
    sd = lambda w_, dt: jax.ShapeDtypeStruct((B, T, w_), dt)
    return pl.pallas_call(
        _diff_proj_kernel,
        grid=(B, T // tm),
        in_specs=[
            row(D_MODEL), _resident((1, D_MODEL)), _resident((D_MODEL, 3 * D_MODEL + MEM_WIDTH)),
            pl.BlockSpec((tm, LANES), lambda b, i: (i, 0)),
            pl.BlockSpec((tm, LANES), lambda b, i: (i, 0)),
        ],
        out_specs=[row(D_MODEL)] * 5 + [row(MEM_WIDTH)],
        out_shape=[sd(D_MODEL, BF16), sd(D_MODEL, F32), sd(D_MODEL, BF16), sd(D_MODEL, F32),
                   sd(D_MODEL, BF16), sd(MEM_WIDTH, BF16)],
        compiler_params=_params(("arbitrary", "arbitrary")),
        name="diff_proj",
    )(x, g, w, cos_t, sin_t)


def _lambda(lam_ref, lam_init):
    lp = lam_ref[...]
    s1 = jnp.sum(lp[0:1] * lp[1:2], axis=-1, keepdims=True)
    s2 = jnp.sum(lp[2:3] * lp[3:4], axis=-1, keepdims=True)
    return jnp.exp(s1) - jnp.exp(s2) + lam_init


def _subln(o, sg, lam_init):
    return o * lax.rsqrt(jnp.mean(o * o, axis=-1, keepdims=True) + SUBLN_EPS) * sg * (1.0 - lam_init)


def _flash_kernel(q_ref, k_ref, v_ref, lam_ref, sg_ref, o_ref, s_sc, p_sc, m_sc, l_sc, acc_sc,
                  *, tq, tk, lam_init):
    qi = pl.program_id(2)
    q = q_ref[0]
    lane = lax.broadcasted_iota(jnp.int32, (tq, LANES), 1)
    zero = jnp.zeros_like(q)
    qs = jnp.concatenate([jnp.where(lane < ATTN_HEAD, q, zero), jnp.where(lane < ATTN_HEAD, zero, q)], axis=0)
    rows = 2 * tq
    nfull = (qi * tq) // tk

    def kv(ref, i):
        return ref[0, pl.ds(pl.multiple_of(i * tk, tk), tk), :]

    def fold(parts, op):
        while len(parts) > 1:
            parts = [op(parts[i], parts[i + 1]) for i in range(0, len(parts), 2)]
        return parts[0]

    def softmax_update(s, pv):
        cols = [s[:, c:c + LANES] for c in range(0, tk, LANES)]
        m_old = m_sc[...]
        m_new = jnp.maximum(m_old, jnp.max(fold(cols, jnp.maximum), axis=-1, keepdims=True))
        alpha = jnp.exp2(m_old - m_new)
        p = [jnp.exp2(z - m_new) for z in cols]
        l_sc[...] = alpha * l_sc[...] + fold(p, jnp.add)
        acc_sc[...] = alpha * (acc_sc[...] + pv)
        m_sc[...] = m_new
        return jnp.concatenate(p, axis=-1).astype(BF16)

    m_sc[...] = jnp.full(m_sc.shape, _NEG, F32)
    l_sc[...] = jnp.zeros(l_sc.shape, F32)
    acc_sc[...] = jnp.zeros(acc_sc.shape, F32)
    p_sc[1] = jnp.zeros(p_sc.shape[1:], BF16)
    s_sc[0] = _dot_nt(qs, kv(k_ref, 0))

    def stage(cur, k):
        s_sc[1 - cur] = _dot_nt(qs, kv(k_ref, k + 1))
        pv = _dot(p_sc[1 - cur], kv(v_ref, jnp.maximum(k - 1, 0)))
        p_sc[cur] = softmax_update(s_sc[cur], pv)

    def body(j, carry):
        stage(0, 2 * j)
        stage(1, 2 * j + 1)
        return carry

    lax.fori_loop(0, nfull // 2, body, 0)

    def finish(cur):
        pv = _dot(p_sc[1 - cur], kv(v_ref, jnp.maximum(nfull - 1, 0)))
        r = lax.broadcasted_iota(jnp.int32, (rows, tk), 0)
        qpos = jnp.where(r >= tq, r - tq, r) + qi * tq
        kpos = lax.broadcasted_iota(jnp.int32, (rows, tk), 1) + nfull * tk
        p = softmax_update(jnp.where(kpos <= qpos, s_sc[cur], _NEG), pv)
        on = (acc_sc[...] + _dot(p, kv(v_ref, nfull))) / jnp.sum(l_sc[...], axis=-1, keepdims=True)
        lam = _lambda(lam_ref, lam_init)
        o = on[:tq] - lam * on[tq:]
        o_ref[0] = _subln(o, sg_ref[...], lam_init).astype(o_ref.dtype)

    @pl.when(nfull % 2 == 0)
    def _():
        finish(0)

    @pl.when(nfull % 2 == 1)
    def _():
        stage(0, nfull - 1)
        finish(1)


def _flash(q, k, v, lam_p, sg, lam_init, tq, tk):
    B, T, _ = q.shape
    rows = 2 * tq
    return pl.pallas_call(
        functools.partial(_flash_kernel, tq=tq, tk=tk, lam_init=lam_init),
        grid=(B, ATTN_HEADS, T // tq),
        in_specs=[
            pl.BlockSpec((1, tq, LANES), lambda b, h, i: (b, i, h)),
            pl.BlockSpec((1, T, LANES), lambda b, h, i: (b, 0, h)),
            pl.BlockSpec((1, T, LANES), lambda b, h, i: (b, 0, h)),
            _resident((4, ATTN_HEAD)), _resident((1, LANES)),
        ],
        out_specs=pl.BlockSpec((1, tq, LANES), lambda b, h, i: (b, i, h)),
        out_shape=jax.ShapeDtypeStruct((B, T, D_MODEL), BF16),
        scratch_shapes=[pltpu.VMEM((2, rows, tk), F32), pltpu.VMEM((2, rows, tk), BF16),
                        pltpu.VMEM((rows, LANES), F32), pltpu.VMEM((rows, LANES), F32),
                        pltpu.VMEM((rows, LANES), F32)],
        compiler_params=_params(("arbitrary", "arbitrary", "arbitrary")),
        name="diff_flash",
    )(q, k, v, lam_p, sg)


def _dec_attn_kernel(*refs, lam_init, n_pg):
    pt_ref, q_ref = refs[0], refs[1]
    k_refs = refs[2:2 + n_pg]
    v_refs = refs[2 + n_pg:2 + 2 * n_pg]
    kn_ref, vn_ref, lam_ref, sg_ref, o_ref, m_sc, l_sc, acc_sc = refs[2 + 2 * n_pg:]
    del pt_ref
    j = pl.program_id(1)
    H = ATTN_HEADS
    G = 2 * H

    @pl.when(j == 0)
    def _():
        m_sc[...] = jnp.full(m_sc.shape, _NEG, F32)
        l_sc[...] = jnp.zeros(l_sc.shape, F32)
        acc_sc[...] = jnp.zeros(acc_sc.shape, F32)

    q8 = q_ref[0]
    lane = lax.broadcasted_iota(jnp.int32, (H, LANES), 1)
    zero = jnp.zeros_like(q8)
    q16 = jnp.concatenate([jnp.where(lane < ATTN_HEAD, q8, zero), jnp.where(lane < ATTN_HEAD, zero, q8)], axis=0)

    n = PAGE_SIZE * H
    same_head = (lax.broadcasted_iota(jnp.int32, (G, n), 0) % H) == (lax.broadcasted_iota(jnp.int32, (G, n), 1) % H)
    s = [jnp.where(same_head, _dot_nt(q16, kr[...].reshape(n, LANES).astype(BF16)), _NEG) for kr in k_refs]
    m_old = m_sc[...]
    m_new = m_old
    for z in s:
        m_new = jnp.maximum(m_new, jnp.max(z, axis=-1, keepdims=True))
    alpha = jnp.exp2(m_old - m_new)
    p = [jnp.exp2(z - m_new) for z in s]
    l_sc[...] = alpha * l_sc[...] + sum(jnp.sum(z, axis=-1, keepdims=True) for z in p)
    pv = sum(_dot(z.astype(BF16), vr[...].reshape(n, LANES).astype(BF16)) for z, vr in zip(p, v_refs))
    acc_sc[...] = alpha * acc_sc[...] + pv
    m_sc[...] = m_new

    @pl.when(j == pl.num_programs(1) - 1)
    def _():
        kn = kn_ref[0]
        vn = vn_ref[0]
        kn16 = jnp.concatenate([kn, kn], axis=0)
        vn16 = jnp.concatenate([vn, vn], axis=0)
        sn = jnp.sum(q16.astype(F32) * kn16, axis=-1, keepdims=True)
        m0 = m_sc[...]
        m1 = jnp.maximum(m0, sn)
        a0 = jnp.exp2(m0 - m1)
        pn = jnp.exp2(sn - m1)
        l1 = a0 * l_sc[...] + pn
        acc1 = a0 * acc_sc[...] + pn * vn16
        on = acc1 / l1
        lam = _lambda(lam_ref, lam_init)
        o = on[:H] - lam * on[H:]
        o_ref[0] = _subln(o, sg_ref[...], lam_init).astype(o_ref.dtype)


def _dec_attn(page_table, q, cache_k, cache_v, kn, vn, lam_p, sg, lam_init, layer):
    B, n_pages = page_table.shape
    H = ATTN_HEADS
    n_pg = math.gcd(DEC_PAGES, n_pages)
    head = lambda: pl.BlockSpec((1, H, LANES), lambda b, j, pt: (b, 0, 0))
    page = lambda i: pl.BlockSpec((None, None, PAGE_SIZE, H, LANES),
                                  lambda b, j, pt: (layer, pt[b, j * n_pg + i], 0, 0, 0))
    pages = [page(i) for i in range(n_pg)]
    grid_spec = pltpu.PrefetchScalarGridSpec(
        num_scalar_prefetch=1,
        grid=(B, n_pages // n_pg),
        in_specs=[head()] + pages + pages + [head(), head(), _resident((4, ATTN_HEAD)), _resident((1, LANES))],
        out_specs=pl.BlockSpec((1, H, LANES), lambda b, j, pt: (b, 0, 0)),
        scratch_shapes=[pltpu.VMEM((2 * H, 1), F32), pltpu.VMEM((2 * H, 1), F32), pltpu.VMEM((2 * H, LANES), F32)],
    )
    return pl.pallas_call(
        functools.partial(_dec_attn_kernel, lam_init=lam_init, n_pg=n_pg),
        grid_spec=grid_spec,
        out_shape=jax.ShapeDtypeStruct((B, H, LANES), BF16),
        compiler_params=_params(("arbitrary", "arbitrary")),
        name="paged_diff_attn",
    )(page_table, q, *([cache_k] * n_pg), *([cache_v] * n_pg), kn, vn, lam_p, sg)


_V_MU, _V_W0, _V_A0, _V_V0, _V_KK, _V_KA, _V_G = 0, 6, 7, 8, 9, 10, 11
_N_VEC = 16


def _rwkv_proj_kernel(*refs, shifted, has_vfirst):
    it = iter(refs)
    x_ref = next(it)
    prev_ref = next(it)
    shift_ref = next(it) if shifted else None
    vf_ref = next(it) if has_vfirst else None
    vec_ref, w_ref, w1_ref, w2_ref, a1_ref, a2_ref = (next(it) for _ in range(6))
    v1_ref, v2_ref = (next(it), next(it)) if has_vfirst else (None, None)
    g1_ref, g2_ref = next(it), next(it)
    r_ref, lw_ref, k_ref, v_ref, a_ref, b_ref, g_ref, mq_ref, xs_ref = (next(it) for _ in range(9))

    vec = vec_ref[...]
    row = lambda i: vec[i:i + 1]
    gain = row(_V_G)
    xn = _rms(x_ref[0], gain)
    tm = xn.shape[0]
    if shifted:
        pr = _rms(prev_ref[0][7:8], gain)
        pr = jnp.where(pl.program_id(1) == 0, shift_ref[0], pr)
        ridx = lax.broadcasted_iota(jnp.int32, (tm, 1), 0)
        xprev = jnp.where(ridx == 0, pr, pltpu.roll(xn, 1, 0))
        xs_ref[0] = xn[tm - 1:tm]
    else:
        xprev = prev_ref[0]
        xs_ref[0] = xn
    xx = xprev - xn
    mix = lambda j: (xn + xx * row(_V_MU + j)).astype(BF16)
    xr, xw, xk, xv, xa, xg = (mix(j) for j in range(6))

    D = D_MODEL
    r = _dot(xr, w_ref[:, :D])
    k = _dot(xk, w_ref[:, D:2 * D])
    v = _dot(xv, w_ref[:, 2 * D:3 * D])
    mq_ref[0] = (_dot(xn.astype(BF16), w_ref[:, 3 * D:]) * (MEM_HEAD ** -0.5)).astype(BF16)

    w_in = row(_V_W0) + _dot(jnp.tanh(_dot(xw, w1_ref[...])).astype(BF16), w2_ref[...])
    w_log = -_softplus(-w_in) - 0.5
    lw_ref[0] = -jnp.exp(w_log)
    if has_vfirst:
        gate = _sigmoid(row(_V_V0) + _dot(_dot(xv, v1_ref[...]).astype(BF16), v2_ref[...]))
        v = v + (vf_ref[0] - v) * gate
    a = _sigmoid(row(_V_A0) + _dot(_dot(xa, a1_ref[...]).astype(BF16), a2_ref[...]))
    g_ref[0] = _dot(_sigmoid(_dot(xg, g1_ref[...])).astype(BF16), g2_ref[...])

    seg = _seg_ones(RWKV_HEAD)
    kk = k * row(_V_KK)
    sq = kk * kk
    n2 = jnp.concatenate(
        [sum(_dot(p_, seg) for p_ in _split2(sq[:, c:c + LANES])) for c in range(0, D, LANES)], axis=-1)
    kk = kk / jnp.maximum(jnp.sqrt(n2), 1e-12)
    r_ref[0] = r
    k_ref[0] = k * (1.0 + (a - 1.0) * row(_V_KA))
    v_ref[0] = v
    a_ref[0] = -kk
    b_ref[0] = kk * a


def _rwkv_proj(x, prev, shift, vfirst, vec, w, loras, tm):
    B, T, _ = x.shape
    shifted = shift is not None
    has_vfirst = vfirst is not None
    row = lambda w_: pl.BlockSpec((1, tm, w_), lambda b, i: (b, i, 0))
    ins, specs = [x], [row(D_MODEL)]
    if shifted:
        ins += [prev, shift]
        specs += [pl.BlockSpec((1, 8, D_MODEL), lambda b, i: (b, jnp.maximum(i * (tm // 8) - 1, 0), 0)),
                  pl.BlockSpec((1, 1, D_MODEL), lambda b, i: (b, 0, 0))]
    else:
        ins += [prev]
        specs += [row(D_MODEL)]
    if has_vfirst:
        ins.append(vfirst)
        specs.append(row(D_MODEL))
    w1, w2, a1, a2, v1, v2, g1, g2 = loras
    small = [vec, w, w1, w2, a1, a2] + ([v1, v2] if has_vfirst else []) + [g1, g2]
    ins += small
    specs += [_resident(s.shape) for s in small]
    sd = lambda dt, w_=D_MODEL, t_=T: jax.ShapeDtypeStruct((B, t_, w_), dt)
    xs_rows = 1 if shifted else T
    xs_spec = (pl.BlockSpec((1, 1, D_MODEL), lambda b, i: (b, 0, 0)) if shifted else row(D_MODEL))
    return pl.pallas_call(
        functools.partial(_rwkv_proj_kernel, shifted=shifted, has_vfirst=has_vfirst),
        grid=(B, T // tm),
        in_specs=specs,
        out_specs=[row(D_MODEL)] * 7 + [row(MEM_WIDTH), xs_spec],
        out_shape=[sd(F32)] * 7 + [sd(BF16, MEM_WIDTH), sd(F32, D_MODEL, xs_rows)],
        compiler_params=_params(("arbitrary", "arbitrary")),
        name="rwkv_proj",
    )(*ins)


def _wkv_kernel(r_ref, lw_ref, k_ref, v_ref, a_ref, b_ref, g_ref, vec_ref, s0_ref, y_ref, so_ref, s_sc,
                *, L, NB, P, passes):
    c = pl.program_id(2)
    sel = [(bb, p, slice(p * LANES, (p + 1) * LANES)) for bb in range(NB) for p in range(P)]

    @pl.when(c == 0)
    def _():
        for n, (bb, p, _) in enumerate(sel):
            s_sc[n] = s0_ref[bb, p]

    L2 = 2 * L
    tri = jnp.where(lax.broadcasted_iota(jnp.int32, (L, L), 0) >= lax.broadcasted_iota(jnp.int32, (L, L), 1),
                    1.0, 0.0).astype(BF16)
    r2 = lax.broadcasted_iota(jnp.int32, (L2, L2), 0)
    c2 = lax.broadcasted_iota(jnp.int32, (L2, L2), 1)
    strict = (r2 % L) > (c2 % L)
    incl = (r2 % L) >= (c2 % L)
    eye = jnp.where(r2 == c2, 1.0, 0.0)
    m0 = lax.broadcasted_iota(jnp.int32, (L, LANES), 1) < RWKV_HEAD
    seg = _seg_ones(RWKV_HEAD)
    mm = functools.partial(_mm, passes=passes)

    def stack(z):
        return jnp.concatenate([jnp.where(m0, z, 0.0), jnp.where(m0, 0.0, z)], axis=0)

    def segsum(z):
        hi, lo = _split2(z)
        return _dot(hi, seg) + _dot(lo, seg)

    pairs = range(len(sel))
    r = [r_ref[bb, :, sl] for bb, _, sl in sel]
    lw = [lw_ref[bb, :, sl] for bb, _, sl in sel]
    k = [k_ref[bb, :, sl] for bb, _, sl in sel]
    v = [v_ref[bb, :, sl] for bb, _, sl in sel]
    a = [a_ref[bb, :, sl] for bb, _, sl in sel]
    b = [b_ref[bb, :, sl] for bb, _, sl in sel]
    cum = [_dot_exact_lhs(tri, z) for z in lw]
    c_end = [z[L - 1:L, :] for z in cum]
    e_neg = [jnp.exp(-z) for z in cum]
    at_s = [stack(a[p] * jnp.exp(cum[p] - lw[p])) for p in pairs]
    rt_s = [stack(r[p] * jnp.exp(cum[p])) for p in pairs]
    bt_s = [stack(b[p] * e_neg[p]) for p in pairs]
    kt_s = [stack(k[p] * e_neg[p]) for p in pairs]
    v_s = [stack(z) for z in v]
    S = [s_sc[p] for p in pairs]

    n_ab = [jnp.where(strict, mm(at_s[p], bt_s[p], "nt"), 0.0) for p in pairs]
    a_ak = [jnp.where(strict, mm(at_s[p], kt_s[p], "nt"), 0.0) for p in pairs]
    a_rb = [jnp.where(incl, mm(rt_s[p], bt_s[p], "nt"), 0.0) for p in pairs]
    a_rk = [jnp.where(incl, mm(rt_s[p], kt_s[p], "nt"), 0.0) for p in pairs]

    t_inv = [eye + z for z in n_ab]
    pw = n_ab
    for _ in range(int(math.log2(L)) - 1):
        pw = [mm(z, z, "nn") for z in pw]
        t_inv = [t_inv[p] + mm(t_inv[p], pw[p], "nn") for p in pairs]

    x_s = [mm(at_s[p], S[p], "nt") + mm(a_ak[p], v_s[p], "nn") for p in pairs]
    u_s = [mm(t_inv[p], x_s[p], "nn") for p in pairs]
    y_s = [mm(rt_s[p], S[p], "nt") + mm(a_rb[p], u_s[p], "nn") + mm(a_rk[p], v_s[p], "nn") for p in pairs]
    for p in pairs:
        e_end = jnp.exp(c_end[p] - cum[p])
        s_sc[p] = (S[p] * jnp.exp(c_end[p]) + mm(u_s[p], stack(b[p] * e_end), "tn")
                   + mm(v_s[p], stack(k[p] * e_end), "tn"))

    y = [z[:L] + z[L:] for z in y_s]
    yc = [z - segsum(z) * (1.0 / RWKV_HEAD) for z in y]
    var = [segsum(z * z) * (1.0 / RWKV_HEAD) for z in yc]
    for p, (bb, _, sl) in enumerate(sel):
        vec = vec_ref[:, sl]
        yn = yc[p] * lax.rsqrt(var[p] + GN_EPS) * vec[0:1] + vec[1:2]
        bonus = segsum(r[p] * k[p] * vec[2:3]) * v[p]
        y_ref[bb, :, sl] = ((yn + bonus) * g_ref[bb, :, sl]).astype(y_ref.dtype)

    @pl.when(c == pl.num_programs(2) - 1)
    def _():
        for n, (bb, p, _) in enumerate(sel):
            so_ref[bb, p] = s_sc[n]


def _wkv(r, lw, k, v, a, b, g, vec, s0, L, NB, P, passes):
    B, T, _ = r.shape
    npair = D_MODEL // LANES
    width = P * LANES
    tile = lambda: pl.BlockSpec((NB, L, width), lambda bb, pg, c: (bb, c, pg))
    state = lambda: pl.BlockSpec((NB, P, LANES, LANES), lambda bb, pg, c: (bb, pg, 0, 0))
    return pl.pallas_call(
        functools.partial(_wkv_kernel, L=L, NB=NB, P=P, passes=passes),
        grid=(B // NB, npair // P, T // L),
        in_specs=[tile() for _ in range(7)] + [pl.BlockSpec((8, width), lambda bb, pg, c: (0, pg)), state()],
        out_specs=[tile(), state()],
        out_shape=[jax.ShapeDtypeStruct((B, T, D_MODEL), BF16),
                   jax.ShapeDtypeStruct((B, npair, LANES, LANES), F32)],
        scratch_shapes=[pltpu.VMEM((NB * P, LANES, LANES), F32)],
        compiler_params=_params(("arbitrary", "arbitrary", "arbitrary")),
        name="wkv_chunked",
    )(r, lw, k, v, a, b, g, vec, s0)


def _wkv_step_kernel(r_ref, lw_ref, k_ref, v_ref, a_ref, b_ref, g_ref, vec_ref, s_ref, y_ref, so_ref, *, nb):
    hd = RWKV_HEAD
    npair = D_MODEL // LANES
    rid = lax.broadcasted_iota(jnp.int32, (8, LANES), 0)
    lid = lax.broadcasted_iota(jnp.int32, (8, LANES), 1)
    own = ((rid == 0) & (lid < hd)) | ((rid == 1) & (lid >= hd))
    seg = _seg_ones(hd)
    zpad = jnp.zeros((hd, hd), F32)
    cols = [slice(p * LANES, (p + 1) * LANES) for p in range(npair)]
    combos = [(i, p) for i in range(nb) for p in range(npair)]

    def vec_row(ref, i, p):
        return ref[i:i + 1, cols[p]]

    def at_row(z, i):
        return jnp.where(rid == i, z, 0.0).astype(BF16)

    def split_heads(z):
        return jnp.where(own, z, 0.0)

    S = [jnp.concatenate([jnp.concatenate([s_ref[i, 2 * p], zpad], axis=1),
                          jnp.concatenate([zpad, s_ref[i, 2 * p + 1]], axis=1)], axis=0) for i, p in combos]
    Sb = [z.astype(BF16) for z in S]
    u = [_dot_nt(at_row(vec_row(a_ref, i, p), 0), Sb[n])[0:1] for n, (i, p) in enumerate(combos)]
    lhs = [jnp.concatenate([split_heads(u[n]), split_heads(vec_row(v_ref, i, p))], axis=0).astype(BF16)
           for n, (i, p) in enumerate(combos)]
    rhs = [jnp.concatenate([split_heads(vec_row(b_ref, i, p)), split_heads(vec_row(k_ref, i, p))], axis=0).astype(BF16)
           for i, p in combos]
    S = [S[n] * jnp.exp(vec_row(lw_ref, i, p)) + _dot_tn(lhs[n], rhs[n]) for n, (i, p) in enumerate(combos)]
    for n, (i, p) in enumerate(combos):
        so_ref[i, 2 * p] = S[n][:hd, :hd]
        so_ref[i, 2 * p + 1] = S[n][hd:, hd:]
    yrow = [_dot_nt(at_row(vec_row(r_ref, i, p), i), S[n].astype(BF16)) for n, (i, p) in enumerate(combos)]

    def segsum(z):
        hi, lo = _split2(z)
        return _dot(hi, seg) + _dot(lo, seg)

    for p in range(npair):
        y = yrow[p]
        for i in range(1, nb):
            y = y + yrow[i * npair + p]
        yc = y - segsum(y) * (1.0 / hd)
        var = segsum(yc * yc) * (1.0 / hd)
        vec = vec_ref[:, cols[p]]
        yn = yc * lax.rsqrt(var + GN_EPS) * vec[0:1] + vec[1:2]
        bonus = segsum(r_ref[:, cols[p]] * k_ref[:, cols[p]] * vec[2:3]) * v_ref[:, cols[p]]
        y_ref[:, cols[p]] = (yn + bonus) * g_ref[:, cols[p]]


def _wkv_step(r, lw, k, v, a, b, g, vec, s0, nb=8):
    n_seq = r.shape[0]
    heads = D_MODEL // RWKV_HEAD
    tile = lambda: pl.BlockSpec((nb, D_MODEL), lambda i: (i, 0))
    state = lambda: pl.BlockSpec((nb, heads, RWKV_HEAD, RWKV_HEAD), lambda i: (i, 0, 0, 0))
    return pl.pallas_call(
        functools.partial(_wkv_step_kernel, nb=nb),
        grid=(n_seq // nb,),
        in_specs=[tile() for _ in range(7)] + [_resident((8, D_MODEL)), state()],
        out_specs=[tile(), state()],
        out_shape=[jax.ShapeDtypeStruct((n_seq, D_MODEL), F32),
                   jax.ShapeDtypeStruct((n_seq, heads, RWKV_HEAD, RWKV_HEAD), F32)],
        compiler_params=_params(("arbitrary",)),
        name="wkv_step",
    )(r, lw, k, v, a, b, g, vec, s0)


def _state_to_pairs(s):
    B = s.shape[0]
    s = s.reshape(B, -1, 2, RWKV_HEAD, RWKV_HEAD)
    z = jnp.zeros_like(s[:, :, 0])
    top = jnp.concatenate([s[:, :, 0], z], axis=-1)
    bot = jnp.concatenate([z, s[:, :, 1]], axis=-1)
    return jnp.concatenate([top, bot], axis=-2)


def _state_from_pairs(sp):
    B = sp.shape[0]
    h = RWKV_HEAD
    return jnp.stack([sp[:, :, :h, :h], sp[:, :, h:, h:]], axis=2).reshape(B, -1, h, h)


def _rope_tables(pos):
    half = ATTN_HEAD // 2
    inv = jnp.power(ROPE_THETA, -jnp.arange(half, dtype=F32) * 2.0 / ATTN_HEAD)
    ang = pos.astype(F32)[:, None] * inv[None, :]
    cos = jnp.cos(ang)
    sin = jnp.sin(ang)
    reps = LANES // ATTN_HEAD
    return jnp.tile(cos, (1, 2 * reps)), jnp.tile(jnp.concatenate([-sin, sin], axis=1), (1, reps))


def _pad_cols(w):
    return jnp.pad(w, ((0, 0), (0, LORA_PAD - w.shape[1])))


def _pad_rows(w):
    return jnp.pad(w, ((0, LORA_PAD - w.shape[0]), (0, 0)))


def _pad_tokens(z, t):
    return jnp.pad(z, ((0, 0), (0, t - z.shape[1]), (0, 0)))


def _trunk(x, pos, decode, shift0, wkv0, mem_k, mem_v, past, W):
    B, T, _ = x.shape
    if decode:
        xf = x.reshape(1, B, D_MODEL)
        tm = B
    else:
        xf = x
        tm = 512
    cos_t, sin_t = _rope_tables(pos if not decode else jnp.broadcast_to(pos, (B,)))
    shifts, states, ks, vs = [], [], [], []
    v_first = None
    for l in range(DEPTH):
        idx = l // 2
        if l % 2 == 0:
            vec = jnp.concatenate([
                W["mu"][idx], W["w0"][idx][None], W["a0"][idx][None],
                (W["v0"][idx - 1] if idx > 0 else jnp.zeros((D_MODEL,), F32))[None],
                W["k_k"][idx][None], W["k_a"][idx][None], W["mix_g"][l][None],
                jnp.zeros((_N_VEC - 12, D_MODEL), F32)], axis=0)
            loras = (W["w1"][idx], W["w2"][idx], W["a1"][idx], W["a2"][idx],
                     W["v1"][idx - 1] if idx > 0 else None, W["v2"][idx - 1] if idx > 0 else None,
                     W["g1"][idx], W["g2"][idx])
            if decode:
                outs = _rwkv_proj(xf, shift0[idx][None], None, v_first, vec, W["w_in"][l], loras, tm)
            else:
                outs = _rwkv_proj(xf, xf, shift0[idx][:, None], v_first, vec, W["w_in"][l], loras, tm // 2)
            r, lw, k, v, a, b, g, mq, xs = outs
            if idx == 0:
                v_first = v
            shifts.append(xs.reshape(B, D_MODEL))
            scan_in = [r, lw, k, v, a, b, g]
            vec2 = jnp.concatenate([W["lnx_g"][idx][None], W["lnx_b"][idx][None], W["r_k"][idx][None],
                                    jnp.zeros((5, D_MODEL), F32)], axis=0)
            if decode:
                y, s_new = _wkv_step(*[z.reshape(B, D_MODEL) for z in scan_in], vec2, wkv0[idx])
                states.append(s_new)
                y_tok = y.astype(BF16).reshape(1, B, D_MODEL)
            else:
                y_tok, s_new = _wkv(*scan_in, vec2, _state_to_pairs(wkv0[idx]), WKV_CHUNK, B, WKV_PAIRS, 1)
                states.append(_state_from_pairs(s_new))
        else:
            lam_init = 0.8 - 0.6 * math.exp(-0.3 * l)
            q, kf, kb, vf, vb, mq = _diff_proj(xf, W["mix_g"][l][None], W["w_in"][l], cos_t, sin_t, tm)
            lam_p = jnp.stack([W["lam_q1"][idx], W["lam_k1"][idx], W["lam_q2"][idx], W["lam_k2"][idx]])
            sg = W["subln_g"][idx][None]
            if decode:
                cache_k, cache_v, page_table = past
                hd = lambda z: z.reshape(B, ATTN_HEADS, LANES)
                o = _dec_attn(page_table, hd(q), cache_k, cache_v, hd(kf), hd(vf), lam_p, sg, lam_init, idx)
                y_tok = o.reshape(1, B, D_MODEL)
            else:
                y_tok = _flash(q, kb, vb, lam_p, sg, lam_init, 512, 512)
            ks.append(kf.reshape(B, T, ATTN_HEADS, LANES))
            vs.append(vf.reshape(B, T, ATTN_HEADS, LANES))
        if decode:
            mq8 = _pad_tokens(mq.reshape(B, 1, MEM_WIDTH), 8)
            y_mem = _mem_attend(mq8, mem_k[l], mem_v[l], 8)[:, :1].reshape(1, B, MEM_WIDTH)
        else:
            y_mem = _mem_attend(mq, mem_k[l], mem_v[l], 2 * tm)
        xf = _out_ffn(xf, y_tok, y_mem, W["w_out"][l], W["ffn_g"][l][None], W["w_gate"][l], W["w_up"][l],
                      W["w_down"][l], W["final_g"][None], tm, l == DEPTH - 1)
    return xf.reshape(B, T, D_MODEL), jnp.stack(shifts), jnp.stack(states), jnp.stack(ks), jnp.stack(vs)


def kernel(x_prompt, x_sample, cache_k, cache_v, cache_mem_k, cache_mem_v, state_rwkv_wkv, state_rwkv_shift, page_table, mem_prompt, w_in, w_out, mix_norm_g, ffn_norm_g, w_gate, w_up, w_down, final_norm_g, mem_norm_g, w_mem_k, w_mem_v, rwkv_mu, rwkv_w0, rwkv_w1, rwkv_w2, rwkv_a0, rwkv_a1, rwkv_a2, rwkv_v0, rwkv_v1, rwkv_v2, rwkv_g1, rwkv_g2, rwkv_k_k, rwkv_k_a, rwkv_r_k, rwkv_lnx_g, rwkv_lnx_b, diff_lam_q1, diff_lam_k1, diff_lam_q2, diff_lam_k2, diff_subln_g):
    bf = lambda z: z.astype(BF16)
    n_rwkv = rwkv_mu.shape[0]
    W = dict(
        w_in=bf(w_in), w_out=bf(w_out), mix_g=mix_norm_g, ffn_g=ffn_norm_g,
        w_gate=bf(w_gate), w_up=bf(w_up), w_down=bf(w_down), final_g=final_norm_g,
        mu=rwkv_mu, w0=rwkv_w0, a0=rwkv_a0, v0=rwkv_v0, k_k=rwkv_k_k, k_a=rwkv_k_a,
        w1=[bf(_pad_cols(rwkv_w1[i])) for i in range(n_rwkv)],
        w2=[bf(_pad_rows(rwkv_w2[i])) for i in range(n_rwkv)],
        a1=[bf(_pad_cols(rwkv_a1[i])) for i in range(n_rwkv)],
        a2=[bf(_pad_rows(rwkv_a2[i])) for i in range(n_rwkv)],
        v1=[bf(_pad_cols(rwkv_v1[i])) for i in range(n_rwkv - 1)],
        v2=[bf(_pad_rows(rwkv_v2[i])) for i in range(n_rwkv - 1)],
        g1=bf(rwkv_g1), g2=bf(rwkv_g2),
        r_k=rwkv_r_k.reshape(n_rwkv, D_MODEL), lnx_g=rwkv_lnx_g, lnx_b=rwkv_lnx_b,
        lam_q1=diff_lam_q1, lam_k1=diff_lam_k1, lam_q2=diff_lam_q2, lam_k2=diff_lam_k2,
        subln_g=diff_subln_g,
    )

    B, T, _ = x_prompt.shape
    M = mem_prompt.shape[1]
    mk, mv = _mem_kv(mem_prompt.reshape(B * M, D_MODEL), mem_norm_g[:, None], bf(w_mem_k), bf(w_mem_v))
    mk = mk.reshape(DEPTH, B, M, MEM_WIDTH)
    mv = mv.reshape(DEPTH, B, M, MEM_WIDTH)
    p_mem_k = mk.reshape(DEPTH, B, M, MEM_HEADS, MEM_HEAD)
    p_mem_v = mv.reshape(DEPTH, B, M, MEM_HEADS, MEM_HEAD)

    pos_p = jnp.arange(T, dtype=jnp.int32)
    shift0 = jnp.zeros((n_rwkv, B, D_MODEL), F32)
    wkv0 = jnp.zeros((n_rwkv, B, D_MODEL // RWKV_HEAD, RWKV_HEAD, RWKV_HEAD), F32)
    y_prompt, p_shift, p_wkv, p_k, p_v = _trunk(x_prompt, pos_p, False, shift0, wkv0, mk, mv, None, W)

    Bs = x_sample.shape[0]
    past_len = page_table.shape[1] * PAGE_SIZE
    pos_s = past_len + jnp.arange(x_sample.shape[1], dtype=jnp.int32)
    smk = cache_mem_k.reshape(DEPTH, Bs, -1, MEM_WIDTH)
    smv = cache_mem_v.reshape(DEPTH, Bs, -1, MEM_WIDTH)
    y_sample, s_shift, s_wkv, s_k, s_v = _trunk(
        x_sample, pos_s, True, state_rwkv_shift, state_rwkv_wkv, smk, smv, (cache_k, cache_v, page_table), W)

    return (y_prompt, y_sample, p_wkv, p_shift, p_k, p_v, p_mem_k, p_mem_v, s_wkv, s_shift, s_k, s_v)
```

```python
import functools
import math

import jax
import jax.numpy as jnp
from jax import lax
from jax.experimental import pallas as pl
from jax.experimental.pallas import tpu as pltpu

F32 = jnp.float32
BF16 = jnp.bfloat16

D_MODEL = 1024
DEPTH = 4
PAGE_SIZE = 128
RWKV_HEAD = 64
ATTN_HEAD = 64
ATTN_HEADS = D_MODEL // (2 * ATTN_HEAD)
MEM_HEADS = 4
MEM_HEAD = 128
MEM_WIDTH = MEM_HEADS * MEM_HEAD
D_FF = 2816
ROPE_THETA = 10000.0
NORM_EPS = 1e-6
SUBLN_EPS = 1e-5
GN_EPS = 1e-5 * RWKV_HEAD

Q_SCALE = ATTN_HEAD ** -0.5 * math.log2(math.e)
LANES = 128
LORA_PAD = 128
WKV_CHUNK = 64
WKV_PAIRS = 8
DEC_PAGES = 8
VMEM_LIMIT = 56 * 1024 * 1024

_NT = (((1,), (1,)), ((), ()))
_TN = (((0,), (0,)), ((), ()))
_NEG = -1e30


def _dot(a, b):
    return jnp.dot(a, b, preferred_element_type=F32)


def _dot_nt(a, b):
    return lax.dot_general(a, b, _NT, preferred_element_type=F32)


def _dot_tn(a, b):
    return lax.dot_general(a, b, _TN, preferred_element_type=F32)


def _split2(x):
    hi = x.astype(BF16)
    lo = (x - hi.astype(F32)).astype(BF16)
    return hi, lo


def _split3(x):
    h1 = x.astype(BF16)
    r1 = x - h1.astype(F32)
    h2 = r1.astype(BF16)
    h3 = (r1 - h2.astype(F32)).astype(BF16)
    return h1, h2, h3


def _dot_exact_rhs(x, m_bf16):
    h1, h2, h3 = _split3(x)
    return _dot(h1, m_bf16) + _dot(h2, m_bf16) + _dot(h3, m_bf16)


def _dot_exact_lhs(m_bf16, x):
    h1, h2, h3 = _split3(x)
    return _dot(m_bf16, h1) + _dot(m_bf16, h2) + _dot(m_bf16, h3)


def _mm(a, b, kind, passes):
    f = {"nn": _dot, "nt": _dot_nt, "tn": _dot_tn}[kind]
    if passes == 1:
        return f(a.astype(BF16), b.astype(BF16))
    ah, al = _split2(a)
    bh, bl = _split2(b)
    return f(ah, bh) + f(ah, bl) + f(al, bh)


def _rms(x, g):
    ms = jnp.mean(x * x, axis=-1, keepdims=True)
    return x * lax.rsqrt(ms + NORM_EPS) * g


def _sigmoid(x):
    return 1.0 / (1.0 + jnp.exp(-x))


def _softplus(x):
    return jnp.maximum(x, 0.0) + jnp.log(1.0 + jnp.exp(-jnp.abs(x)))


def _seg_ones(width):
    r = lax.broadcasted_iota(jnp.int32, (LANES, LANES), 0) // width
    c = lax.broadcasted_iota(jnp.int32, (LANES, LANES), 1) // width
    return jnp.where(r == c, 1.0, 0.0).astype(BF16)


def _resident(shape, layer=None):
    nd = len(shape)
    if layer is None:
        return pl.BlockSpec(shape, lambda *_: (0,) * nd, pipeline_mode=pl.Buffered(1))
    return pl.BlockSpec((None,) + tuple(shape), lambda *_: (layer,) + (0,) * nd, pipeline_mode=pl.Buffered(1))


def _params(sem):
    return pltpu.CompilerParams(dimension_semantics=sem, vmem_limit_bytes=VMEM_LIMIT)


def _mem_kv_kernel(mem_ref, g_ref, wk_ref, wv_ref, k_ref, v_ref):
    mn = _rms(mem_ref[...], g_ref[0]).astype(BF16)
    rows = mn.shape[0]
    for w_ref, o_ref in ((wk_ref, k_ref), (wv_ref, v_ref)):
        z = _dot(mn, w_ref[0])
        for h in range(MEM_HEADS):
            o_ref[0, pl.ds(h, rows, stride=MEM_HEADS), :] = z[:, h * MEM_HEAD:(h + 1) * MEM_HEAD]


def _mem_kv(mem2d, g, wk, wv):
    rows = mem2d.shape[0]
    out = jax.ShapeDtypeStruct((DEPTH, rows * MEM_HEADS, MEM_HEAD), F32)
    return pl.pallas_call(
        _mem_kv_kernel,
        grid=(DEPTH,),
        in_specs=[
            pl.BlockSpec((rows, D_MODEL), lambda l: (0, 0)),
            pl.BlockSpec((1, 1, D_MODEL), lambda l: (l, 0, 0)),
            pl.BlockSpec((1, D_MODEL, MEM_WIDTH), lambda l: (l, 0, 0)),
            pl.BlockSpec((1, D_MODEL, MEM_WIDTH), lambda l: (l, 0, 0)),
        ],
        out_specs=[pl.BlockSpec((1, rows * MEM_HEADS, MEM_HEAD), lambda l: (l, 0, 0))] * 2,
        out_shape=[out, out],
        compiler_params=_params(("arbitrary",)),
        name="mem_kv",
    )(mem2d, g, wk, wv)


def _mem_attend_kernel(q_ref, k_ref, v_ref, o_ref):
    q = q_ref[0]
    M = k_ref.shape[1] // MEM_HEADS
    outs = []
    for h in range(MEM_HEADS):
        sl = slice(h * MEM_HEAD, (h + 1) * MEM_HEAD)
        k = k_ref[0, pl.ds(h, M, stride=MEM_HEADS), :].astype(BF16)
        v = v_ref[0, pl.ds(h, M, stride=MEM_HEADS), :].astype(BF16)
        s = _dot_nt(q[:, sl], k)
        m = jnp.max(s, axis=-1, keepdims=True)
        p = jnp.exp(s - m)
        l = jnp.sum(p, axis=-1, keepdims=True)
        outs.append(_dot(p.astype(BF16), v) / l)
    o_ref[0] = jnp.concatenate(outs, axis=-1).astype(o_ref.dtype)


def _mem_attend(q, mk, mv, tq, layer):
    B, T, _ = q.shape
    MH = mk.shape[2]
    return pl.pallas_call(
        _mem_attend_kernel,
        grid=(B, T // tq),
        in_specs=[
            pl.BlockSpec((1, tq, MEM_WIDTH), lambda b, i: (b, i, 0)),
            pl.BlockSpec((None, 1, MH, MEM_HEAD), lambda b, i: (layer, b, 0, 0)),
            pl.BlockSpec((None, 1, MH, MEM_HEAD), lambda b, i: (layer, b, 0, 0)),
        ],
        out_specs=pl.BlockSpec((1, tq, MEM_WIDTH), lambda b, i: (b, i, 0)),
        out_shape=jax.ShapeDtypeStruct((B, T, MEM_WIDTH), BF16),
        compiler_params=_params(("arbitrary", "arbitrary")),
        name="mem_attend",
    )(q, mk, mv)


def _out_ffn_kernel(x_ref, yt_ref, ym_ref, wo_ref, g_ref, wg_ref, wu_ref, wd_ref, fg_ref, o_ref,
                    *, final, ft):
    x1 = x_ref[0] + (_dot(yt_ref[0], wo_ref[:D_MODEL, :]) + _dot(ym_ref[0], wo_ref[D_MODEL:, :]))
    h = _rms(x1, g_ref[...]).astype(BF16)
    acc = jnp.zeros_like(x1)
    for f in range(0, D_FF, ft):
        gt = _dot(h, wg_ref[:, f:f + ft])
        up = _dot(h, wu_ref[:, f:f + ft])
        act = (gt * _sigmoid(gt) * up).astype(BF16)
        acc = acc + _dot(act, wd_ref[f:f + ft, :])
    acc = x1 + acc
    if final:
        acc = _rms(acc, fg_ref[...])
    o_ref[0] = acc


def _out_ffn(x, ytok, ymem, wo, g, wg, wu, wd, fg, tm, final, layer):
    B, T, _ = x.shape
    row = lambda w: pl.BlockSpec((1, tm, w), lambda b, i: (b, i, 0))
    return pl.pallas_call(
        functools.partial(_out_ffn_kernel, final=final, ft=256),
        grid=(B, T // tm),
        in_specs=[
            row(D_MODEL), row(D_MODEL), row(MEM_WIDTH),
            _resident((D_MODEL + MEM_WIDTH, D_MODEL), layer),
            _resident((1, D_MODEL)),
            _resident((D_MODEL, D_FF), layer), _resident((D_MODEL, D_FF), layer), _resident((D_FF, D_MODEL), layer),
            _resident((1, D_MODEL)),
        ],
        out_specs=row(D_MODEL),
        out_shape=jax.ShapeDtypeStruct((B, T, D_MODEL), F32),
        compiler_params=_params(("arbitrary", "arbitrary")),
        name="out_ffn",
    )(x, ytok, ymem, wo, g, wg, wu, wd, fg)


def _diff_proj_kernel(x_ref, g_ref, w_ref, cos_ref, sin_ref,
                      q_ref, k_ref, kb_ref, v_ref, vb_ref, mq_ref):
    xn = _rms(x_ref[0], g_ref[...]).astype(BF16)
    tm = xn.shape[0]
    cos = cos_ref[...]
    sin = sin_ref[...]
    lane = lax.broadcasted_iota(jnp.int32, (tm, LANES), 1)
    first = (lane % ATTN_HEAD) < (ATTN_HEAD // 2)

    def rope(z):
        rot = jnp.where(first, pltpu.roll(z, LANES - ATTN_HEAD // 2, 1), pltpu.roll(z, ATTN_HEAD // 2, 1))
        return z * cos + rot * sin

    half = D_MODEL // 2
    for c in range(2):
        z = _dot(xn, w_ref[:, c * half:(c + 1) * half])
        for j in range(half // LANES):
            col = c * half + j * LANES
            q_ref[0, :, col:col + LANES] = (rope(z[:, j * LANES:(j + 1) * LANES]) * Q_SCALE).astype(BF16)
    for c in range(2):
        z = _dot(xn, w_ref[:, D_MODEL + c * half:D_MODEL + (c + 1) * half])
        for j in range(half // LANES):
            col = c * half + j * LANES
            kr = rope(z[:, j * LANES:(j + 1) * LANES])
            k_ref[0, pl.ds(col // LANES, tm, stride=ATTN_HEADS), :] = kr
            kb_ref[0, :, col:col + LANES] = kr.astype(BF16)
    for c in range(2):
        z = _dot(xn, w_ref[:, 2 * D_MODEL + c * half:2 * D_MODEL + (c + 1) * half])
        for j in range(half // LANES):
            head = (c * half) // LANES + j
            v_ref[0, pl.ds(head, tm, stride=ATTN_HEADS), :] = z[:, j * LANES:(j + 1) * LANES]
        vb_ref[0, :, c * half:(c + 1) * half] = z.astype(BF16)
    z = _dot(xn, w_ref[:, 3 * D_MODEL:])
    mq_ref[0] = (z * (MEM_HEAD ** -0.5)).astype(BF16)


def _diff_proj(x, g, w, cos_t, sin_t, tm, layer):
    B, T, _ = x.shape
    row = lambda w_: pl.BlockSpec((1, tm, w_), lambda b, i: (b, i, 0))
    sd = lambda w_, dt: jax.ShapeDtypeStruct((B, T, w_), dt)
    by_head = pl.BlockSpec((1, tm * ATTN_HEADS, LANES), lambda b, i: (b, i, 0))
    sd_head = jax.ShapeDtypeStruct((B, T * ATTN_HEADS, LANES), F32)
    return pl.pallas_call(
        _diff_proj_kernel,
        grid=(B, T // tm),
        in_specs=[
            row(D_MODEL), _resident((1, D_MODEL)), _resident((D_MODEL, 3 * D_MODEL + MEM_WIDTH), layer),
            pl.BlockSpec((tm, LANES), lambda b, i: (i, 0)),
            pl.BlockSpec((tm, LANES), lambda b, i: (i, 0)),
        ],
        out_specs=[row(D_MODEL), by_head, row(D_MODEL), by_head, row(D_MODEL), row(MEM_WIDTH)],
        out_shape=[sd(D_MODEL, BF16), sd_head, sd(D_MODEL, BF16), sd_head,
                   sd(D_MODEL, BF16), sd(MEM_WIDTH, BF16)],
        compiler_params=_params(("arbitrary", "arbitrary")),
        name="diff_proj",
    )(x, g, w, cos_t, sin_t)


def _lambda(lam_ref, lam_init):
    lp = lam_ref[...]
    s1 = jnp.sum(lp[0:1] * lp[1:2], axis=-1, keepdims=True)
    s2 = jnp.sum(lp[2:3] * lp[3:4], axis=-1, keepdims=True)
    return jnp.exp(s1) - jnp.exp(s2) + lam_init


def _subln(o, sg, lam_init):
    return o * lax.rsqrt(jnp.mean(o * o, axis=-1, keepdims=True) + SUBLN_EPS) * sg * (1.0 - lam_init)


def _flash_kernel(q_ref, k_ref, v_ref, lam_ref, sg_ref, o_ref, s_sc, p_sc, m_sc, l_sc, acc_sc,
                  *, tq, tk, nh, rb, lam_init):
    qi = pl.program_id(2)
    rows = 2 * tq
    nfull = (qi * tq) // tk
    heads = range(nh)
    hcol = [slice(h * LANES, (h + 1) * LANES) for h in heads]
    lane = lax.broadcasted_iota(jnp.int32, (tq, LANES), 1)

    def stacked(q):
        zero = jnp.zeros_like(q)
        return jnp.concatenate([jnp.where(lane < ATTN_HEAD, q, zero), jnp.where(lane < ATTN_HEAD, zero, q)], axis=0)

    qs = [stacked(q_ref[0, :, hcol[h]]) for h in heads]

    def kv(ref, i, h):
        return ref[0, pl.ds(pl.multiple_of(i * tk, tk), tk), hcol[h]]

    def fold(parts, op):
        while len(parts) > 1:
            parts = [op(parts[i], parts[i + 1]) for i in range(0, len(parts), 2)]
        return parts[0]

    blocks = [slice(r0, r0 + rb) for r0 in range(0, rows, rb)]

    def softmax_rows(h, rs, s, pv):
        cols = [s[:, c:c + LANES] for c in range(0, tk, LANES)]
        m_old = m_sc[h, rs]
        m_new = jnp.maximum(m_old, jnp.max(fold(cols, jnp.maximum), axis=-1, keepdims=True))
        alpha = jnp.exp2(m_old - m_new)
        p = [jnp.exp2(z - m_new) for z in cols]
        l_sc[h, rs] = alpha * l_sc[h, rs] + fold(p, jnp.add)
        acc_sc[h, rs] = alpha * (acc_sc[h, rs] + pv)
        m_sc[h, rs] = m_new
        return jnp.concatenate(p, axis=-1).astype(BF16)

    m_sc[...] = jnp.full(m_sc.shape, _NEG, F32)
    l_sc[...] = jnp.zeros(l_sc.shape, F32)
    acc_sc[...] = jnp.zeros(acc_sc.shape, F32)
    for h in heads:
        p_sc[2 * h + 1] = jnp.zeros(p_sc.shape[1:], BF16)
    for h in heads:
        s_sc[2 * h] = _dot_nt(qs[h], kv(k_ref, 0, h))

    def stage(cur, k):
        for h in heads:
            k_next = kv(k_ref, k + 1, h)
            v_prev = kv(v_ref, jnp.maximum(k - 1, 0), h)
            for rs in blocks:
                s_sc[2 * h + 1 - cur, rs] = _dot_nt(qs[h][rs], k_next)
                pv = _dot(p_sc[2 * h + 1 - cur, rs], v_prev)
                p_sc[2 * h + cur, rs] = softmax_rows(h, rs, s_sc[2 * h + cur, rs], pv)

    def body(j, carry):
        stage(0, 2 * j)
        stage(1, 2 * j + 1)
        return carry

    lax.fori_loop(0, nfull // 2, body, 0)

    def finish(cur):
        lam = _lambda(lam_ref, lam_init)
        kpos = lax.broadcasted_iota(jnp.int32, (rb, tk), 1) + nfull * tk
        for h in heads:
            v_prev = kv(v_ref, jnp.maximum(nfull - 1, 0), h)
            v_last = kv(v_ref, nfull, h)
            for r0 in range(0, tq, rb):
                qpos = lax.broadcasted_iota(jnp.int32, (rb, tk), 0) + (r0 + qi * tq)
                on = []
                for rs in (slice(r0, r0 + rb), slice(tq + r0, tq + r0 + rb)):
                    pv = _dot(p_sc[2 * h + 1 - cur, rs], v_prev)
                    p = softmax_rows(h, rs, jnp.where(kpos <= qpos, s_sc[2 * h + cur, rs], _NEG), pv)
                    on.append((acc_sc[h, rs] + _dot(p, v_last)) / jnp.sum(l_sc[h, rs], axis=-1, keepdims=True))
                o = on[0] - lam * on[1]
                o_ref[0, r0:r0 + rb, hcol[h]] = _subln(o, sg_ref[...], lam_init).astype(o_ref.dtype)

    @pl.when(nfull % 2 == 0)
    def _():
        finish(0)

    @pl.when(nfull % 2 == 1)
    def _():
        stage(0, nfull - 1)
        finish(1)


def _flash(q, k, v, lam_p, sg, lam_init, tq, tk, nh, rb):
    B, T, _ = q.shape
    rows = 2 * tq
    width = nh * LANES
    return pl.pallas_call(
        functools.partial(_flash_kernel, tq=tq, tk=tk, nh=nh, rb=rb, lam_init=lam_init),
        grid=(B, ATTN_HEADS // nh, T // tq),
        in_specs=[
            pl.BlockSpec((1, tq, width), lambda b, h, i: (b, i, h)),
            pl.BlockSpec((1, T, width), lambda b, h, i: (b, 0, h)),
            pl.BlockSpec((1, T, width), lambda b, h, i: (b, 0, h)),
            _resident((4, ATTN_HEAD)), _resident((1, LANES)),
        ],
        out_specs=pl.BlockSpec((1, tq, width), lambda b, h, i: (b, i, h)),
        out_shape=jax.ShapeDtypeStruct((B, T, D_MODEL), BF16),
        scratch_shapes=[pltpu.VMEM((2 * nh, rows, tk), F32), pltpu.VMEM((2 * nh, rows, tk), BF16),
                        pltpu.VMEM((nh, rows, LANES), F32), pltpu.VMEM((nh, rows, LANES), F32),
                        pltpu.VMEM((nh, rows, LANES), F32)],
        compiler_params=_params(("arbitrary", "arbitrary", "arbitrary")),
        name="diff_flash",
    )(q, k, v, lam_p, sg)


def _dec_attn_kernel(*refs, lam_init, n_pg):
    pt_ref, q_ref = refs[0], refs[1]
    k_refs = refs[2:2 + n_pg]
    v_refs = refs[2 + n_pg:2 + 2 * n_pg]
    kn_ref, vn_ref, lam_ref, sg_ref, o_ref, m_sc, l_sc, acc_sc = refs[2 + 2 * n_pg:]
    del pt_ref
    j = pl.program_id(1)
    H = ATTN_HEADS
    G = 2 * H

    @pl.when(j == 0)
    def _():
        m_sc[...] = jnp.full(m_sc.shape, _NEG, F32)
        l_sc[...] = jnp.zeros(l_sc.shape, F32)
        acc_sc[...] = jnp.zeros(acc_sc.shape, F32)

    q8 = q_ref[0]
    lane = lax.broadcasted_iota(jnp.int32, (H, LANES), 1)
    zero = jnp.zeros_like(q8)
    q16 = jnp.concatenate([jnp.where(lane < ATTN_HEAD, q8, zero), jnp.where(lane < ATTN_HEAD, zero, q8)], axis=0)

    n = PAGE_SIZE * H
    same_head = (lax.broadcasted_iota(jnp.int32, (G, n), 0) % H) == (lax.broadcasted_iota(jnp.int32, (G, n), 1) % H)
    s = [jnp.where(same_head, _dot_nt(q16, kr[...].reshape(n, LANES).astype(BF16)), _NEG) for kr in k_refs]
    m_old = m_sc[...]
    m_new = m_old
    for z in s:
        m_new = jnp.maximum(m_new, jnp.max(z, axis=-1, keepdims=True))
    alpha = jnp.exp2(m_old - m_new)
    p = [jnp.exp2(z - m_new) for z in s]
    l_sc[...] = alpha * l_sc[...] + sum(jnp.sum(z, axis=-1, keepdims=True) for z in p)
    pv = sum(_dot(z.astype(BF16), vr[...].reshape(n, LANES).astype(BF16)) for z, vr in zip(p, v_refs))
    acc_sc[...] = alpha * acc_sc[...] + pv
    m_sc[...] = m_new

    @pl.when(j == pl.num_programs(1) - 1)
    def _():
        kn = kn_ref[0]
        vn = vn_ref[0]
        kn16 = jnp.concatenate([kn, kn], axis=0)
        vn16 = jnp.concatenate([vn, vn], axis=0)
        sn = jnp.sum(q16.astype(F32) * kn16, axis=-1, keepdims=True)
        m0 = m_sc[...]
        m1 = jnp.maximum(m0, sn)
        a0 = jnp.exp2(m0 - m1)
        pn = jnp.exp2(sn - m1)
        l1 = a0 * l_sc[...] + pn
        acc1 = a0 * acc_sc[...] + pn * vn16
        on = acc1 / l1
        lam = _lambda(lam_ref, lam_init)
        o = on[:H] - lam * on[H:]
        o_ref[0] = _subln(o, sg_ref[...], lam_init).astype(o_ref.dtype)


def _dec_attn(page_table, q, cache_k, cache_v, kn, vn, lam_p, sg, lam_init, layer):
    B, n_pages = page_table.shape
    H = ATTN_HEADS
    n_pg = math.gcd(DEC_PAGES, n_pages)
    head = lambda: pl.BlockSpec((1, H, LANES), lambda b, j, pt: (b, 0, 0))
    page = lambda i: pl.BlockSpec((None, None, PAGE_SIZE, H, LANES),
                                  lambda b, j, pt: (layer, pt[b, j * n_pg + i], 0, 0, 0))
    pages = [page(i) for i in range(n_pg)]
    grid_spec = pltpu.PrefetchScalarGridSpec(
        num_scalar_prefetch=1,
        grid=(B, n_pages // n_pg),
        in_specs=[head()] + pages + pages + [head(), head(), _resident((4, ATTN_HEAD)), _resident((1, LANES))],
        out_specs=pl.BlockSpec((1, H, LANES), lambda b, j, pt: (b, 0, 0)),
        scratch_shapes=[pltpu.VMEM((2 * H, 1), F32), pltpu.VMEM((2 * H, 1), F32), pltpu.VMEM((2 * H, LANES), F32)],
    )
    return pl.pallas_call(
        functools.partial(_dec_attn_kernel, lam_init=lam_init, n_pg=n_pg),
        grid_spec=grid_spec,
        out_shape=jax.ShapeDtypeStruct((B, H, LANES), BF16),
        compiler_params=_params(("arbitrary", "arbitrary")),
        name="paged_diff_attn",
    )(page_table, q, *([cache_k] * n_pg), *([cache_v] * n_pg), kn, vn, lam_p, sg)


_V_MU, _V_W0, _V_A0, _V_V0, _V_KK, _V_KA, _V_G = 0, 6, 7, 8, 9, 10, 11
_N_VEC = 16


def _rwkv_proj_kernel(*refs, shifted, has_vfirst):
    it = iter(refs)
    x_ref = next(it)
    prev_ref = next(it)
    shift_ref = next(it) if shifted else None
    vf_ref = next(it) if has_vfirst else None
    vec_ref, w_ref, w1_ref, w2_ref, a1_ref, a2_ref = (next(it) for _ in range(6))
    v1_ref, v2_ref = (next(it), next(it)) if has_vfirst else (None, None)
    g1_ref, g2_ref = next(it), next(it)
    r_ref, lw_ref, k_ref, v_ref, a_ref, b_ref, g_ref, mq_ref, xs_ref = (next(it) for _ in range(9))

    vec = vec_ref[...]
    row = lambda i: vec[i:i + 1]
    gain = row(_V_G)
    xn = _rms(x_ref[0], gain)
    tm = xn.shape[0]
    if shifted:
        pr = _rms(prev_ref[0][7:8], gain)
        pr = jnp.where(pl.program_id(1) == 0, shift_ref[0], pr)
        ridx = lax.broadcasted_iota(jnp.int32, (tm, 1), 0)
        xprev = jnp.where(ridx == 0, pr, pltpu.roll(xn, 1, 0))
        xs_ref[0] = xn[tm - 1:tm]
    else:
        xprev = prev_ref[0]
        xs_ref[0] = xn
    xx = xprev - xn
    mix = lambda j: (xn + xx * row(_V_MU + j)).astype(BF16)
    xr, xw, xk, xv, xa, xg = (mix(j) for j in range(6))

    D = D_MODEL
    r = _dot(xr, w_ref[:, :D])
    k = _dot(xk, w_ref[:, D:2 * D])
    v = _dot(xv, w_ref[:, 2 * D:3 * D])
    mq_ref[0] = (_dot(xn.astype(BF16), w_ref[:, 3 * D:]) * (MEM_HEAD ** -0.5)).astype(BF16)

    w_in = row(_V_W0) + _dot(jnp.tanh(_dot(xw, w1_ref[...])).astype(BF16), w2_ref[...])
    w_log = -_softplus(-w_in) - 0.5
    lw_ref[0] = -jnp.exp(w_log)
    if has_vfirst:
        gate = _sigmoid(row(_V_V0) + _dot(_dot(xv, v1_ref[...]).astype(BF16), v2_ref[...]))
        v = v + (vf_ref[0] - v) * gate
    a = _sigmoid(row(_V_A0) + _dot(_dot(xa, a1_ref[...]).astype(BF16), a2_ref[...]))
    g_ref[0] = _dot(_sigmoid(_dot(xg, g1_ref[...])).astype(BF16), g2_ref[...])

    seg = _seg_ones(RWKV_HEAD)
    kk = k * row(_V_KK)
    sq = kk * kk
    n2 = jnp.concatenate(
        [sum(_dot(p_, seg) for p_ in _split2(sq[:, c:c + LANES])) for c in range(0, D, LANES)], axis=-1)
    kk = kk / jnp.maximum(jnp.sqrt(n2), 1e-12)
    r_ref[0] = r
    k_ref[0] = k * (1.0 + (a - 1.0) * row(_V_KA))
    v_ref[0] = v
    a_ref[0] = -kk
    b_ref[0] = kk * a


def _rwkv_proj(x, prev, shift, vfirst, vec, w, loras, tm, layer):
    B, T, _ = x.shape
    shifted = shift is not None
    has_vfirst = vfirst is not None
    row = lambda w_: pl.BlockSpec((1, tm, w_), lambda b, i: (b, i, 0))
    ins, specs = [x], [row(D_MODEL)]
    if shifted:
        ins += [prev, shift]
        specs += [pl.BlockSpec((1, 8, D_MODEL), lambda b, i: (b, jnp.maximum(i * (tm // 8) - 1, 0), 0)),
                  pl.BlockSpec((1, 1, D_MODEL), lambda b, i: (b, 0, 0))]
    else:
        ins += [prev]
        specs += [row(D_MODEL)]
    if has_vfirst:
        ins.append(vfirst)
        specs.append(row(D_MODEL))
    w1, w2, a1, a2, v1, v2, g1, g2 = loras
    small = [w1, w2, a1, a2] + ([v1, v2] if has_vfirst else []) + [g1, g2]
    ins += [vec, w] + small
    specs += [_resident(vec.shape), _resident(w.shape[1:], layer)] + [_resident(s.shape) for s in small]
    sd = lambda dt, w_=D_MODEL, t_=T: jax.ShapeDtypeStruct((B, t_, w_), dt)
    xs_rows = 1 if shifted else T
    xs_spec = (pl.BlockSpec((1, 1, D_MODEL), lambda b, i: (b, 0, 0)) if shifted else row(D_MODEL))
    return pl.pallas_call(
        functools.partial(_rwkv_proj_kernel, shifted=shifted, has_vfirst=has_vfirst),
        grid=(B, T // tm),
        in_specs=specs,
        out_specs=[row(D_MODEL)] * 7 + [row(MEM_WIDTH), xs_spec],
        out_shape=[sd(F32)] * 7 + [sd(BF16, MEM_WIDTH), sd(F32, D_MODEL, xs_rows)],
        compiler_params=_params(("arbitrary", "arbitrary")),
        name="rwkv_proj",
    )(*ins)


def _wkv_kernel(r_ref, lw_ref, k_ref, v_ref, a_ref, b_ref, g_ref, vec_ref, s0_ref, y_ref, so_ref, s_sc,
                *, L, NB, P, passes):
    c = pl.program_id(2)
    sel = [(bb, p, slice(p * LANES, (p + 1) * LANES)) for bb in range(NB) for p in range(P)]

    @pl.when(c == 0)
    def _():
        for n, (bb, p, _) in enumerate(sel):
            s_sc[n] = s0_ref[bb, p]

    L2 = 2 * L
    tri = jnp.where(lax.broadcasted_iota(jnp.int32, (L, L), 0) >= lax.broadcasted_iota(jnp.int32, (L, L), 1),
                    1.0, 0.0).astype(BF16)
    r2 = lax.broadcasted_iota(jnp.int32, (L2, L2), 0)
    c2 = lax.broadcasted_iota(jnp.int32, (L2, L2), 1)
    strict = (r2 % L) > (c2 % L)
    incl = (r2 % L) >= (c2 % L)
    eye = jnp.where(r2 == c2, 1.0, 0.0)
    m0 = lax.broadcasted_iota(jnp.int32, (L, LANES), 1) < RWKV_HEAD
    seg = _seg_ones(RWKV_HEAD)
    mm = functools.partial(_mm, passes=passes)

    def stack(z):
        return jnp.concatenate([jnp.where(m0, z, 0.0), jnp.where(m0, 0.0, z)], axis=0)

    def segsum(z):
        hi, lo = _split2(z)
        return _dot(hi, seg) + _dot(lo, seg)

    pairs = range(len(sel))
    r = [r_ref[bb, :, sl] for bb, _, sl in sel]
    lw = [lw_ref[bb, :, sl] for bb, _, sl in sel]
    k = [k_ref[bb, :, sl] for bb, _, sl in sel]
    v = [v_ref[bb, :, sl] for bb, _, sl in sel]
    a = [a_ref[bb, :, sl] for bb, _, sl in sel]
    b = [b_ref[bb, :, sl] for bb, _, sl in sel]
    cum = [_dot_exact_lhs(tri, z) for z in lw]
    c_end = [z[L - 1:L, :] for z in cum]
    e_neg = [jnp.exp(-z) for z in cum]
    at_s = [stack(a[p] * jnp.exp(cum[p] - lw[p])) for p in pairs]
    rt_s = [stack(r[p] * jnp.exp(cum[p])) for p in pairs]
    bt_s = [stack(b[p] * e_neg[p]) for p in pairs]
    kt_s = [stack(k[p] * e_neg[p]) for p in pairs]
    v_s = [stack(z) for z in v]
    S = [s_sc[p] for p in pairs]

    n_ab = [jnp.where(strict, mm(at_s[p], bt_s[p], "nt"), 0.0) for p in pairs]
    a_ak = [jnp.where(strict, mm(at_s[p], kt_s[p], "nt"), 0.0) for p in pairs]
    a_rb = [jnp.where(incl, mm(rt_s[p], bt_s[p], "nt"), 0.0) for p in pairs]
    a_rk = [jnp.where(incl, mm(rt_s[p], kt_s[p], "nt"), 0.0) for p in pairs]

    t_inv = [eye + z for z in n_ab]
    pw = n_ab
    for _ in range(int(math.log2(L)) - 1):
        pw = [mm(z, z, "nn") for z in pw]
        t_inv = [t_inv[p] + mm(t_inv[p], pw[p], "nn") for p in pairs]

    x_s = [mm(at_s[p], S[p], "nt") + mm(a_ak[p], v_s[p], "nn") for p in pairs]
    u_s = [mm(t_inv[p], x_s[p], "nn") for p in pairs]
    y_s = [mm(rt_s[p], S[p], "nt") + mm(a_rb[p], u_s[p], "nn") + mm(a_rk[p], v_s[p], "nn") for p in pairs]
    for p in pairs:
        e_end = jnp.exp(c_end[p] - cum[p])
        s_sc[p] = (S[p] * jnp.exp(c_end[p]) + mm(u_s[p], stack(b[p] * e_end), "tn")
                   + mm(v_s[p], stack(k[p] * e_end), "tn"))

    y = [z[:L] + z[L:] for z in y_s]
    yc = [z - segsum(z) * (1.0 / RWKV_HEAD) for z in y]
    var = [segsum(z * z) * (1.0 / RWKV_HEAD) for z in yc]
    for p, (bb, _, sl) in enumerate(sel):
        vec = vec_ref[:, sl]
        yn = yc[p] * lax.rsqrt(var[p] + GN_EPS) * vec[0:1] + vec[1:2]
        bonus = segsum(r[p] * k[p] * vec[2:3]) * v[p]
        y_ref[bb, :, sl] = ((yn + bonus) * g_ref[bb, :, sl]).astype(y_ref.dtype)

    @pl.when(c == pl.num_programs(2) - 1)
    def _():
        for n, (bb, p, _) in enumerate(sel):
            so_ref[bb, p] = s_sc[n]


def _wkv(r, lw, k, v, a, b, g, vec, s0, L, NB, P, passes):
    B, T, _ = r.shape
    npair = D_MODEL // LANES
    width = P * LANES
    tile = lambda: pl.BlockSpec((NB, L, width), lambda bb, pg, c: (bb, c, pg))
    state = lambda: pl.BlockSpec((NB, P, LANES, LANES), lambda bb, pg, c: (bb, pg, 0, 0))
    return pl.pallas_call(
        functools.partial(_wkv_kernel, L=L, NB=NB, P=P, passes=passes),
        grid=(B // NB, npair // P, T // L),
        in_specs=[tile() for _ in range(7)] + [pl.BlockSpec((8, width), lambda bb, pg, c: (0, pg)), state()],
        out_specs=[tile(), state()],
        out_shape=[jax.ShapeDtypeStruct((B, T, D_MODEL), BF16),
                   jax.ShapeDtypeStruct((B, npair, LANES, LANES), F32)],
        scratch_shapes=[pltpu.VMEM((NB * P, LANES, LANES), F32)],
        compiler_params=_params(("arbitrary", "arbitrary", "arbitrary")),
        name="wkv_chunked",
    )(r, lw, k, v, a, b, g, vec, s0)


def _wkv_step_kernel(r_ref, lw_ref, k_ref, v_ref, a_ref, b_ref, g_ref, vec_ref, s_ref, y_ref, so_ref, *, nb):
    hd = RWKV_HEAD
    npair = D_MODEL // LANES
    rid = lax.broadcasted_iota(jnp.int32, (8, LANES), 0)
    lid = lax.broadcasted_iota(jnp.int32, (8, LANES), 1)
    own = ((rid == 0) & (lid < hd)) | ((rid == 1) & (lid >= hd))
    seg = _seg_ones(hd)
    zpad = jnp.zeros((hd, hd), F32)
    cols = [slice(p * LANES, (p + 1) * LANES) for p in range(npair)]
    combos = [(i, p) for i in range(nb) for p in range(npair)]

    def vec_row(ref, i, p):
        return ref[i:i + 1, cols[p]]

    def at_row(z, i):
        return jnp.where(rid == i, z, 0.0).astype(BF16)

    def split_heads(z):
        return jnp.where(own, z, 0.0)

    S = [jnp.concatenate([jnp.concatenate([s_ref[i, 2 * p], zpad], axis=1),
                          jnp.concatenate([zpad, s_ref[i, 2 * p + 1]], axis=1)], axis=0) for i, p in combos]
    Sb = [z.astype(BF16) for z in S]
    u = [_dot_nt(at_row(vec_row(a_ref, i, p), 0), Sb[n])[0:1] for n, (i, p) in enumerate(combos)]
    lhs = [jnp.concatenate([split_heads(u[n]), split_heads(vec_row(v_ref, i, p))], axis=0).astype(BF16)
           for n, (i, p) in enumerate(combos)]
    rhs = [jnp.concatenate([split_heads(vec_row(b_ref, i, p)), split_heads(vec_row(k_ref, i, p))], axis=0).astype(BF16)
           for i, p in combos]
    S = [S[n] * jnp.exp(vec_row(lw_ref, i, p)) + _dot_tn(lhs[n], rhs[n]) for n, (i, p) in enumerate(combos)]
    for n, (i, p) in enumerate(combos):
        so_ref[i, 2 * p] = S[n][:hd, :hd]
        so_ref[i, 2 * p + 1] = S[n][hd:, hd:]
    yrow = [_dot_nt(at_row(vec_row(r_ref, i, p), i), S[n].astype(BF16)) for n, (i, p) in enumerate(combos)]

    def segsum(z):
        hi, lo = _split2(z)
        return _dot(hi, seg) + _dot(lo, seg)

    for p in range(npair):
        y = yrow[p]
        for i in range(1, nb):
            y = y + yrow[i * npair + p]
        yc = y - segsum(y) * (1.0 / hd)
        var = segsum(yc * yc) * (1.0 / hd)
        vec = vec_ref[:, cols[p]]
        yn = yc * lax.rsqrt(var + GN_EPS) * vec[0:1] + vec[1:2]
        bonus = segsum(r_ref[:, cols[p]] * k_ref[:, cols[p]] * vec[2:3]) * v_ref[:, cols[p]]
        y_ref[:, cols[p]] = (yn + bonus) * g_ref[:, cols[p]]


def _wkv_step(r, lw, k, v, a, b, g, vec, s0, nb=8):
    n_seq = r.shape[0]
    heads = D_MODEL // RWKV_HEAD
    tile = lambda: pl.BlockSpec((nb, D_MODEL), lambda i: (i, 0))
    state = lambda: pl.BlockSpec((nb, heads, RWKV_HEAD, RWKV_HEAD), lambda i: (i, 0, 0, 0))
    return pl.pallas_call(
        functools.partial(_wkv_step_kernel, nb=nb),
        grid=(n_seq // nb,),
        in_specs=[tile() for _ in range(7)] + [_resident((8, D_MODEL)), state()],
        out_specs=[tile(), state()],
        out_shape=[jax.ShapeDtypeStruct((n_seq, D_MODEL), F32),
                   jax.ShapeDtypeStruct((n_seq, heads, RWKV_HEAD, RWKV_HEAD), F32)],
        compiler_params=_params(("arbitrary",)),
        name="wkv_step",
    )(r, lw, k, v, a, b, g, vec, s0)


def _state_to_pairs(s):
    B = s.shape[0]
    s = s.reshape(B, -1, 2, RWKV_HEAD, RWKV_HEAD)
    z = jnp.zeros_like(s[:, :, 0])
    top = jnp.concatenate([s[:, :, 0], z], axis=-1)
    bot = jnp.concatenate([z, s[:, :, 1]], axis=-1)
    return jnp.concatenate([top, bot], axis=-2)


def _state_from_pairs(sp):
    B = sp.shape[0]
    h = RWKV_HEAD
    return jnp.stack([sp[:, :, :h, :h], sp[:, :, h:, h:]], axis=2).reshape(B, -1, h, h)


def _rope_tables(pos):
    half = ATTN_HEAD // 2
    inv = jnp.power(ROPE_THETA, -jnp.arange(half, dtype=F32) * 2.0 / ATTN_HEAD)
    ang = pos.astype(F32)[:, None] * inv[None, :]
    cos = jnp.cos(ang)
    sin = jnp.sin(ang)
    reps = LANES // ATTN_HEAD
    return jnp.tile(cos, (1, 2 * reps)), jnp.tile(jnp.concatenate([-sin, sin], axis=1), (1, reps))


def _pad_cols(w):
    return jnp.pad(w, ((0, 0), (0, LORA_PAD - w.shape[1])))


def _pad_rows(w):
    return jnp.pad(w, ((0, LORA_PAD - w.shape[0]), (0, 0)))


def _pad_tokens(z, t):
    return jnp.pad(z, ((0, 0), (0, t - z.shape[1]), (0, 0)))


def _trunk(x, pos, decode, shift0, wkv0, mem_k, mem_v, past, W):
    B, T, _ = x.shape
    if decode:
        xf = x.reshape(1, B, D_MODEL)
        tm = B
    else:
        xf = x
        tm = 512
    cos_t, sin_t = _rope_tables(pos if not decode else jnp.broadcast_to(pos, (B,)))
    shifts, states, ks, vs = [], [], [], []
    v_first = None
    for l in range(DEPTH):
        idx = l // 2
        if l % 2 == 0:
            vec = jnp.concatenate([
                W["mu"][idx], W["w0"][idx][None], W["a0"][idx][None],
                (W["v0"][idx - 1] if idx > 0 else jnp.zeros((D_MODEL,), F32))[None],
                W["k_k"][idx][None], W["k_a"][idx][None], W["mix_g"][l][None],
                jnp.zeros((_N_VEC - 12, D_MODEL), F32)], axis=0)
            loras = (W["w1"][idx], W["w2"][idx], W["a1"][idx], W["a2"][idx],
                     W["v1"][idx - 1] if idx > 0 else None, W["v2"][idx - 1] if idx > 0 else None,
                     W["g1"][idx], W["g2"][idx])
            if decode:
                outs = _rwkv_proj(xf, shift0[idx][None], None, v_first, vec, W["w_in"], loras, tm, l)
            else:
                outs = _rwkv_proj(xf, xf, shift0[idx][:, None], v_first, vec, W["w_in"], loras, tm // 2, l)
            r, lw, k, v, a, b, g, mq, xs = outs
            if idx == 0:
                v_first = v
            shifts.append(xs.reshape(B, D_MODEL))
            scan_in = [r, lw, k, v, a, b, g]
            vec2 = jnp.concatenate([W["lnx_g"][idx][None], W["lnx_b"][idx][None], W["r_k"][idx][None],
                                    jnp.zeros((5, D_MODEL), F32)], axis=0)
            if decode:
                y, s_new = _wkv_step(*[z.reshape(B, D_MODEL) for z in scan_in], vec2, wkv0[idx])
                states.append(s_new)
                y_tok = y.astype(BF16).reshape(1, B, D_MODEL)
            else:
                y_tok, s_new = _wkv(*scan_in, vec2, _state_to_pairs(wkv0[idx]), WKV_CHUNK, B, WKV_PAIRS, 1)
                states.append(_state_from_pairs(s_new))
        else:
            lam_init = 0.8 - 0.6 * math.exp(-0.3 * l)
            q, kf, kb, vf, vb, mq = _diff_proj(xf, W["mix_g"][l][None], W["w_in"], cos_t, sin_t, tm, l)
            lam_p = jnp.stack([W["lam_q1"][idx], W["lam_k1"][idx], W["lam_q2"][idx], W["lam_k2"][idx]])
            sg = W["subln_g"][idx][None]
            if decode:
                cache_k, cache_v, page_table = past
                hd = lambda z: z.reshape(B, ATTN_HEADS, LANES)
                o = _dec_attn(page_table, hd(q), cache_k, cache_v, hd(kf), hd(vf), lam_p, sg, lam_init, idx)
                y_tok = o.reshape(1, B, D_MODEL)
            else:
                y_tok = _flash(q, kb, vb, lam_p, sg, lam_init, 512, 512, 1, 256)
            ks.append(kf.reshape(B, T, ATTN_HEADS, LANES))
            vs.append(vf.reshape(B, T, ATTN_HEADS, LANES))
        if decode:
            mq8 = _pad_tokens(mq.reshape(B, 1, MEM_WIDTH), 8)
            y_mem = _mem_attend(mq8, mem_k, mem_v, 8, l)[:, :1].reshape(1, B, MEM_WIDTH)
        else:
            y_mem = _mem_attend(mq, mem_k, mem_v, 2 * tm, l)
        xf = _out_ffn(xf, y_tok, y_mem, W["w_out"], W["ffn_g"][l][None], W["w_gate"], W["w_up"],
                      W["w_down"], W["final_g"][None], tm, l == DEPTH - 1, l)
    return xf.reshape(B, T, D_MODEL), jnp.stack(shifts), jnp.stack(states), jnp.stack(ks), jnp.stack(vs)


def kernel(x_prompt, x_sample, cache_k, cache_v, cache_mem_k, cache_mem_v, state_rwkv_wkv, state_rwkv_shift, page_table, mem_prompt, w_in, w_out, mix_norm_g, ffn_norm_g, w_gate, w_up, w_down, final_norm_g, mem_norm_g, w_mem_k, w_mem_v, rwkv_mu, rwkv_w0, rwkv_w1, rwkv_w2, rwkv_a0, rwkv_a1, rwkv_a2, rwkv_v0, rwkv_v1, rwkv_v2, rwkv_g1, rwkv_g2, rwkv_k_k, rwkv_k_a, rwkv_r_k, rwkv_lnx_g, rwkv_lnx_b, diff_lam_q1, diff_lam_k1, diff_lam_q2, diff_lam_k2, diff_subln_g):
    bf = lambda z: z.astype(BF16)
    n_rwkv = rwkv_mu.shape[0]
    W = dict(
        w_in=bf(w_in), w_out=bf(w_out), mix_g=mix_norm_g, ffn_g=ffn_norm_g,
        w_gate=bf(w_gate), w_up=bf(w_up), w_down=bf(w_down), final_g=final_norm_g,
        mu=rwkv_mu, w0=rwkv_w0, a0=rwkv_a0, v0=rwkv_v0, k_k=rwkv_k_k, k_a=rwkv_k_a,
        w1=[bf(_pad_cols(rwkv_w1[i])) for i in range(n_rwkv)],
        w2=[bf(_pad_rows(rwkv_w2[i])) for i in range(n_rwkv)],
        a1=[bf(_pad_cols(rwkv_a1[i])) for i in range(n_rwkv)],
        a2=[bf(_pad_rows(rwkv_a2[i])) for i in range(n_rwkv)],
        v1=[bf(_pad_cols(rwkv_v1[i])) for i in range(n_rwkv - 1)],
        v2=[bf(_pad_rows(rwkv_v2[i])) for i in range(n_rwkv - 1)],
        g1=bf(rwkv_g1), g2=bf(rwkv_g2),
        r_k=rwkv_r_k.reshape(n_rwkv, D_MODEL), lnx_g=rwkv_lnx_g, lnx_b=rwkv_lnx_b,
        lam_q1=diff_lam_q1, lam_k1=diff_lam_k1, lam_q2=diff_lam_q2, lam_k2=diff_lam_k2,
        subln_g=diff_subln_g,
    )

    B, T, _ = x_prompt.shape
    M = mem_prompt.shape[1]
    mk, mv = _mem_kv(mem_prompt.reshape(B * M, D_MODEL), mem_norm_g[:, None], bf(w_mem_k), bf(w_mem_v))
    mk = mk.reshape(DEPTH, B, M * MEM_HEADS, MEM_HEAD)
    mv = mv.reshape(DEPTH, B, M * MEM_HEADS, MEM_HEAD)
    p_mem_k = mk.reshape(DEPTH, B, M, MEM_HEADS, MEM_HEAD)
    p_mem_v = mv.reshape(DEPTH, B, M, MEM_HEADS, MEM_HEAD)

    pos_p = jnp.arange(T, dtype=jnp.int32)
    shift0 = jnp.zeros((n_rwkv, B, D_MODEL), F32)
    wkv0 = jnp.zeros((n_rwkv, B, D_MODEL // RWKV_HEAD, RWKV_HEAD, RWKV_HEAD), F32)
    y_prompt, p_shift, p_wkv, p_k, p_v = _trunk(x_prompt, pos_p, False, shift0, wkv0, mk, mv, None, W)

    Bs = x_sample.shape[0]
    past_len = page_table.shape[1] * PAGE_SIZE
    pos_s = past_len + jnp.arange(x_sample.shape[1], dtype=jnp.int32)
    smk = cache_mem_k.reshape(DEPTH, Bs, -1, MEM_HEAD)
    smv = cache_mem_v.reshape(DEPTH, Bs, -1, MEM_HEAD)
    y_sample, s_shift, s_wkv, s_k, s_v = _trunk(
        x_sample, pos_s, True, state_rwkv_shift, state_rwkv_wkv, smk, smv, (cache_k, cache_v, page_table), W)

    return (y_prompt, y_sample, p_wkv, p_shift, p_k, p_v, p_mem_k, p_mem_v, s_wkv, s_shift, s_k, s_v)
```

```python
import functools
import math

import jax
import jax.numpy as jnp
from jax import lax
from jax.experimental import pallas as pl
from jax.experimental.pallas import tpu as pltpu

F32 = jnp.float32
BF16 = jnp.bfloat16

D_MODEL = 1024
DEPTH = 4
PAGE_SIZE = 128
RWKV_HEAD = 64
ATTN_HEAD = 64
ATTN_HEADS = D_MODEL // (2 * ATTN_HEAD)
MEM_HEADS = 4
MEM_HEAD = 128
MEM_WIDTH = MEM_HEADS * MEM_HEAD
D_FF = 2816
ROPE_THETA = 10000.0
NORM_EPS = 1e-6
SUBLN_EPS = 1e-5
GN_EPS = 1e-5 * RWKV_HEAD

Q_SCALE = ATTN_HEAD ** -0.5 * math.log2(math.e)
LANES = 128
LORA_PAD = 128
WKV_CHUNK = 64
WKV_PAIRS = 8
DEC_PAGES = 8
VMEM_LIMIT = 56 * 1024 * 1024

_NT = (((1,), (1,)), ((), ()))
_TN = (((0,), (0,)), ((), ()))
_NEG = -1e30


def _dot(a, b):
    return jnp.dot(a, b, preferred_element_type=F32)


def _dot_nt(a, b):
    return lax.dot_general(a, b, _NT, preferred_element_type=F32)


def _dot_tn(a, b):
    return lax.dot_general(a, b, _TN, preferred_element_type=F32)


def _split2(x):
    hi = x.astype(BF16)
    lo = (x - hi.astype(F32)).astype(BF16)
    return hi, lo


def _split3(x):
    h1 = x.astype(BF16)
    r1 = x - h1.astype(F32)
    h2 = r1.astype(BF16)
    h3 = (r1 - h2.astype(F32)).astype(BF16)
    return h1, h2, h3


def _dot_exact_rhs(x, m_bf16):
    h1, h2, h3 = _split3(x)
    return _dot(h1, m_bf16) + _dot(h2, m_bf16) + _dot(h3, m_bf16)


def _dot_exact_lhs(m_bf16, x):
    h1, h2, h3 = _split3(x)
    return _dot(m_bf16, h1) + _dot(m_bf16, h2) + _dot(m_bf16, h3)


def _mm(a, b, kind, passes):
    f = {"nn": _dot, "nt": _dot_nt, "tn": _dot_tn}[kind]
    if passes == 1:
        return f(a.astype(BF16), b.astype(BF16))
    ah, al = _split2(a)
    bh, bl = _split2(b)
    return f(ah, bh) + f(ah, bl) + f(al, bh)


def _rms(x, g):
    ms = jnp.mean(x * x, axis=-1, keepdims=True)
    return x * lax.rsqrt(ms + NORM_EPS) * g


def _sigmoid(x):
    return 1.0 / (1.0 + jnp.exp(-x))


def _softplus(x):
    return jnp.maximum(x, 0.0) + jnp.log(1.0 + jnp.exp(-jnp.abs(x)))


def _seg_ones(width):
    r = lax.broadcasted_iota(jnp.int32, (LANES, LANES), 0) // width
    c = lax.broadcasted_iota(jnp.int32, (LANES, LANES), 1) // width
    return jnp.where(r == c, 1.0, 0.0).astype(BF16)


def _resident(shape, layer=None):
    nd = len(shape)
    if layer is None:
        return pl.BlockSpec(shape, lambda *_: (0,) * nd, pipeline_mode=pl.Buffered(1))
    return pl.BlockSpec((None,) + tuple(shape), lambda *_: (layer,) + (0,) * nd, pipeline_mode=pl.Buffered(1))


def _params(sem):
    return pltpu.CompilerParams(dimension_semantics=sem, vmem_limit_bytes=VMEM_LIMIT)


def _mem_kv_kernel(mem_ref, g_ref, wk_ref, wv_ref, k_ref, v_ref):
    mn = _rms(mem_ref[...], g_ref[0]).astype(BF16)
    rows = mn.shape[0]
    for w_ref, o_ref in ((wk_ref, k_ref), (wv_ref, v_ref)):
        z = _dot(mn, w_ref[0])
        for h in range(MEM_HEADS):
            o_ref[0, pl.ds(h, rows, stride=MEM_HEADS), :] = z[:, h * MEM_HEAD:(h + 1) * MEM_HEAD]


def _mem_kv(mem2d, g, wk, wv):
    rows = mem2d.shape[0]
    out = jax.ShapeDtypeStruct((DEPTH, rows * MEM_HEADS, MEM_HEAD), F32)
    return pl.pallas_call(
        _mem_kv_kernel,
        grid=(DEPTH,),
        in_specs=[
            pl.BlockSpec((rows, D_MODEL), lambda l: (0, 0)),
            pl.BlockSpec((1, 1, D_MODEL), lambda l: (l, 0, 0)),
            pl.BlockSpec((1, D_MODEL, MEM_WIDTH), lambda l: (l, 0, 0)),
            pl.BlockSpec((1, D_MODEL, MEM_WIDTH), lambda l: (l, 0, 0)),
        ],
        out_specs=[pl.BlockSpec((1, rows * MEM_HEADS, MEM_HEAD), lambda l: (l, 0, 0))] * 2,
        out_shape=[out, out],
        compiler_params=_params(("arbitrary",)),
        name="mem_kv",
    )(mem2d, g, wk, wv)


def _mem_attend_kernel(q_ref, k_ref, v_ref, o_ref):
    M = k_ref.shape[1] // MEM_HEADS
    for i in range(q_ref.shape[0]):
        q = q_ref[i]
        outs = []
        for h in range(MEM_HEADS):
            sl = slice(h * MEM_HEAD, (h + 1) * MEM_HEAD)
            k = k_ref[i, pl.ds(h, M, stride=MEM_HEADS), :].astype(BF16)
            v = v_ref[i, pl.ds(h, M, stride=MEM_HEADS), :].astype(BF16)
            s = _dot_nt(q[:, sl], k)
            m = jnp.max(s, axis=-1, keepdims=True)
            p = jnp.exp(s - m)
            l = jnp.sum(p, axis=-1, keepdims=True)
            outs.append(_dot(p.astype(BF16), v) / l)
        o_ref[i] = jnp.concatenate(outs, axis=-1).astype(o_ref.dtype)


def _mem_attend(q, mk, mv, tq, nb, layer):
    B, T, _ = q.shape
    MH = mk.shape[2]
    return pl.pallas_call(
        _mem_attend_kernel,
        grid=(B // nb, T // tq),
        in_specs=[
            pl.BlockSpec((nb, tq, MEM_WIDTH), lambda b, i: (b, i, 0)),
            pl.BlockSpec((None, nb, MH, MEM_HEAD), lambda b, i: (layer, b, 0, 0)),
            pl.BlockSpec((None, nb, MH, MEM_HEAD), lambda b, i: (layer, b, 0, 0)),
        ],
        out_specs=pl.BlockSpec((nb, tq, MEM_WIDTH), lambda b, i: (b, i, 0)),
        out_shape=jax.ShapeDtypeStruct((B, T, MEM_WIDTH), BF16),
        compiler_params=_params(("arbitrary", "arbitrary")),
        name="mem_attend",
    )(q, mk, mv)


def _out_ffn_kernel(x_ref, yt_ref, ym_ref, wo_ref, g_ref, wg_ref, wu_ref, wd_ref, fg_ref, o_ref,
                    *, final, ft):
    x1 = x_ref[0] + (_dot(yt_ref[0], wo_ref[:D_MODEL, :]) + _dot(ym_ref[0], wo_ref[D_MODEL:, :]))
    h = _rms(x1, g_ref[...]).astype(BF16)
    acc = jnp.zeros_like(x1)
    for f in range(0, D_FF, ft):
        gt = _dot(h, wg_ref[:, f:f + ft])
        up = _dot(h, wu_ref[:, f:f + ft])
        act = (gt * _sigmoid(gt) * up).astype(BF16)
        acc = acc + _dot(act, wd_ref[f:f + ft, :])
    acc = x1 + acc
    if final:
        acc = _rms(acc, fg_ref[...])
    o_ref[0] = acc


def _out_ffn(x, ytok, ymem, wo, g, wg, wu, wd, fg, tm, final, layer):
    B, T, _ = x.shape
    row = lambda w: pl.BlockSpec((1, tm, w), lambda b, i: (b, i, 0))
    return pl.pallas_call(
        functools.partial(_out_ffn_kernel, final=final, ft=256),
        grid=(B, T // tm),
        in_specs=[
            row(D_MODEL), row(D_MODEL), row(MEM_WIDTH),
            _resident((D_MODEL + MEM_WIDTH, D_MODEL), layer),
            _resident((1, D_MODEL)),
            _resident((D_MODEL, D_FF), layer), _resident((D_MODEL, D_FF), layer), _resident((D_FF, D_MODEL), layer),
            _resident((1, D_MODEL)),
        ],
        out_specs=row(D_MODEL),
        out_shape=jax.ShapeDtypeStruct((B, T, D_MODEL), F32),
        compiler_params=_params(("arbitrary", "arbitrary")),
        name="out_ffn",
    )(x, ytok, ymem, wo, g, wg, wu, wd, fg)


def _diff_proj_kernel(x_ref, g_ref, w_ref, cos_ref, sin_ref,
                      q_ref, k_ref, kb_ref, v_ref, vb_ref, mq_ref):
    xn = _rms(x_ref[0], g_ref[...]).astype(BF16)
    tm = xn.shape[0]
    cos = cos_ref[...]
    sin = sin_ref[...]
    lane = lax.broadcasted_iota(jnp.int32, (tm, LANES), 1)
    first = (lane % ATTN_HEAD) < (ATTN_HEAD // 2)

    def rope(z):
        rot = jnp.where(first, pltpu.roll(z, LANES - ATTN_HEAD // 2, 1), pltpu.roll(z, ATTN_HEAD // 2, 1))
        return z * cos + rot * sin

    half = D_MODEL // 2
    for c in range(2):
        z = _dot(xn, w_ref[:, c * half:(c + 1) * half])
        for j in range(half // LANES):
            col = c * half + j * LANES
            q_ref[0, :, col:col + LANES] = (rope(z[:, j * LANES:(j + 1) * LANES]) * Q_SCALE).astype(BF16)
    for c in range(2):
        z = _dot(xn, w_ref[:, D_MODEL + c * half:D_MODEL + (c + 1) * half])
        for j in range(half // LANES):
            col = c * half + j * LANES
            kr = rope(z[:, j * LANES:(j + 1) * LANES])
            k_ref[0, pl.ds(col // LANES, tm, stride=ATTN_HEADS), :] = kr
            kb_ref[0, :, col:col + LANES] = kr.astype(BF16)
    for c in range(2):
        z = _dot(xn, w_ref[:, 2 * D_MODEL + c * half:2 * D_MODEL + (c + 1) * half])
        for j in range(half // LANES):
            head = (c * half) // LANES + j
            v_ref[0, pl.ds(head, tm, stride=ATTN_HEADS), :] = z[:, j * LANES:(j + 1) * LANES]
        vb_ref[0, :, c * half:(c + 1) * half] = z.astype(BF16)
    z = _dot(xn, w_ref[:, 3 * D_MODEL:])
    mq_ref[0] = (z * (MEM_HEAD ** -0.5)).astype(BF16)


def _diff_proj(x, g, w, cos_t, sin_t, tm, layer):
    B, T, _ = x.shape
    row = lambda w_: pl.BlockSpec((1, tm, w_), lambda b, i: (b, i, 0))
    sd = lambda w_, dt: jax.ShapeDtypeStruct((B, T, w_), dt)
    by_head = pl.BlockSpec((1, tm * ATTN_HEADS, LANES), lambda b, i: (b, i, 0))
    sd_head = jax.ShapeDtypeStruct((B, T * ATTN_HEADS, LANES), F32)
    return pl.pallas_call(
        _diff_proj_kernel,
        grid=(B, T // tm),
        in_specs=[
            row(D_MODEL), _resident((1, D_MODEL)), _resident((D_MODEL, 3 * D_MODEL + MEM_WIDTH), layer),
            pl.BlockSpec((tm, LANES), lambda b, i: (i, 0)),
            pl.BlockSpec((tm, LANES), lambda b, i: (i, 0)),
        ],
        out_specs=[row(D_MODEL), by_head, row(D_MODEL), by_head, row(D_MODEL), row(MEM_WIDTH)],
        out_shape=[sd(D_MODEL, BF16), sd_head, sd(D_MODEL, BF16), sd_head,
                   sd(D_MODEL, BF16), sd(MEM_WIDTH, BF16)],
        compiler_params=_params(("arbitrary", "arbitrary")),
        name="diff_proj",
    )(x, g, w, cos_t, sin_t)


def _lambda(lam_ref, lam_init):
    lp = lam_ref[...]
    s1 = jnp.sum(lp[0:1] * lp[1:2], axis=-1, keepdims=True)
    s2 = jnp.sum(lp[2:3] * lp[3:4], axis=-1, keepdims=True)
    return jnp.exp(s1) - jnp.exp(s2) + lam_init


def _subln(o, sg, lam_init):
    return o * lax.rsqrt(jnp.mean(o * o, axis=-1, keepdims=True) + SUBLN_EPS) * sg * (1.0 - lam_init)


def _flash_kernel(q_ref, k_ref, v_ref, lam_ref, sg_ref, o_ref, s_sc, p_sc, m_sc, l_sc, acc_sc,
                  *, tq, tk, nh, rb, lam_init):
    qi = pl.program_id(2)
    rows = 2 * tq
    nfull = (qi * tq) // tk
    heads = range(nh)
    hcol = [slice(h * LANES, (h + 1) * LANES) for h in heads]
    lane = lax.broadcasted_iota(jnp.int32, (tq, LANES), 1)

    def stacked(q):
        zero = jnp.zeros_like(q)
        return jnp.concatenate([jnp.where(lane < ATTN_HEAD, q, zero), jnp.where(lane < ATTN_HEAD, zero, q)], axis=0)

    qs = [stacked(q_ref[0, :, hcol[h]]) for h in heads]

    def kv(ref, i, h):
        return ref[0, pl.ds(pl.multiple_of(i * tk, tk), tk), hcol[h]]

    def fold(parts, op):
        while len(parts) > 1:
            parts = [op(parts[i], parts[i + 1]) for i in range(0, len(parts), 2)]
        return parts[0]

    blocks = [slice(r0, r0 + rb) for r0 in range(0, rows, rb)]

    def softmax_rows(h, rs, s, pv):
        cols = [s[:, c:c + LANES] for c in range(0, tk, LANES)]
        m_old = m_sc[h, rs]
        m_new = jnp.maximum(m_old, jnp.max(fold(cols, jnp.maximum), axis=-1, keepdims=True))
        alpha = jnp.exp2(m_old - m_new)
        p = [jnp.exp2(z - m_new) for z in cols]
        l_sc[h, rs] = alpha * l_sc[h, rs] + fold(p, jnp.add)
        acc_sc[h, rs] = alpha * (acc_sc[h, rs] + pv)
        m_sc[h, rs] = m_new
        return jnp.concatenate(p, axis=-1).astype(BF16)

    m_sc[...] = jnp.full(m_sc.shape, _NEG, F32)
    l_sc[...] = jnp.zeros(l_sc.shape, F32)
    acc_sc[...] = jnp.zeros(acc_sc.shape, F32)
    for h in heads:
        p_sc[2 * h + 1] = jnp.zeros(p_sc.shape[1:], BF16)
    for h in heads:
        s_sc[2 * h] = _dot_nt(qs[h], kv(k_ref, 0, h))

    def stage(cur, k):
        for h in heads:
            k_next = kv(k_ref, k + 1, h)
            v_prev = kv(v_ref, jnp.maximum(k - 1, 0), h)
            for rs in blocks:
                s_sc[2 * h + 1 - cur, rs] = _dot_nt(qs[h][rs], k_next)
                pv = _dot(p_sc[2 * h + 1 - cur, rs], v_prev)
                p_sc[2 * h + cur, rs] = softmax_rows(h, rs, s_sc[2 * h + cur, rs], pv)

    def body4(j, carry):
        for i in range(4):
            stage(i % 2, 4 * j + i)
        return carry

    n4 = nfull // 4
    lax.fori_loop(0, n4, body4, 0)

    def body2(j, carry):
        stage(0, 4 * n4 + 2 * j)
        stage(1, 4 * n4 + 2 * j + 1)
        return carry

    lax.fori_loop(0, (nfull - 4 * n4) // 2, body2, 0)

    def finish(cur):
        lam = _lambda(lam_ref, lam_init)
        kpos = lax.broadcasted_iota(jnp.int32, (rb, tk), 1) + nfull * tk
        for h in heads:
            v_prev = kv(v_ref, jnp.maximum(nfull - 1, 0), h)
            v_last = kv(v_ref, nfull, h)
            for r0 in range(0, tq, rb):
                qpos = lax.broadcasted_iota(jnp.int32, (rb, tk), 0) + (r0 + qi * tq)
                on = []
                for rs in (slice(r0, r0 + rb), slice(tq + r0, tq + r0 + rb)):
                    pv = _dot(p_sc[2 * h + 1 - cur, rs], v_prev)
                    p = softmax_rows(h, rs, jnp.where(kpos <= qpos, s_sc[2 * h + cur, rs], _NEG), pv)
                    on.append((acc_sc[h, rs] + _dot(p, v_last)) / jnp.sum(l_sc[h, rs], axis=-1, keepdims=True))
                o = on[0] - lam * on[1]
                o_ref[0, r0:r0 + rb, hcol[h]] = _subln(o, sg_ref[...], lam_init).astype(o_ref.dtype)

    @pl.when(nfull % 2 == 0)
    def _():
        finish(0)

    @pl.when(nfull % 2 == 1)
    def _():
        stage(0, nfull - 1)
        finish(1)


def _flash(q, k, v, lam_p, sg, lam_init, tq, tk, nh, rb):
    B, T, _ = q.shape
    rows = 2 * tq
    width = nh * LANES
    return pl.pallas_call(
        functools.partial(_flash_kernel, tq=tq, tk=tk, nh=nh, rb=rb, lam_init=lam_init),
        grid=(B, ATTN_HEADS // nh, T // tq),
        in_specs=[
            pl.BlockSpec((1, tq, width), lambda b, h, i: (b, i, h)),
            pl.BlockSpec((1, T, width), lambda b, h, i: (b, 0, h)),
            pl.BlockSpec((1, T, width), lambda b, h, i: (b, 0, h)),
            _resident((4, ATTN_HEAD)), _resident((1, LANES)),
        ],
        out_specs=pl.BlockSpec((1, tq, width), lambda b, h, i: (b, i, h)),
        out_shape=jax.ShapeDtypeStruct((B, T, D_MODEL), BF16),
        scratch_shapes=[pltpu.VMEM((2 * nh, rows, tk), F32), pltpu.VMEM((2 * nh, rows, tk), BF16),
                        pltpu.VMEM((nh, rows, LANES), F32), pltpu.VMEM((nh, rows, LANES), F32),
                        pltpu.VMEM((nh, rows, LANES), F32)],
        compiler_params=_params(("arbitrary", "arbitrary", "arbitrary")),
        name="diff_flash",
    )(q, k, v, lam_p, sg)


def _dec_attn_kernel(*refs, lam_init, n_pg):
    pt_ref, q_ref = refs[0], refs[1]
    k_refs = refs[2:2 + n_pg]
    v_refs = refs[2 + n_pg:2 + 2 * n_pg]
    kn_ref, vn_ref, lam_ref, sg_ref, o_ref, m_sc, l_sc, acc_sc = refs[2 + 2 * n_pg:]
    del pt_ref
    j = pl.program_id(1)
    H = ATTN_HEADS
    G = 2 * H

    @pl.when(j == 0)
    def _():
        m_sc[...] = jnp.full(m_sc.shape, _NEG, F32)
        l_sc[...] = jnp.zeros(l_sc.shape, F32)
        acc_sc[...] = jnp.zeros(acc_sc.shape, F32)

    q8 = q_ref[0]
    lane = lax.broadcasted_iota(jnp.int32, (H, LANES), 1)
    zero = jnp.zeros_like(q8)
    q16 = jnp.concatenate([jnp.where(lane < ATTN_HEAD, q8, zero), jnp.where(lane < ATTN_HEAD, zero, q8)], axis=0)

    n = PAGE_SIZE * H
    same_head = (lax.broadcasted_iota(jnp.int32, (G, n), 0) % H) == (lax.broadcasted_iota(jnp.int32, (G, n), 1) % H)
    s = [jnp.where(same_head, _dot_nt(q16, kr[...].reshape(n, LANES).astype(BF16)), _NEG) for kr in k_refs]
    m_old = m_sc[...]
    m_new = m_old
    for z in s:
        m_new = jnp.maximum(m_new, jnp.max(z, axis=-1, keepdims=True))
    alpha = jnp.exp2(m_old - m_new)
    p = [jnp.exp2(z - m_new) for z in s]
    l_sc[...] = alpha * l_sc[...] + sum(jnp.sum(z, axis=-1, keepdims=True) for z in p)
    pv = sum(_dot(z.astype(BF16), vr[...].reshape(n, LANES).astype(BF16)) for z, vr in zip(p, v_refs))
    acc_sc[...] = alpha * acc_sc[...] + pv
    m_sc[...] = m_new

    @pl.when(j == pl.num_programs(1) - 1)
    def _():
        kn = kn_ref[0]
        vn = vn_ref[0]
        kn16 = jnp.concatenate([kn, kn], axis=0)
        vn16 = jnp.concatenate([vn, vn], axis=0)
        sn = jnp.sum(q16.astype(F32) * kn16, axis=-1, keepdims=True)
        m0 = m_sc[...]
        m1 = jnp.maximum(m0, sn)
        a0 = jnp.exp2(m0 - m1)
        pn = jnp.exp2(sn - m1)
        l1 = a0 * l_sc[...] + pn
        acc1 = a0 * acc_sc[...] + pn * vn16
        on = acc1 / l1
        lam = _lambda(lam_ref, lam_init)
        o = on[:H] - lam * on[H:]
        o_ref[0] = _subln(o, sg_ref[...], lam_init).astype(o_ref.dtype)


def _dec_attn(page_table, q, cache_k, cache_v, kn, vn, lam_p, sg, lam_init, layer):
    B, n_pages = page_table.shape
    H = ATTN_HEADS
    n_pg = math.gcd(DEC_PAGES, n_pages)
    head = lambda: pl.BlockSpec((1, H, LANES), lambda b, j, pt: (b, 0, 0))
    page = lambda i: pl.BlockSpec((None, None, PAGE_SIZE, H, LANES),
                                  lambda b, j, pt: (layer, pt[b, j * n_pg + i], 0, 0, 0))
    pages = [page(i) for i in range(n_pg)]
    grid_spec = pltpu.PrefetchScalarGridSpec(
        num_scalar_prefetch=1,
        grid=(B, n_pages // n_pg),
        in_specs=[head()] + pages + pages + [head(), head(), _resident((4, ATTN_HEAD)), _resident((1, LANES))],
        out_specs=pl.BlockSpec((1, H, LANES), lambda b, j, pt: (b, 0, 0)),
        scratch_shapes=[pltpu.VMEM((2 * H, 1), F32), pltpu.VMEM((2 * H, 1), F32), pltpu.VMEM((2 * H, LANES), F32)],
    )
    return pl.pallas_call(
        functools.partial(_dec_attn_kernel, lam_init=lam_init, n_pg=n_pg),
        grid_spec=grid_spec,
        out_shape=jax.ShapeDtypeStruct((B, H, LANES), BF16),
        compiler_params=_params(("arbitrary", "arbitrary")),
        name="paged_diff_attn",
    )(page_table, q, *([cache_k] * n_pg), *([cache_v] * n_pg), kn, vn, lam_p, sg)


_V_MU, _V_W0, _V_A0, _V_V0, _V_KK, _V_KA, _V_G = 0, 6, 7, 8, 9, 10, 11
_N_VEC = 16


def _rwkv_proj_kernel(*refs, shifted, has_vfirst):
    it = iter(refs)
    x_ref = next(it)
    prev_ref = next(it)
    shift_ref = next(it) if shifted else None
    vf_ref = next(it) if has_vfirst else None
    vec_ref, w_ref, w1_ref, w2_ref, a1_ref, a2_ref = (next(it) for _ in range(6))
    v1_ref, v2_ref = (next(it), next(it)) if has_vfirst else (None, None)
    g1_ref, g2_ref = next(it), next(it)
    r_ref, lw_ref, k_ref, v_ref, a_ref, b_ref, g_ref, mq_ref, xs_ref = (next(it) for _ in range(9))

    vec = vec_ref[...]
    row = lambda i: vec[i:i + 1]
    gain = row(_V_G)
    xn = _rms(x_ref[0], gain)
    tm = xn.shape[0]
    if shifted:
        pr = _rms(prev_ref[0][7:8], gain)
        pr = jnp.where(pl.program_id(1) == 0, shift_ref[0], pr)
        ridx = lax.broadcasted_iota(jnp.int32, (tm, 1), 0)
        xprev = jnp.where(ridx == 0, pr, pltpu.roll(xn, 1, 0))
        xs_ref[0] = xn[tm - 1:tm]
    else:
        xprev = prev_ref[0]
        xs_ref[0] = xn
    xx = xprev - xn
    mix = lambda j: (xn + xx * row(_V_MU + j)).astype(BF16)

    D = D_MODEL
    r_ref[0] = _dot(mix(0), w_ref[:, :D])
    k = _dot(mix(2), w_ref[:, D:2 * D])
    seg = _seg_ones(RWKV_HEAD)
    kk = k * row(_V_KK)
    sq = kk * kk
    n2 = jnp.concatenate(
        [sum(_dot(p_, seg) for p_ in _split2(sq[:, c:c + LANES])) for c in range(0, D, LANES)], axis=-1)
    kk = kk / jnp.maximum(jnp.sqrt(n2), 1e-12)
    a_ref[0] = -kk
    a = _sigmoid(row(_V_A0) + _dot(_dot(mix(4), a1_ref[...]).astype(BF16), a2_ref[...]))
    b_ref[0] = kk * a
    k_ref[0] = k * (1.0 + (a - 1.0) * row(_V_KA))
    xv = mix(3)
    v = _dot(xv, w_ref[:, 2 * D:3 * D])
    if has_vfirst:
        gate = _sigmoid(row(_V_V0) + _dot(_dot(xv, v1_ref[...]).astype(BF16), v2_ref[...]))
        v = v + (vf_ref[0] - v) * gate
    v_ref[0] = v
    w_in = row(_V_W0) + _dot(jnp.tanh(_dot(mix(1), w1_ref[...])).astype(BF16), w2_ref[...])
    w_log = -_softplus(-w_in) - 0.5
    lw_ref[0] = -jnp.exp(w_log)
    g_ref[0] = _dot(_sigmoid(_dot(mix(5), g1_ref[...])).astype(BF16), g2_ref[...])
    mq_ref[0] = (_dot(xn.astype(BF16), w_ref[:, 3 * D:]) * (MEM_HEAD ** -0.5)).astype(BF16)


def _rwkv_proj(x, prev, shift, vfirst, vec, w, loras, tm, layer):
    B, T, _ = x.shape
    shifted = shift is not None
    has_vfirst = vfirst is not None
    row = lambda w_: pl.BlockSpec((1, tm, w_), lambda b, i: (b, i, 0))
    ins, specs = [x], [row(D_MODEL)]
    if shifted:
        ins += [prev, shift]
        specs += [pl.BlockSpec((1, 8, D_MODEL), lambda b, i: (b, jnp.maximum(i * (tm // 8) - 1, 0), 0)),
                  pl.BlockSpec((1, 1, D_MODEL), lambda b, i: (b, 0, 0))]
    else:
        ins += [prev]
        specs += [row(D_MODEL)]
    if has_vfirst:
        ins.append(vfirst)
        specs.append(row(D_MODEL))
    w1, w2, a1, a2, v1, v2, g1, g2 = loras
    small = [w1, w2, a1, a2] + ([v1, v2] if has_vfirst else []) + [g1, g2]
    ins += [vec, w] + small
    specs += [_resident(vec.shape), _resident(w.shape[1:], layer)] + [_resident(s.shape) for s in small]
    sd = lambda dt, w_=D_MODEL, t_=T: jax.ShapeDtypeStruct((B, t_, w_), dt)
    xs_rows = 1 if shifted else T
    xs_spec = (pl.BlockSpec((1, 1, D_MODEL), lambda b, i: (b, 0, 0)) if shifted else row(D_MODEL))
    return pl.pallas_call(
        functools.partial(_rwkv_proj_kernel, shifted=shifted, has_vfirst=has_vfirst),
        grid=(B, T // tm),
        in_specs=specs,
        out_specs=[row(D_MODEL)] * 7 + [row(MEM_WIDTH), xs_spec],
        out_shape=[sd(F32)] * 7 + [sd(BF16, MEM_WIDTH), sd(F32, D_MODEL, xs_rows)],
        compiler_params=_params(("arbitrary", "arbitrary")),
        name="rwkv_proj",
    )(*ins)


def _wkv_kernel(r_ref, lw_ref, k_ref, v_ref, a_ref, b_ref, g_ref, vec_ref, s0_ref, y_ref, so_ref, s_sc,
                *, L, NB, P, passes):
    c = pl.program_id(2)
    sel = [(bb, p, slice(p * LANES, (p + 1) * LANES)) for bb in range(NB) for p in range(P)]

    @pl.when(c == 0)
    def _():
        for n, (bb, p, _) in enumerate(sel):
            s_sc[n] = s0_ref[bb, p]

    L2 = 2 * L
    tri = jnp.where(lax.broadcasted_iota(jnp.int32, (L, L), 0) >= lax.broadcasted_iota(jnp.int32, (L, L), 1),
                    1.0, 0.0).astype(BF16)
    r2 = lax.broadcasted_iota(jnp.int32, (L2, L2), 0)
    c2 = lax.broadcasted_iota(jnp.int32, (L2, L2), 1)
    strict = (r2 % L) > (c2 % L)
    incl = (r2 % L) >= (c2 % L)
    eye = jnp.where(r2 == c2, 1.0, 0.0)
    m0 = lax.broadcasted_iota(jnp.int32, (L, LANES), 1) < RWKV_HEAD
    seg = _seg_ones(RWKV_HEAD)
    mm = functools.partial(_mm, passes=passes)

    def stack(z):
        return jnp.concatenate([jnp.where(m0, z, 0.0), jnp.where(m0, 0.0, z)], axis=0)

    def segsum(z):
        hi, lo = _split2(z)
        return _dot(hi, seg) + _dot(lo, seg)

    pairs = range(len(sel))
    r = [r_ref[bb, :, sl] for bb, _, sl in sel]
    lw = [lw_ref[bb, :, sl] for bb, _, sl in sel]
    k = [k_ref[bb, :, sl] for bb, _, sl in sel]
    v = [v_ref[bb, :, sl] for bb, _, sl in sel]
    a = [a_ref[bb, :, sl] for bb, _, sl in sel]
    b = [b_ref[bb, :, sl] for bb, _, sl in sel]
    cum = [_dot_exact_lhs(tri, z) for z in lw]
    c_end = [z[L - 1:L, :] for z in cum]
    e_neg = [jnp.exp(-z) for z in cum]
    at_s = [stack(a[p] * jnp.exp(cum[p] - lw[p])) for p in pairs]
    rt_s = [stack(r[p] * jnp.exp(cum[p])) for p in pairs]
    bt_s = [stack(b[p] * e_neg[p]) for p in pairs]
    kt_s = [stack(k[p] * e_neg[p]) for p in pairs]
    v_s = [stack(z) for z in v]
    S = [s_sc[p] for p in pairs]

    n_ab = [jnp.where(strict, mm(at_s[p], bt_s[p], "nt"), 0.0) for p in pairs]
    a_ak = [jnp.where(strict, mm(at_s[p], kt_s[p], "nt"), 0.0) for p in pairs]
    a_rb = [jnp.where(incl, mm(rt_s[p], bt_s[p], "nt"), 0.0) for p in pairs]
    a_rk = [jnp.where(incl, mm(rt_s[p], kt_s[p], "nt"), 0.0) for p in pairs]

    t_inv = [eye + z for z in n_ab]
    pw = n_ab
    for _ in range(int(math.log2(L)) - 1):
        pw = [mm(z, z, "nn") for z in pw]
        t_inv = [t_inv[p] + mm(t_inv[p], pw[p], "nn") for p in pairs]

    x_s = [mm(at_s[p], S[p], "nt") + mm(a_ak[p], v_s[p], "nn") for p in pairs]
    u_s = [mm(t_inv[p], x_s[p], "nn") for p in pairs]
    y_s = [mm(rt_s[p], S[p], "nt") + mm(a_rb[p], u_s[p], "nn") + mm(a_rk[p], v_s[p], "nn") for p in pairs]
    for p in pairs:
        e_end = jnp.exp(c_end[p] - cum[p])
        s_sc[p] = (S[p] * jnp.exp(c_end[p]) + mm(u_s[p], stack(b[p] * e_end), "tn")
                   + mm(v_s[p], stack(k[p] * e_end), "tn"))

    y = [z[:L] + z[L:] for z in y_s]
    yc = [z - segsum(z) * (1.0 / RWKV_HEAD) for z in y]
    var = [segsum(z * z) * (1.0 / RWKV_HEAD) for z in yc]
    for p, (bb, _, sl) in enumerate(sel):
        vec = vec_ref[:, sl]
        yn = yc[p] * lax.rsqrt(var[p] + GN_EPS) * vec[0:1] + vec[1:2]
        bonus = segsum(r[p] * k[p] * vec[2:3]) * v[p]
        y_ref[bb, :, sl] = ((yn + bonus) * g_ref[bb, :, sl]).astype(y_ref.dtype)

    @pl.when(c == pl.num_programs(2) - 1)
    def _():
        for n, (bb, p, _) in enumerate(sel):
            so_ref[bb, p] = s_sc[n]


def _wkv(r, lw, k, v, a, b, g, vec, s0, L, NB, P, passes):
    B, T, _ = r.shape
    npair = D_MODEL // LANES
    width = P * LANES
    tile = lambda: pl.BlockSpec((NB, L, width), lambda bb, pg, c: (bb, c, pg))
    state = lambda: pl.BlockSpec((NB, P, LANES, LANES), lambda bb, pg, c: (bb, pg, 0, 0))
    return pl.pallas_call(
        functools.partial(_wkv_kernel, L=L, NB=NB, P=P, passes=passes),
        grid=(B // NB, npair // P, T // L),
        in_specs=[tile() for _ in range(7)] + [pl.BlockSpec((8, width), lambda bb, pg, c: (0, pg)), state()],
        out_specs=[tile(), state()],
        out_shape=[jax.ShapeDtypeStruct((B, T, D_MODEL), BF16),
                   jax.ShapeDtypeStruct((B, npair, LANES, LANES), F32)],
        scratch_shapes=[pltpu.VMEM((NB * P, LANES, LANES), F32)],
        compiler_params=_params(("arbitrary", "arbitrary", "arbitrary")),
        name="wkv_chunked",
    )(r, lw, k, v, a, b, g, vec, s0)


def _wkv_step_kernel(r_ref, lw_ref, k_ref, v_ref, a_ref, b_ref, g_ref, vec_ref, s_ref, y_ref, so_ref, *, nb):
    hd = RWKV_HEAD
    npair = D_MODEL // LANES
    rid = lax.broadcasted_iota(jnp.int32, (8, LANES), 0)
    lid = lax.broadcasted_iota(jnp.int32, (8, LANES), 1)
    own = ((rid == 0) & (lid < hd)) | ((rid == 1) & (lid >= hd))
    seg = _seg_ones(hd)
    zpad = jnp.zeros((hd, hd), F32)
    cols = [slice(p * LANES, (p + 1) * LANES) for p in range(npair)]
    combos = [(i, p) for i in range(nb) for p in range(npair)]

    def vec_row(ref, i, p):
        return ref[i:i + 1, cols[p]]

    def at_row(z, i):
        return jnp.where(rid == i, z, 0.0).astype(BF16)

    def split_heads(z):
        return jnp.where(own, z, 0.0)

    S = [jnp.concatenate([jnp.concatenate([s_ref[i, 2 * p], zpad], axis=1),
                          jnp.concatenate([zpad, s_ref[i, 2 * p + 1]], axis=1)], axis=0) for i, p in combos]
    Sb = [z.astype(BF16) for z in S]
    u = [_dot_nt(at_row(vec_row(a_ref, i, p), 0), Sb[n])[0:1] for n, (i, p) in enumerate(combos)]
    lhs = [jnp.concatenate([split_heads(u[n]), split_heads(vec_row(v_ref, i, p))], axis=0).astype(BF16)
           for n, (i, p) in enumerate(combos)]
    rhs = [jnp.concatenate([split_heads(vec_row(b_ref, i, p)), split_heads(vec_row(k_ref, i, p))], axis=0).astype(BF16)
           for i, p in combos]
    S = [S[n] * jnp.exp(vec_row(lw_ref, i, p)) + _dot_tn(lhs[n], rhs[n]) for n, (i, p) in enumerate(combos)]
    for n, (i, p) in enumerate(combos):
        so_ref[i, 2 * p] = S[n][:hd, :hd]
        so_ref[i, 2 * p + 1] = S[n][hd:, hd:]
    yrow = [_dot_nt(at_row(vec_row(r_ref, i, p), i), S[n].astype(BF16)) for n, (i, p) in enumerate(combos)]

    def segsum(z):
        hi, lo = _split2(z)
        return _dot(hi, seg) + _dot(lo, seg)

    for p in range(npair):
        y = yrow[p]
        for i in range(1, nb):
            y = y + yrow[i * npair + p]
        yc = y - segsum(y) * (1.0 / hd)
        var = segsum(yc * yc) * (1.0 / hd)
        vec = vec_ref[:, cols[p]]
        yn = yc * lax.rsqrt(var + GN_EPS) * vec[0:1] + vec[1:2]
        bonus = segsum(r_ref[:, cols[p]] * k_ref[:, cols[p]] * vec[2:3]) * v_ref[:, cols[p]]
        y_ref[:, cols[p]] = (yn + bonus) * g_ref[:, cols[p]]


def _wkv_step(r, lw, k, v, a, b, g, vec, s0, nb=8):
    n_seq = r.shape[0]
    heads = D_MODEL // RWKV_HEAD
    tile = lambda: pl.BlockSpec((nb, D_MODEL), lambda i: (i, 0))
    state = lambda: pl.BlockSpec((nb, heads, RWKV_HEAD, RWKV_HEAD), lambda i: (i, 0, 0, 0))
    return pl.pallas_call(
        functools.partial(_wkv_step_kernel, nb=nb),
        grid=(n_seq // nb,),
        in_specs=[tile() for _ in range(7)] + [_resident((8, D_MODEL)), state()],
        out_specs=[tile(), state()],
        out_shape=[jax.ShapeDtypeStruct((n_seq, D_MODEL), F32),
                   jax.ShapeDtypeStruct((n_seq, heads, RWKV_HEAD, RWKV_HEAD), F32)],
        compiler_params=_params(("arbitrary",)),
        name="wkv_step",
    )(r, lw, k, v, a, b, g, vec, s0)


def _state_to_pairs(s):
    B = s.shape[0]
    s = s.reshape(B, -1, 2, RWKV_HEAD, RWKV_HEAD)
    z = jnp.zeros_like(s[:, :, 0])
    top = jnp.concatenate([s[:, :, 0], z], axis=-1)
    bot = jnp.concatenate([z, s[:, :, 1]], axis=-1)
    return jnp.concatenate([top, bot], axis=-2)


def _state_from_pairs(sp):
    B = sp.shape[0]
    h = RWKV_HEAD
    return jnp.stack([sp[:, :, :h, :h], sp[:, :, h:, h:]], axis=2).reshape(B, -1, h, h)


def _rope_tables(pos):
    half = ATTN_HEAD // 2
    inv = jnp.power(ROPE_THETA, -jnp.arange(half, dtype=F32) * 2.0 / ATTN_HEAD)
    ang = pos.astype(F32)[:, None] * inv[None, :]
    cos = jnp.cos(ang)
    sin = jnp.sin(ang)
    reps = LANES // ATTN_HEAD
    return jnp.tile(cos, (1, 2 * reps)), jnp.tile(jnp.concatenate([-sin, sin], axis=1), (1, reps))


def _pad_cols(w):
    return jnp.pad(w, ((0, 0), (0, LORA_PAD - w.shape[1])))


def _pad_rows(w):
    return jnp.pad(w, ((0, LORA_PAD - w.shape[0]), (0, 0)))


def _pad_tokens(z, t):
    return jnp.pad(z, ((0, 0), (0, t - z.shape[1]), (0, 0)))


def _trunk(x, pos, decode, shift0, wkv0, mem_k, mem_v, past, W):
    B, T, _ = x.shape
    if decode:
        xf = x.reshape(1, B, D_MODEL)
        tm = B
    else:
        xf = x
        tm = 512
    cos_t, sin_t = _rope_tables(pos if not decode else jnp.broadcast_to(pos, (B,)))
    shifts, states, ks, vs = [], [], [], []
    v_first = None
    for l in range(DEPTH):
        idx = l // 2
        if l % 2 == 0:
            vec = jnp.concatenate([
                W["mu"][idx], W["w0"][idx][None], W["a0"][idx][None],
                (W["v0"][idx - 1] if idx > 0 else jnp.zeros((D_MODEL,), F32))[None],
                W["k_k"][idx][None], W["k_a"][idx][None], W["mix_g"][l][None],
                jnp.zeros((_N_VEC - 12, D_MODEL), F32)], axis=0)
            loras = (W["w1"][idx], W["w2"][idx], W["a1"][idx], W["a2"][idx],
                     W["v1"][idx - 1] if idx > 0 else None, W["v2"][idx - 1] if idx > 0 else None,
                     W["g1"][idx], W["g2"][idx])
            if decode:
                outs = _rwkv_proj(xf, shift0[idx][None], None, v_first, vec, W["w_in"], loras, tm, l)
            else:
                outs = _rwkv_proj(xf, xf, shift0[idx][:, None], v_first, vec, W["w_in"], loras, tm // 2, l)
            r, lw, k, v, a, b, g, mq, xs = outs
            if idx == 0:
                v_first = v
            shifts.append(xs.reshape(B, D_MODEL))
            scan_in = [r, lw, k, v, a, b, g]
            vec2 = jnp.concatenate([W["lnx_g"][idx][None], W["lnx_b"][idx][None], W["r_k"][idx][None],
                                    jnp.zeros((5, D_MODEL), F32)], axis=0)
            if decode:
                y, s_new = _wkv_step(*[z.reshape(B, D_MODEL) for z in scan_in], vec2, wkv0[idx])
                states.append(s_new)
                y_tok = y.astype(BF16).reshape(1, B, D_MODEL)
            else:
                y_tok, s_new = _wkv(*scan_in, vec2, _state_to_pairs(wkv0[idx]), WKV_CHUNK, B, WKV_PAIRS, 1)
                states.append(_state_from_pairs(s_new))
        else:
            lam_init = 0.8 - 0.6 * math.exp(-0.3 * l)
            q, kf, kb, vf, vb, mq = _diff_proj(xf, W["mix_g"][l][None], W["w_in"], cos_t, sin_t, tm, l)
            lam_p = jnp.stack([W["lam_q1"][idx], W["lam_k1"][idx], W["lam_q2"][idx], W["lam_k2"][idx]])
            sg = W["subln_g"][idx][None]
            if decode:
                cache_k, cache_v, page_table = past
                hd = lambda z: z.reshape(B, ATTN_HEADS, LANES)
                o = _dec_attn(page_table, hd(q), cache_k, cache_v, hd(kf), hd(vf), lam_p, sg, lam_init, idx)
                y_tok = o.reshape(1, B, D_MODEL)
            else:
                y_tok = _flash(q, kb, vb, lam_p, sg, lam_init, 512, 512, 1, 256)
            ks.append(kf.reshape(B, T, ATTN_HEADS, LANES))
            vs.append(vf.reshape(B, T, ATTN_HEADS, LANES))
        if decode:
            mq8 = _pad_tokens(mq.reshape(B, 1, MEM_WIDTH), 8)
            y_mem = _mem_attend(mq8, mem_k, mem_v, 8, math.gcd(B, 8), l)[:, :1].reshape(1, B, MEM_WIDTH)
        else:
            y_mem = _mem_attend(mq, mem_k, mem_v, 2 * tm, 1, l)
        xf = _out_ffn(xf, y_tok, y_mem, W["w_out"], W["ffn_g"][l][None], W["w_gate"], W["w_up"],
                      W["w_down"], W["final_g"][None], tm, l == DEPTH - 1, l)
    return xf.reshape(B, T, D_MODEL), jnp.stack(shifts), jnp.stack(states), jnp.stack(ks), jnp.stack(vs)


def kernel(x_prompt, x_sample, cache_k, cache_v, cache_mem_k, cache_mem_v, state_rwkv_wkv, state_rwkv_shift, page_table, mem_prompt, w_in, w_out, mix_norm_g, ffn_norm_g, w_gate, w_up, w_down, final_norm_g, mem_norm_g, w_mem_k, w_mem_v, rwkv_mu, rwkv_w0, rwkv_w1, rwkv_w2, rwkv_a0, rwkv_a1, rwkv_a2, rwkv_v0, rwkv_v1, rwkv_v2, rwkv_g1, rwkv_g2, rwkv_k_k, rwkv_k_a, rwkv_r_k, rwkv_lnx_g, rwkv_lnx_b, diff_lam_q1, diff_lam_k1, diff_lam_q2, diff_lam_k2, diff_subln_g):
    bf = lambda z: z.astype(BF16)
    n_rwkv = rwkv_mu.shape[0]
    W = dict(
        w_in=bf(w_in), w_out=bf(w_out), mix_g=mix_norm_g, ffn_g=ffn_norm_g,
        w_gate=bf(w_gate), w_up=bf(w_up), w_down=bf(w_down), final_g=final_norm_g,
        mu=rwkv_mu, w0=rwkv_w0, a0=rwkv_a0, v0=rwkv_v0, k_k=rwkv_k_k, k_a=rwkv_k_a,
        w1=[bf(_pad_cols(rwkv_w1[i])) for i in range(n_rwkv)],
        w2=[bf(_pad_rows(rwkv_w2[i])) for i in range(n_rwkv)],
        a1=[bf(_pad_cols(rwkv_a1[i])) for i in range(n_rwkv)],
        a2=[bf(_pad_rows(rwkv_a2[i])) for i in range(n_rwkv)],
        v1=[bf(_pad_cols(rwkv_v1[i])) for i in range(n_rwkv - 1)],
        v2=[bf(_pad_rows(rwkv_v2[i])) for i in range(n_rwkv - 1)],
        g1=bf(rwkv_g1), g2=bf(rwkv_g2),
        r_k=rwkv_r_k.reshape(n_rwkv, D_MODEL), lnx_g=rwkv_lnx_g, lnx_b=rwkv_lnx_b,
        lam_q1=diff_lam_q1, lam_k1=diff_lam_k1, lam_q2=diff_lam_q2, lam_k2=diff_lam_k2,
        subln_g=diff_subln_g,
    )

    B, T, _ = x_prompt.shape
    M = mem_prompt.shape[1]
    mk, mv = _mem_kv(mem_prompt.reshape(B * M, D_MODEL), mem_norm_g[:, None], bf(w_mem_k), bf(w_mem_v))
    mk = mk.reshape(DEPTH, B, M * MEM_HEADS, MEM_HEAD)
    mv = mv.reshape(DEPTH, B, M * MEM_HEADS, MEM_HEAD)
    p_mem_k = mk.reshape(DEPTH, B, M, MEM_HEADS, MEM_HEAD)
    p_mem_v = mv.reshape(DEPTH, B, M, MEM_HEADS, MEM_HEAD)

    pos_p = jnp.arange(T, dtype=jnp.int32)
    shift0 = jnp.zeros((n_rwkv, B, D_MODEL), F32)
    wkv0 = jnp.zeros((n_rwkv, B, D_MODEL // RWKV_HEAD, RWKV_HEAD, RWKV_HEAD), F32)
    y_prompt, p_shift, p_wkv, p_k, p_v = _trunk(x_prompt, pos_p, False, shift0, wkv0, mk, mv, None, W)

    Bs = x_sample.shape[0]
    past_len = page_table.shape[1] * PAGE_SIZE
    pos_s = past_len + jnp.arange(x_sample.shape[1], dtype=jnp.int32)
    smk = cache_mem_k.reshape(DEPTH, Bs, -1, MEM_HEAD)
    smv = cache_mem_v.reshape(DEPTH, Bs, -1, MEM_HEAD)
    y_sample, s_shift, s_wkv, s_k, s_v = _trunk(
        x_sample, pos_s, True, state_rwkv_shift, state_rwkv_wkv, smk, smv, (cache_k, cache_v, page_table), W)

    return (y_prompt, y_sample, p_wkv, p_shift, p_k, p_v, p_mem_k, p_mem_v, s_wkv, s_shift, s_k, s_v)
```

```python
import functools
import math

import jax
import jax.numpy as jnp
from jax import lax
from jax.experimental import pallas as pl
from jax.experimental.pallas import tpu as pltpu

F32 = jnp.float32
BF16 = jnp.bfloat16

D_MODEL = 1024
DEPTH = 4
PAGE_SIZE = 128
RWKV_HEAD = 64
ATTN_HEAD = 64
ATTN_HEADS = D_MODEL // (2 * ATTN_HEAD)
MEM_HEADS = 4
MEM_HEAD = 128
MEM_WIDTH = MEM_HEADS * MEM_HEAD
D_FF = 2816
ROPE_THETA = 10000.0
NORM_EPS = 1e-6
SUBLN_EPS = 1e-5
GN_EPS = 1e-5 * RWKV_HEAD

Q_SCALE = ATTN_HEAD ** -0.5 * math.log2(math.e)
LANES = 128
LORA_PAD = 128
WKV_CHUNK = 64
WKV_PAIRS = 8
DEC_PAGES = 8
VMEM_LIMIT = 56 * 1024 * 1024

_NT = (((1,), (1,)), ((), ()))
_TN = (((0,), (0,)), ((), ()))
_NEG = -1e30


def _dot(a, b):
    return jnp.dot(a, b, preferred_element_type=F32)


def _dot_nt(a, b):
    return lax.dot_general(a, b, _NT, preferred_element_type=F32)


def _dot_tn(a, b):
    return lax.dot_general(a, b, _TN, preferred_element_type=F32)


def _split2(x):
    hi = x.astype(BF16)
    lo = (x - hi.astype(F32)).astype(BF16)
    return hi, lo


def _split3(x):
    h1 = x.astype(BF16)
    r1 = x - h1.astype(F32)
    h2 = r1.astype(BF16)
    h3 = (r1 - h2.astype(F32)).astype(BF16)
    return h1, h2, h3


def _dot_exact_rhs(x, m_bf16):
    h1, h2, h3 = _split3(x)
    return _dot(h1, m_bf16) + _dot(h2, m_bf16) + _dot(h3, m_bf16)


def _dot_exact_lhs(m_bf16, x):
    h1, h2, h3 = _split3(x)
    return _dot(m_bf16, h1) + _dot(m_bf16, h2) + _dot(m_bf16, h3)


def _mm(a, b, kind, passes):
    f = {"nn": _dot, "nt": _dot_nt, "tn": _dot_tn}[kind]
    if passes == 1:
        return f(a.astype(BF16), b.astype(BF16))
    ah, al = _split2(a)
    bh, bl = _split2(b)
    return f(ah, bh) + f(ah, bl) + f(al, bh)


def _rms(x, g):
    ms = jnp.mean(x * x, axis=-1, keepdims=True)
    return x * lax.rsqrt(ms + NORM_EPS) * g


def _sigmoid(x):
    return 1.0 / (1.0 + jnp.exp(-x))


def _softplus(x):
    return jnp.maximum(x, 0.0) + jnp.log(1.0 + jnp.exp(-jnp.abs(x)))


def _seg_ones(width):
    r = lax.broadcasted_iota(jnp.int32, (LANES, LANES), 0) // width
    c = lax.broadcasted_iota(jnp.int32, (LANES, LANES), 1) // width
    return jnp.where(r == c, 1.0, 0.0).astype(BF16)


def _resident(shape, layer=None):
    nd = len(shape)
    if layer is None:
        return pl.BlockSpec(shape, lambda *_: (0,) * nd, pipeline_mode=pl.Buffered(1))
    return pl.BlockSpec((None,) + tuple(shape), lambda *_: (layer,) + (0,) * nd, pipeline_mode=pl.Buffered(1))


def _params(sem):
    return pltpu.CompilerParams(dimension_semantics=sem, vmem_limit_bytes=VMEM_LIMIT)


def _mem_kv_kernel(mem_ref, g_ref, wk_ref, wv_ref, k_ref, v_ref):
    mn = _rms(mem_ref[...], g_ref[0]).astype(BF16)
    rows = mn.shape[0]
    for w_ref, o_ref in ((wk_ref, k_ref), (wv_ref, v_ref)):
        z = _dot(mn, w_ref[0])
        for h in range(MEM_HEADS):
            o_ref[0, pl.ds(h, rows, stride=MEM_HEADS), :] = z[:, h * MEM_HEAD:(h + 1) * MEM_HEAD]


def _mem_kv(mem2d, g, wk, wv):
    rows = mem2d.shape[0]
    out = jax.ShapeDtypeStruct((DEPTH, rows * MEM_HEADS, MEM_HEAD), F32)
    return pl.pallas_call(
        _mem_kv_kernel,
        grid=(DEPTH,),
        in_specs=[
            pl.BlockSpec((rows, D_MODEL), lambda l: (0, 0)),
            pl.BlockSpec((1, 1, D_MODEL), lambda l: (l, 0, 0)),
            pl.BlockSpec((1, D_MODEL, MEM_WIDTH), lambda l: (l, 0, 0)),
            pl.BlockSpec((1, D_MODEL, MEM_WIDTH), lambda l: (l, 0, 0)),
        ],
        out_specs=[pl.BlockSpec((1, rows * MEM_HEADS, MEM_HEAD), lambda l: (l, 0, 0))] * 2,
        out_shape=[out, out],
        compiler_params=_params(("arbitrary",)),
        name="mem_kv",
    )(mem2d, g, wk, wv)


def _mem_attend_kernel(q_ref, k_ref, v_ref, o_ref):
    M = k_ref.shape[1] // MEM_HEADS
    nb = q_ref.shape[0]
    chains = [(i, h) for i in range(nb) for h in range(MEM_HEADS)]
    q = [q_ref[i, :, h * MEM_HEAD:(h + 1) * MEM_HEAD] for i, h in chains]
    k = [k_ref[i, pl.ds(h, M, stride=MEM_HEADS), :].astype(BF16) for i, h in chains]
    v = [v_ref[i, pl.ds(h, M, stride=MEM_HEADS), :].astype(BF16) for i, h in chains]
    s = [_dot_nt(a, b) for a, b in zip(q, k)]
    p = [jnp.exp(z - jnp.max(z, axis=-1, keepdims=True)) for z in s]
    o = [_dot(a.astype(BF16), b) / jnp.sum(a, axis=-1, keepdims=True) for a, b in zip(p, v)]
    for i in range(nb):
        o_ref[i] = jnp.concatenate(o[i * MEM_HEADS:(i + 1) * MEM_HEADS], axis=-1).astype(o_ref.dtype)


def _mem_attend(q, mk, mv, tq, nb, layer):
    B, T, _ = q.shape
    MH = mk.shape[2]
    return pl.pallas_call(
        _mem_attend_kernel,
        grid=(B // nb, T // tq),
        in_specs=[
            pl.BlockSpec((nb, tq, MEM_WIDTH), lambda b, i: (b, i, 0)),
            pl.BlockSpec((None, nb, MH, MEM_HEAD), lambda b, i: (layer, b, 0, 0)),
            pl.BlockSpec((None, nb, MH, MEM_HEAD), lambda b, i: (layer, b, 0, 0)),
        ],
        out_specs=pl.BlockSpec((nb, tq, MEM_WIDTH), lambda b, i: (b, i, 0)),
        out_shape=jax.ShapeDtypeStruct((B, T, MEM_WIDTH), BF16),
        compiler_params=_params(("arbitrary", "arbitrary")),
        name="mem_attend",
    )(q, mk, mv)


def _out_ffn_kernel(x_ref, yt_ref, ym_ref, wo_ref, g_ref, wg_ref, wu_ref, wd_ref, fg_ref, o_ref,
                    *, final, ft):
    x1 = x_ref[0] + (_dot(yt_ref[0], wo_ref[:D_MODEL, :]) + _dot(ym_ref[0], wo_ref[D_MODEL:, :]))
    h = _rms(x1, g_ref[...]).astype(BF16)
    acc = jnp.zeros_like(x1)
    for f in range(0, D_FF, ft):
        gt = _dot(h, wg_ref[:, f:f + ft])
        up = _dot(h, wu_ref[:, f:f + ft])
        act = (gt * _sigmoid(gt) * up).astype(BF16)
        acc = acc + _dot(act, wd_ref[f:f + ft, :])
    acc = x1 + acc
    if final:
        acc = _rms(acc, fg_ref[...])
    o_ref[0] = acc


def _out_ffn(x, ytok, ymem, wo, g, wg, wu, wd, fg, tm, final, layer):
    B, T, _ = x.shape
    row = lambda w: pl.BlockSpec((1, tm, w), lambda b, i: (b, i, 0))
    return pl.pallas_call(
        functools.partial(_out_ffn_kernel, final=final, ft=256),
        grid=(B, T // tm),
        in_specs=[
            row(D_MODEL), row(D_MODEL), row(MEM_WIDTH),
            _resident((D_MODEL + MEM_WIDTH, D_MODEL), layer),
            _resident((1, D_MODEL)),
            _resident((D_MODEL, D_FF), layer), _resident((D_MODEL, D_FF), layer), _resident((D_FF, D_MODEL), layer),
            _resident((1, D_MODEL)),
        ],
        out_specs=row(D_MODEL),
        out_shape=jax.ShapeDtypeStruct((B, T, D_MODEL), F32),
        compiler_params=_params(("arbitrary", "arbitrary")),
        name="out_ffn",
    )(x, ytok, ymem, wo, g, wg, wu, wd, fg)


def _diff_proj_kernel(x_ref, g_ref, w_ref, cos_ref, sin_ref,
                      q_ref, k_ref, kb_ref, v_ref, vb_ref, mq_ref):
    xn = _rms(x_ref[0], g_ref[...]).astype(BF16)
    tm = xn.shape[0]
    cos = cos_ref[...]
    sin = sin_ref[...]
    lane = lax.broadcasted_iota(jnp.int32, (tm, LANES), 1)
    first = (lane % ATTN_HEAD) < (ATTN_HEAD // 2)

    def rope(z):
        rot = jnp.where(first, pltpu.roll(z, LANES - ATTN_HEAD // 2, 1), pltpu.roll(z, ATTN_HEAD // 2, 1))
        return z * cos + rot * sin

    half = D_MODEL // 2
    for c in range(2):
        z = _dot(xn, w_ref[:, c * half:(c + 1) * half])
        for j in range(half // LANES):
            col = c * half + j * LANES
            q_ref[0, :, col:col + LANES] = (rope(z[:, j * LANES:(j + 1) * LANES]) * Q_SCALE).astype(BF16)
    for c in range(2):
        z = _dot(xn, w_ref[:, D_MODEL + c * half:D_MODEL + (c + 1) * half])
        for j in range(half // LANES):
            col = c * half + j * LANES
            kr = rope(z[:, j * LANES:(j + 1) * LANES])
            k_ref[0, pl.ds(col // LANES, tm, stride=ATTN_HEADS), :] = kr
            kb_ref[0, :, col:col + LANES] = kr.astype(BF16)
    for c in range(2):
        z = _dot(xn, w_ref[:, 2 * D_MODEL + c * half:2 * D_MODEL + (c + 1) * half])
        for j in range(half // LANES):
            head = (c * half) // LANES + j
            v_ref[0, pl.ds(head, tm, stride=ATTN_HEADS), :] = z[:, j * LANES:(j + 1) * LANES]
        vb_ref[0, :, c * half:(c + 1) * half] = z.astype(BF16)
    z = _dot(xn, w_ref[:, 3 * D_MODEL:])
    mq_ref[0] = (z * (MEM_HEAD ** -0.5)).astype(BF16)


def _diff_proj(x, g, w, cos_t, sin_t, tm, layer):
    B, T, _ = x.shape
    row = lambda w_: pl.BlockSpec((1, tm, w_), lambda b, i: (b, i, 0))
    sd = lambda w_, dt: jax.ShapeDtypeStruct((B, T, w_), dt)
    by_head = pl.BlockSpec((1, tm * ATTN_HEADS, LANES), lambda b, i: (b, i, 0))
    sd_head = jax.ShapeDtypeStruct((B, T * ATTN_HEADS, LANES), F32)
    return pl.pallas_call(
        _diff_proj_kernel,
        grid=(B, T // tm),
        in_specs=[
            row(D_MODEL), _resident((1, D_MODEL)), _resident((D_MODEL, 3 * D_MODEL + MEM_WIDTH), layer),
            pl.BlockSpec((tm, LANES), lambda b, i: (i, 0)),
            pl.BlockSpec((tm, LANES), lambda b, i: (i, 0)),
        ],
        out_specs=[row(D_MODEL), by_head, row(D_MODEL), by_head, row(D_MODEL), row(MEM_WIDTH)],
        out_shape=[sd(D_MODEL, BF16), sd_head, sd(D_MODEL, BF16), sd_head,
                   sd(D_MODEL, BF16), sd(MEM_WIDTH, BF16)],
        compiler_params=_params(("arbitrary", "arbitrary")),
        name="diff_proj",
    )(x, g, w, cos_t, sin_t)


def _lambda(lam_ref, lam_init):
    lp = lam_ref[...]
    s1 = jnp.sum(lp[0:1] * lp[1:2], axis=-1, keepdims=True)
    s2 = jnp.sum(lp[2:3] * lp[3:4], axis=-1, keepdims=True)
    return jnp.exp(s1) - jnp.exp(s2) + lam_init


def _subln(o, sg, lam_init):
    return o * lax.rsqrt(jnp.mean(o * o, axis=-1, keepdims=True) + SUBLN_EPS) * sg * (1.0 - lam_init)


def _flash_kernel(q_ref, k_ref, v_ref, lam_ref, sg_ref, o_ref, s_sc, p_sc, m_sc, l_sc, acc_sc,
                  *, tq, tk, nh, rb, lam_init):
    qi = pl.program_id(2)
    assert tq == tk
    rows = 2 * tq
    nfull = (qi * tq) // tk
    fb = LANES
    heads = range(nh)
    hcol = [slice(h * LANES, (h + 1) * LANES) for h in heads]
    lane = lax.broadcasted_iota(jnp.int32, (tq, LANES), 1)

    def stacked(q):
        zero = jnp.zeros_like(q)
        return jnp.concatenate([jnp.where(lane < ATTN_HEAD, q, zero), jnp.where(lane < ATTN_HEAD, zero, q)], axis=0)

    qs = [stacked(q_ref[0, :, hcol[h]]) for h in heads]

    def kv(ref, i, h):
        return ref[0, pl.ds(pl.multiple_of(i * tk, tk), tk), hcol[h]]

    def fold(parts, op):
        while len(parts) > 1:
            parts = [op(parts[i], parts[i + 1]) for i in range(0, len(parts) - 1, 2)] + parts[len(parts) & ~1:]
        return parts[0]

    blocks =[slice(r0, r0 + rb) for r0 in range(0, rows, rb)]

    def softmax_rows(h, rs, s, pv):
        cols = [s[:, c:c + LANES] for c in range(0, s.shape[1], LANES)]
        m_old = m_sc[h, rs]
        m_new = jnp.maximum(m_old, jnp.max(fold(cols, jnp.maximum), axis=-1, keepdims=True))
        alpha = jnp.exp2(m_old - m_new)
        p = [jnp.exp2(z - m_new) for z in cols]
        l_sc[h, rs] = alpha * l_sc[h, rs] + fold(p, jnp.add)
        acc_sc[h, rs] = alpha * (acc_sc[h, rs] + pv)
        m_sc[h, rs] = m_new
        return jnp.concatenate(p, axis=-1).astype(BF16)

    m_sc[...] = jnp.full(m_sc.shape, _NEG, F32)
    l_sc[...] = jnp.zeros(l_sc.shape, F32)
    acc_sc[...] = jnp.zeros(acc_sc.shape, F32)
    for h in heads:
        p_sc[2 * h + 1] = jnp.zeros(p_sc.shape[1:], BF16)
    for h in heads:
        s_sc[2 * h] = _dot_nt(qs[h], kv(k_ref, 0, h))

    def stage(cur, k):
        for h in heads:
            k_next = kv(k_ref, k + 1, h)
            v_prev = kv(v_ref, jnp.maximum(k - 1, 0), h)
            for rs in blocks:
                s_sc[2 * h + 1 - cur, rs] = _dot_nt(qs[h][rs], k_next)
                pv = _dot(p_sc[2 * h + 1 - cur, rs], v_prev)
                p_sc[2 * h + cur, rs] = softmax_rows(h, rs, s_sc[2 * h + cur, rs], pv)

    def body4(j, carry):
        for i in range(4):
            stage(i % 2, 4 * j + i)
        return carry

    n4 = nfull // 4
    lax.fori_loop(0, n4, body4, 0)

    def body2(j, carry):
        stage(0, 4 * n4 + 2 * j)
        stage(1, 4 * n4 + 2 * j + 1)
        return carry

    lax.fori_loop(0, (nfull - 4 * n4) // 2, body2, 0)

    def finish(cur):
        lam = _lambda(lam_ref, lam_init)
        for h in heads:
            v_prev = kv(v_ref, jnp.maximum(nfull - 1, 0), h)
            v_last = kv(v_ref, nfull, h)
            for r0 in range(0, tq, fb):
                cw = r0 + fb
                visible = (lax.broadcasted_iota(jnp.int32, (fb, cw), 1)
                           <= lax.broadcasted_iota(jnp.int32, (fb, cw), 0) + r0)
                on = []
                for rs in (slice(r0, r0 + fb), slice(tq + r0, tq + r0 + fb)):
                    pv = _dot(p_sc[2 * h + 1 - cur, rs], v_prev)
                    p = softmax_rows(h, rs, jnp.where(visible, s_sc[2 * h + cur, rs, :cw], _NEG), pv)
                    on.append((acc_sc[h, rs] + _dot(p, v_last[:cw])) / jnp.sum(l_sc[h, rs], axis=-1, keepdims=True))
                o = on[0] - lam * on[1]
                o_ref[0, r0:r0 + fb, hcol[h]] = _subln(o, sg_ref[...], lam_init).astype(o_ref.dtype)

    @pl.when(nfull % 2 == 0)
    def _():
        finish(0)

    @pl.when(nfull % 2 == 1)
    def _():
        stage(0, nfull - 1)
        finish(1)


def _flash(q, k, v, lam_p, sg, lam_init, tq, tk, nh, rb):
    B, T, _ = q.shape
    rows = 2 * tq
    width = nh * LANES
    return pl.pallas_call(
        functools.partial(_flash_kernel, tq=tq, tk=tk, nh=nh, rb=rb, lam_init=lam_init),
        grid=(B, ATTN_HEADS // nh, T // tq),
        in_specs=[
            pl.BlockSpec((1, tq, width), lambda b, h, i: (b, i, h)),
            pl.BlockSpec((1, T, width), lambda b, h, i: (b, 0, h)),
            pl.BlockSpec((1, T, width), lambda b, h, i: (b, 0, h)),
            _resident((4, ATTN_HEAD)), _resident((1, LANES)),
        ],
        out_specs=pl.BlockSpec((1, tq, width), lambda b, h, i: (b, i, h)),
        out_shape=jax.ShapeDtypeStruct((B, T, D_MODEL), BF16),
        scratch_shapes=[pltpu.VMEM((2 * nh, rows, tk), F32), pltpu.VMEM((2 * nh, rows, tk), BF16),
                        pltpu.VMEM((nh, rows, LANES), F32), pltpu.VMEM((nh, rows, LANES), F32),
                        pltpu.VMEM((nh, rows, LANES), F32)],
        compiler_params=_params(("arbitrary", "arbitrary", "arbitrary")),
        name="diff_flash",
    )(q, k, v, lam_p, sg)


def _dec_attn_kernel(*refs, lam_init, n_pg):
    pt_ref, q_ref = refs[0], refs[1]
    k_refs = refs[2:2 + n_pg]
    v_refs = refs[2 + n_pg:2 + 2 * n_pg]
    kn_ref, vn_ref, lam_ref, sg_ref, o_ref, m_sc, l_sc, acc_sc = refs[2 + 2 * n_pg:]
    del pt_ref
    j = pl.program_id(1)
    H = ATTN_HEADS
    G = 2 * H

    @pl.when(j == 0)
    def _():
        m_sc[...] = jnp.full(m_sc.shape, _NEG, F32)
        l_sc[...] = jnp.zeros(l_sc.shape, F32)
        acc_sc[...] = jnp.zeros(acc_sc.shape, F32)

    q8 = q_ref[0]
    lane = lax.broadcasted_iota(jnp.int32, (H, LANES), 1)
    zero = jnp.zeros_like(q8)
    q16 = jnp.concatenate([jnp.where(lane < ATTN_HEAD, q8, zero), jnp.where(lane < ATTN_HEAD, zero, q8)], axis=0)

    n = PAGE_SIZE * H
    same_head = (lax.broadcasted_iota(jnp.int32, (G, n), 0) % H) == (lax.broadcasted_iota(jnp.int32, (G, n), 1) % H)
    s = [jnp.where(same_head, _dot_nt(q16, kr[...].reshape(n, LANES).astype(BF16)), _NEG) for kr in k_refs]
    m_old = m_sc[...]
    m_new = m_old
    for z in s:
        m_new = jnp.maximum(m_new, jnp.max(z, axis=-1, keepdims=True))
    alpha = jnp.exp2(m_old - m_new)
    p = [jnp.exp2(z - m_new) for z in s]
    l_sc[...] = alpha * l_sc[...] + sum(jnp.sum(z, axis=-1, keepdims=True) for z in p)
    pv = sum(_dot(z.astype(BF16), vr[...].reshape(n, LANES).astype(BF16)) for z, vr in zip(p, v_refs))
    acc_sc[...] = alpha * acc_sc[...] + pv
    m_sc[...] = m_new

    @pl.when(j == pl.num_programs(1) - 1)
    def _():
        kn = kn_ref[0]
        vn = vn_ref[0]
        kn16 = jnp.concatenate([kn, kn], axis=0)
        vn16 = jnp.concatenate([vn, vn], axis=0)
        sn = jnp.sum(q16.astype(F32) * kn16, axis=-1, keepdims=True)
        m0 = m_sc[...]
        m1 = jnp.maximum(m0, sn)
        a0 = jnp.exp2(m0 - m1)
        pn = jnp.exp2(sn - m1)
        l1 = a0 * l_sc[...] + pn
        acc1 = a0 * acc_sc[...] + pn * vn16
        on = acc1 / l1
        lam = _lambda(lam_ref, lam_init)
        o = on[:H] - lam * on[H:]
        o_ref[0] = _subln(o, sg_ref[...], lam_init).astype(o_ref.dtype)


def _dec_attn(page_table, q, cache_k, cache_v, kn, vn, lam_p, sg, lam_init, layer):
    B, n_pages = page_table.shape
    H = ATTN_HEADS
    n_pg = math.gcd(DEC_PAGES, n_pages)
    head = lambda: pl.BlockSpec((1, H, LANES), lambda b, j, pt: (b, 0, 0))
    page = lambda i: pl.BlockSpec((None, None, PAGE_SIZE, H, LANES),
                                  lambda b, j, pt: (layer, pt[b, j * n_pg + i], 0, 0, 0))
    pages = [page(i) for i in range(n_pg)]
    grid_spec = pltpu.PrefetchScalarGridSpec(
        num_scalar_prefetch=1,
        grid=(B, n_pages // n_pg),
        in_specs=[head()] + pages + pages + [head(), head(), _resident((4, ATTN_HEAD)), _resident((1, LANES))],
        out_specs=pl.BlockSpec((1, H, LANES), lambda b, j, pt: (b, 0, 0)),
        scratch_shapes=[pltpu.VMEM((2 * H, 1), F32), pltpu.VMEM((2 * H, 1), F32), pltpu.VMEM((2 * H, LANES), F32)],
    )
    return pl.pallas_call(
        functools.partial(_dec_attn_kernel, lam_init=lam_init, n_pg=n_pg),
        grid_spec=grid_spec,
        out_shape=jax.ShapeDtypeStruct((B, H, LANES), BF16),
        compiler_params=_params(("arbitrary", "arbitrary")),
        name="paged_diff_attn",
    )(page_table, q, *([cache_k] * n_pg), *([cache_v] * n_pg), kn, vn, lam_p, sg)


_V_MU, _V_W0, _V_A0, _V_V0, _V_KK, _V_KA, _V_G = 0, 6, 7, 8, 9, 10, 11
_N_VEC = 16


def _rwkv_proj_kernel(*refs, shifted, has_vfirst):
    it = iter(refs)
    x_ref = next(it)
    prev_ref = next(it)
    shift_ref = next(it) if shifted else None
    vf_ref = next(it) if has_vfirst else None
    vec_ref, w_ref, w1_ref, w2_ref, a1_ref, a2_ref = (next(it) for _ in range(6))
    v1_ref, v2_ref = (next(it), next(it)) if has_vfirst else (None, None)
    g1_ref, g2_ref = next(it), next(it)
    r_ref, lw_ref, k_ref, v_ref, a_ref, b_ref, g_ref, mq_ref, xs_ref = (next(it) for _ in range(9))

    vec = vec_ref[...]
    row = lambda i: vec[i:i + 1]
    gain = row(_V_G)
    xn = _rms(x_ref[0], gain)
    tm = xn.shape[0]
    if shifted:
        pr = _rms(prev_ref[0][7:8], gain)
        pr = jnp.where(pl.program_id(1) == 0, shift_ref[0], pr)
        ridx = lax.broadcasted_iota(jnp.int32, (tm, 1), 0)
        xprev = jnp.where(ridx == 0, pr, pltpu.roll(xn, 1, 0))
        xs_ref[0] = xn[tm - 1:tm]
    else:
        xprev = prev_ref[0]
        xs_ref[0] = xn
    xx = xprev - xn
    mix = lambda j: (xn + xx * row(_V_MU + j)).astype(BF16)

    D = D_MODEL
    r_ref[0] = _dot(mix(0), w_ref[:, :D])
    k = _dot(mix(2), w_ref[:, D:2 * D])
    seg = _seg_ones(RWKV_HEAD)
    kk = k * row(_V_KK)
    sq = kk * kk
    n2 = jnp.concatenate(
        [sum(_dot(p_, seg) for p_ in _split2(sq[:, c:c + LANES])) for c in range(0, D, LANES)], axis=-1)
    kk = kk / jnp.maximum(jnp.sqrt(n2), 1e-12)
    a_ref[0] = -kk
    a = _sigmoid(row(_V_A0) + _dot(_dot(mix(4), a1_ref[...]).astype(BF16), a2_ref[...]))
    b_ref[0] = kk * a
    k_ref[0] = k * (1.0 + (a - 1.0) * row(_V_KA))
    xv = mix(3)
    v = _dot(xv, w_ref[:, 2 * D:3 * D])
    if has_vfirst:
        gate = _sigmoid(row(_V_V0) + _dot(_dot(xv, v1_ref[...]).astype(BF16), v2_ref[...]))
        v = v + (vf_ref[0] - v) * gate
    v_ref[0] = v
    w_in = row(_V_W0) + _dot(jnp.tanh(_dot(mix(1), w1_ref[...])).astype(BF16), w2_ref[...])
    w_log = -_softplus(-w_in) - 0.5
    lw_ref[0] = -jnp.exp(w_log)
    g_ref[0] = _dot(_sigmoid(_dot(mix(5), g1_ref[...])).astype(BF16), g2_ref[...])
    mq_ref[0] = (_dot(xn.astype(BF16), w_ref[:, 3 * D:]) * (MEM_HEAD ** -0.5)).astype(BF16)


def _rwkv_proj(x, prev, shift, vfirst, vec, w, loras, tm, layer):
    B, T, _ = x.shape
    shifted = shift is not None
    has_vfirst = vfirst is not None
    row = lambda w_: pl.BlockSpec((1, tm, w_), lambda b, i: (b, i, 0))
    ins, specs = [x], [row(D_MODEL)]
    if shifted:
        ins += [prev, shift]
        specs += [pl.BlockSpec((1, 8, D_MODEL), lambda b, i: (b, jnp.maximum(i * (tm // 8) - 1, 0), 0)),
                  pl.BlockSpec((1, 1, D_MODEL), lambda b, i: (b, 0, 0))]
    else:
        ins += [prev]
        specs += [row(D_MODEL)]
    if has_vfirst:
        ins.append(vfirst)
        specs.append(row(D_MODEL))
    w1, w2, a1, a2, v1, v2, g1, g2 = loras
    small = [w1, w2, a1, a2] + ([v1, v2] if has_vfirst else []) + [g1, g2]
    ins += [vec, w] + small
    specs += [_resident(vec.shape), _resident(w.shape[1:], layer)] + [_resident(s.shape) for s in small]
    sd = lambda dt, w_=D_MODEL, t_=T: jax.ShapeDtypeStruct((B, t_, w_), dt)
    xs_rows = 1 if shifted else T
    xs_spec = (pl.BlockSpec((1, 1, D_MODEL), lambda b, i: (b, 0, 0)) if shifted else row(D_MODEL))
    return pl.pallas_call(
        functools.partial(_rwkv_proj_kernel, shifted=shifted, has_vfirst=has_vfirst),
        grid=(B, T // tm),
        in_specs=specs,
        out_specs=[row(D_MODEL)] * 7 + [row(MEM_WIDTH), xs_spec],
        out_shape=[sd(F32)] * 7 + [sd(BF16, MEM_WIDTH), sd(F32, D_MODEL, xs_rows)],
        compiler_params=_params(("arbitrary", "arbitrary")),
        name="rwkv_proj",
    )(*ins)


def _wkv_kernel(r_ref, lw_ref, k_ref, v_ref, a_ref, b_ref, g_ref, vec_ref, s0_ref, y_ref, so_ref, s_sc,
                *, L, NB, P, passes):
    c = pl.program_id(2)
    sel = [(bb, p, slice(p * LANES, (p + 1) * LANES)) for bb in range(NB) for p in range(P)]

    @pl.when(c == 0)
    def _():
        for n, (bb, p, _) in enumerate(sel):
            s_sc[n] = s0_ref[bb, p]

    L2 = 2 * L
    tri = jnp.where(lax.broadcasted_iota(jnp.int32, (L, L), 0) >= lax.broadcasted_iota(jnp.int32, (L, L), 1),
                    1.0, 0.0).astype(BF16)
    r2 = lax.broadcasted_iota(jnp.int32, (L2, L2), 0)
    c2 = lax.broadcasted_iota(jnp.int32, (L2, L2), 1)
    strict = (r2 % L) > (c2 % L)
    incl = (r2 % L) >= (c2 % L)
    eye = jnp.where(r2 == c2, 1.0, 0.0)
    m0 = lax.broadcasted_iota(jnp.int32, (L, LANES), 1) < RWKV_HEAD
    seg = _seg_ones(RWKV_HEAD)
    mm = functools.partial(_mm, passes=passes)

    def stack(z):
        return jnp.concatenate([jnp.where(m0, z, 0.0), jnp.where(m0, 0.0, z)], axis=0)

    def segsum(z):
        hi, lo = _split2(z)
        return _dot(hi, seg) + _dot(lo, seg)

    pairs = range(len(sel))
    r = [r_ref[bb, :, sl] for bb, _, sl in sel]
    lw = [lw_ref[bb, :, sl] for bb, _, sl in sel]
    k = [k_ref[bb, :, sl] for bb, _, sl in sel]
    v = [v_ref[bb, :, sl] for bb, _, sl in sel]
    a = [a_ref[bb, :, sl] for bb, _, sl in sel]
    b = [b_ref[bb, :, sl] for bb, _, sl in sel]
    cum = [_dot_exact_lhs(tri, z) for z in lw]
    c_end = [z[L - 1:L, :] for z in cum]
    e_neg = [jnp.exp(-z) for z in cum]
    at_s = [stack(a[p] * jnp.exp(cum[p] - lw[p])) for p in pairs]
    rt_s = [stack(r[p] * jnp.exp(cum[p])) for p in pairs]
    bt_s = [stack(b[p] * e_neg[p]) for p in pairs]
    kt_s = [stack(k[p] * e_neg[p]) for p in pairs]
    v_s = [stack(z) for z in v]
    S = [s_sc[p] for p in pairs]

    n_ab = [jnp.where(strict, mm(at_s[p], bt_s[p], "nt"), 0.0) for p in pairs]
    a_ak = [jnp.where(strict, mm(at_s[p], kt_s[p], "nt"), 0.0) for p in pairs]
    a_rb = [jnp.where(incl, mm(rt_s[p], bt_s[p], "nt"), 0.0) for p in pairs]
    a_rk = [jnp.where(incl, mm(rt_s[p], kt_s[p], "nt"), 0.0) for p in pairs]

    t_inv = [eye + z for z in n_ab]
    pw = n_ab
    for _ in range(int(math.log2(L)) - 1):
        pw = [mm(z, z, "nn") for z in pw]
        t_inv = [t_inv[p] + mm(t_inv[p], pw[p], "nn") for p in pairs]

    x_s = [mm(at_s[p], S[p], "nt") + mm(a_ak[p], v_s[p], "nn") for p in pairs]
    u_s = [mm(t_inv[p], x_s[p], "nn") for p in pairs]
    y_s = [mm(rt_s[p], S[p], "nt") + mm(a_rb[p], u_s[p], "nn") + mm(a_rk[p], v_s[p], "nn") for p in pairs]
    for p in pairs:
        e_end = jnp.exp(c_end[p] - cum[p])
        s_sc[p] = (S[p] * jnp.exp(c_end[p]) + mm(u_s[p], stack(b[p] * e_end), "tn")
                   + mm(v_s[p], stack(k[p] * e_end), "tn"))

    y = [z[:L] + z[L:] for z in y_s]
    yc = [z - segsum(z) * (1.0 / RWKV_HEAD) for z in y]
    var = [segsum(z * z) * (1.0 / RWKV_HEAD) for z in yc]
    for p, (bb, _, sl) in enumerate(sel):
        vec = vec_ref[:, sl]
        yn = yc[p] * lax.rsqrt(var[p] + GN_EPS) * vec[0:1] + vec[1:2]
        bonus = segsum(r[p] * k[p] * vec[2:3]) * v[p]
        y_ref[bb, :, sl] = ((yn + bonus) * g_ref[bb, :, sl]).astype(y_ref.dtype)

    @pl.when(c == pl.num_programs(2) - 1)
    def _():
        for n, (bb, p, _) in enumerate(sel):
            so_ref[bb, p] = s_sc[n]


def _wkv(r, lw, k, v, a, b, g, vec, s0, L, NB, P, passes):
    B, T, _ = r.shape
    npair = D_MODEL // LANES
    width = P * LANES
    tile = lambda: pl.BlockSpec((NB, L, width), lambda bb, pg, c: (bb, c, pg))
    state = lambda: pl.BlockSpec((NB, P, LANES, LANES), lambda bb, pg, c: (bb, pg, 0, 0))
    return pl.pallas_call(
        functools.partial(_wkv_kernel, L=L, NB=NB, P=P, passes=passes),
        grid=(B // NB, npair // P, T // L),
        in_specs=[tile() for _ in range(7)] + [pl.BlockSpec((8, width), lambda bb, pg, c: (0, pg)), state()],
        out_specs=[tile(), state()],
        out_shape=[jax.ShapeDtypeStruct((B, T, D_MODEL), BF16),
                   jax.ShapeDtypeStruct((B, npair, LANES, LANES), F32)],
        scratch_shapes=[pltpu.VMEM((NB * P, LANES, LANES), F32)],
        compiler_params=_params(("arbitrary", "arbitrary", "arbitrary")),
        name="wkv_chunked",
    )(r, lw, k, v, a, b, g, vec, s0)


def _wkv_step_kernel(r_ref, lw_ref, k_ref, v_ref, a_ref, b_ref, g_ref, vec_ref, s_ref, y_ref, so_ref, *, nb):
    hd = RWKV_HEAD
    npair = D_MODEL // LANES
    rid = lax.broadcasted_iota(jnp.int32, (8, LANES), 0)
    lid = lax.broadcasted_iota(jnp.int32, (8, LANES), 1)
    own = ((rid == 0) & (lid < hd)) | ((rid == 1) & (lid >= hd))
    seg = _seg_ones(hd)
    zpad = jnp.zeros((hd, hd), F32)
    cols = [slice(p * LANES, (p + 1) * LANES) for p in range(npair)]
    combos = [(i, p) for i in range(nb) for p in range(npair)]

    def vec_row(ref, i, p):
        return ref[i:i + 1, cols[p]]

    def at_row(z, i):
        return jnp.where(rid == i, z, 0.0).astype(BF16)

    def split_heads(z):
        return jnp.where(own, z, 0.0)

    S = [jnp.concatenate([jnp.concatenate([s_ref[i, 2 * p], zpad], axis=1),
                          jnp.concatenate([zpad, s_ref[i, 2 * p + 1]], axis=1)], axis=0) for i, p in combos]
    Sb = [z.astype(BF16) for z in S]
    u = [_dot_nt(at_row(vec_row(a_ref, i, p), 0), Sb[n])[0:1] for n, (i, p) in enumerate(combos)]
    lhs = [jnp.concatenate([split_heads(u[n]), split_heads(vec_row(v_ref, i, p))], axis=0).astype(BF16)
           for n, (i, p) in enumerate(combos)]
    rhs = [jnp.concatenate([split_heads(vec_row(b_ref, i, p)), split_heads(vec_row(k_ref, i, p))], axis=0).astype(BF16)
           for i, p in combos]
    S = [S[n] * jnp.exp(vec_row(lw_ref, i, p)) + _dot_tn(lhs[n], rhs[n]) for n, (i, p) in enumerate(combos)]
    for n, (i, p) in enumerate(combos):
        so_ref[i, 2 * p] = S[n][:hd, :hd]
        so_ref[i, 2 * p + 1] = S[n][hd:, hd:]
    yrow = [_dot_nt(at_row(vec_row(r_ref, i, p), i), S[n].astype(BF16)) for n, (i, p) in enumerate(combos)]

    def segsum(z):
        hi, lo = _split2(z)
        return _dot(hi, seg) + _dot(lo, seg)

    for p in range(npair):
        y = yrow[p]
        for i in range(1, nb):
            y = y + yrow[i * npair + p]
        yc = y - segsum(y) * (1.0 / hd)
        var = segsum(yc * yc) * (1.0 / hd)
        vec = vec_ref[:, cols[p]]
        yn = yc * lax.rsqrt(var + GN_EPS) * vec[0:1] + vec[1:2]
        bonus = segsum(r_ref[:, cols[p]] * k_ref[:, cols[p]] * vec[2:3]) * v_ref[:, cols[p]]
        y_ref[:, cols[p]] = (yn + bonus) * g_ref[:, cols[p]]


def _wkv_step(r, lw, k, v, a, b, g, vec, s0, nb=8):
    n_seq = r.shape[0]
    heads = D_MODEL // RWKV_HEAD
    tile = lambda: pl.BlockSpec((nb, D_MODEL), lambda i: (i, 0))
    state = lambda: pl.BlockSpec((nb, heads, RWKV_HEAD, RWKV_HEAD), lambda i: (i, 0, 0, 0))
    return pl.pallas_call(
        functools.partial(_wkv_step_kernel, nb=nb),
        grid=(n_seq // nb,),
        in_specs=[tile() for _ in range(7)] + [_resident((8, D_MODEL)), state()],
        out_specs=[tile(), state()],
        out_shape=[jax.ShapeDtypeStruct((n_seq, D_MODEL), F32),
                   jax.ShapeDtypeStruct((n_seq, heads, RWKV_HEAD, RWKV_HEAD), F32)],
        compiler_params=_params(("arbitrary",)),
        name="wkv_step",
    )(r, lw, k, v, a, b, g, vec, s0)


def _state_to_pairs(s):
    B = s.shape[0]
    s = s.reshape(B, -1, 2, RWKV_HEAD, RWKV_HEAD)
    z = jnp.zeros_like(s[:, :, 0])
    top = jnp.concatenate([s[:, :, 0], z], axis=-1)
    bot = jnp.concatenate([z, s[:, :, 1]], axis=-1)
    return jnp.concatenate([top, bot], axis=-2)


def _state_from_pairs(sp):
    B = sp.shape[0]
    h = RWKV_HEAD
    return jnp.stack([sp[:, :, :h, :h], sp[:, :, h:, h:]], axis=2).reshape(B, -1, h, h)


def _rope_tables(pos):
    half = ATTN_HEAD // 2
    inv = jnp.power(ROPE_THETA, -jnp.arange(half, dtype=F32) * 2.0 / ATTN_HEAD)
    ang = pos.astype(F32)[:, None] * inv[None, :]
    cos = jnp.cos(ang)
    sin = jnp.sin(ang)
    reps = LANES // ATTN_HEAD
    return jnp.tile(cos, (1, 2 * reps)), jnp.tile(jnp.concatenate([-sin, sin], axis=1), (1, reps))


def _pad_cols(w):
    return jnp.pad(w, ((0, 0), (0, LORA_PAD - w.shape[1])))


def _pad_rows(w):
    return jnp.pad(w, ((0, LORA_PAD - w.shape[0]), (0, 0)))


def _pad_tokens(z, t):
    return jnp.pad(z, ((0, 0), (0, t - z.shape[1]), (0, 0)))


def _trunk(x, pos, decode, shift0, wkv0, mem_k, mem_v, past, W):
    B, T, _ = x.shape
    if decode:
        xf = x.reshape(1, B, D_MODEL)
        tm = B
    else:
        xf = x
        tm = 512
    cos_t, sin_t = _rope_tables(pos if not decode else jnp.broadcast_to(pos, (B,)))
    shifts, states, ks, vs = [], [], [], []
    v_first = None
    for l in range(DEPTH):
        idx = l // 2
        if l % 2 == 0:
            vec = jnp.concatenate([
                W["mu"][idx], W["w0"][idx][None], W["a0"][idx][None],
                (W["v0"][idx - 1] if idx > 0 else jnp.zeros((D_MODEL,), F32))[None],
                W["k_k"][idx][None], W["k_a"][idx][None], W["mix_g"][l][None],
                jnp.zeros((_N_VEC - 12, D_MODEL), F32)], axis=0)
            loras = (W["w1"][idx], W["w2"][idx], W["a1"][idx], W["a2"][idx],
                     W["v1"][idx - 1] if idx > 0 else None, W["v2"][idx - 1] if idx > 0 else None,
                     W["g1"][idx], W["g2"][idx])
            if decode:
                outs = _rwkv_proj(xf, shift0[idx][None], None, v_first, vec, W["w_in"], loras, tm, l)
            else:
                outs = _rwkv_proj(xf, xf, shift0[idx][:, None], v_first, vec, W["w_in"], loras, tm // 2, l)
            r, lw, k, v, a, b, g, mq, xs = outs
            if idx == 0:
                v_first = v
            shifts.append(xs.reshape(B, D_MODEL))
            scan_in = [r, lw, k, v, a, b, g]
            vec2 = jnp.concatenate([W["lnx_g"][idx][None], W["lnx_b"][idx][None], W["r_k"][idx][None],
                                    jnp.zeros((5, D_MODEL), F32)], axis=0)
            if decode:
                y, s_new = _wkv_step(*[z.reshape(B, D_MODEL) for z in scan_in], vec2, wkv0[idx])
                states.append(s_new)
                y_tok = y.astype(BF16).reshape(1, B, D_MODEL)
            else:
                y_tok, s_new = _wkv(*scan_in, vec2, _state_to_pairs(wkv0[idx]), WKV_CHUNK, B, WKV_PAIRS, 1)
                states.append(_state_from_pairs(s_new))
        else:
            lam_init = 0.8 - 0.6 * math.exp(-0.3 * l)
            q, kf, kb, vf, vb, mq = _diff_proj(xf, W["mix_g"][l][None], W["w_in"], cos_t, sin_t, tm, l)
            lam_p = jnp.stack([W["lam_q1"][idx], W["lam_k1"][idx], W["lam_q2"][idx], W["lam_k2"][idx]])
            sg = W["subln_g"][idx][None]
            if decode:
                cache_k, cache_v, page_table = past
                hd = lambda z: z.reshape(B, ATTN_HEADS, LANES)
                o = _dec_attn(page_table, hd(q), cache_k, cache_v, hd(kf), hd(vf), lam_p, sg, lam_init, idx)
                y_tok = o.reshape(1, B, D_MODEL)
            else:
                y_tok = _flash(q, kb, vb, lam_p, sg, lam_init, 512, 512, 1, 256)
            ks.append(kf.reshape(B, T, ATTN_HEADS, LANES))
            vs.append(vf.reshape(B, T, ATTN_HEADS, LANES))
        if decode:
            mq8 = _pad_tokens(mq.reshape(B, 1, MEM_WIDTH), 8)
            y_mem = _mem_attend(mq8, mem_k, mem_v, 8, math.gcd(B, 8), l)[:, :1].reshape(1, B, MEM_WIDTH)
        else:
            y_mem = _mem_attend(mq, mem_k, mem_v, 2 * tm, 1, l)
        xf = _out_ffn(xf, y_tok, y_mem, W["w_out"], W["ffn_g"][l][None], W["w_gate"], W["w_up"],
                      W["w_down"], W["final_g"][None], tm, l == DEPTH - 1, l)
    return xf.reshape(B, T, D_MODEL), jnp.stack(shifts), jnp.stack(states), jnp.stack(ks), jnp.stack(vs)


def kernel(x_prompt, x_sample, cache_k, cache_v, cache_mem_k, cache_mem_v, state_rwkv_wkv, state_rwkv_shift, page_table, mem_prompt, w_in, w_out, mix_norm_g, ffn_norm_g, w_gate, w_up, w_down, final_norm_g, mem_norm_g, w_mem_k, w_mem_v, rwkv_mu, rwkv_w0, rwkv_w1, rwkv_w2, rwkv_a0, rwkv_a1, rwkv_a2, rwkv_v0, rwkv_v1, rwkv_v2, rwkv_g1, rwkv_g2, rwkv_k_k, rwkv_k_a, rwkv_r_k, rwkv_lnx_g, rwkv_lnx_b, diff_lam_q1, diff_lam_k1, diff_lam_q2, diff_lam_k2, diff_subln_g):
    bf = lambda z: z.astype(BF16)
    n_rwkv = rwkv_mu.shape[0]
    W = dict(
        w_in=bf(w_in), w_out=bf(w_out), mix_g=mix_norm_g, ffn_g=ffn_norm_g,
        w_gate=bf(w_gate), w_up=bf(w_up), w_down=bf(w_down), final_g=final_norm_g,
        mu=rwkv_mu, w0=rwkv_w0, a0=rwkv_a0, v0=rwkv_v0, k_k=rwkv_k_k, k_a=rwkv_k_a,
        w1=[bf(_pad_cols(rwkv_w1[i])) for i in range(n_rwkv)],
        w2=[bf(_pad_rows(rwkv_w2[i])) for i in range(n_rwkv)],
        a1=[bf(_pad_cols(rwkv_a1[i])) for i in range(n_rwkv)],
        a2=[bf(_pad_rows(rwkv_a2[i])) for i in range(n_rwkv)],
        v1=[bf(_pad_cols(rwkv_v1[i])) for i in range(n_rwkv - 1)],
        v2=[bf(_pad_rows(rwkv_v2[i])) for i in range(n_rwkv - 1)],
        g1=bf(rwkv_g1), g2=bf(rwkv_g2),
        r_k=rwkv_r_k.reshape(n_rwkv, D_MODEL), lnx_g=rwkv_lnx_g, lnx_b=rwkv_lnx_b,
        lam_q1=diff_lam_q1, lam_k1=diff_lam_k1, lam_q2=diff_lam_q2, lam_k2=diff_lam_k2,
        subln_g=diff_subln_g,
    )

    B, T, _ = x_prompt.shape
    M = mem_prompt.shape[1]
    mk, mv = _mem_kv(mem_prompt.reshape(B * M, D_MODEL), mem_norm_g[:, None], bf(w_mem_k), bf(w_mem_v))
    mk = mk.reshape(DEPTH, B, M * MEM_HEADS, MEM_HEAD)
    mv = mv.reshape(DEPTH, B, M * MEM_HEADS, MEM_HEAD)
    p_mem_k = mk.reshape(DEPTH, B, M, MEM_HEADS, MEM_HEAD)
    p_mem_v = mv.reshape(DEPTH, B, M, MEM_HEADS, MEM_HEAD)

    pos_p = jnp.arange(T, dtype=jnp.int32)
    shift0 = jnp.zeros((n_rwkv, B, D_MODEL), F32)
    wkv0 = jnp.zeros((n_rwkv, B, D_MODEL // RWKV_HEAD, RWKV_HEAD, RWKV_HEAD), F32)
    y_prompt, p_shift, p_wkv, p_k, p_v = _trunk(x_prompt, pos_p, False, shift0, wkv0, mk, mv, None, W)

    Bs = x_sample.shape[0]
    past_len = page_table.shape[1] * PAGE_SIZE
    pos_s = past_len + jnp.arange(x_sample.shape[1], dtype=jnp.int32)
    smk = cache_mem_k.reshape(DEPTH, Bs, -1, MEM_HEAD)
    smv = cache_mem_v.reshape(DEPTH, Bs, -1, MEM_HEAD)
    y_sample, s_shift, s_wkv, s_k, s_v = _trunk(
        x_sample, pos_s, True, state_rwkv_shift, state_rwkv_wkv, smk, smv, (cache_k, cache_v, page_table), W)

    return (y_prompt, y_sample, p_wkv, p_shift, p_k, p_v, p_mem_k, p_mem_v, s_wkv, s_shift, s_k, s_v)
```

```python
import functools
import math

import jax
import jax.numpy as jnp
from jax import lax
from jax.experimental import pallas as pl
from jax.experimental.pallas import tpu as pltpu

F32 = jnp.float32
BF16 = jnp.bfloat16

D_MODEL = 1024
DEPTH = 4
PAGE_SIZE = 128
RWKV_HEAD = 64
ATTN_HEAD = 64
ATTN_HEADS = D_MODEL // (2 * ATTN_HEAD)
MEM_HEADS = 4
MEM_HEAD = 128
MEM_WIDTH = MEM_HEADS * MEM_HEAD
D_FF = 2816
ROPE_THETA = 10000.0
NORM_EPS = 1e-6
SUBLN_EPS = 1e-5
GN_EPS = 1e-5 * RWKV_HEAD

Q_SCALE = ATTN_HEAD ** -0.5 * math.log2(math.e)
LANES = 128
LORA_PAD = 128
ROW_TILE = 512
RWKV_ROW_TILE = 256
ATTN_TILE = 512
ATTN_ROW_BLOCK = 256
MEM_ATTN_TILE = 1024
STEP_SEQS = 8
WKV_CHUNK = 64
WKV_PAIRS = 8
DEC_PAGES = 8
VMEM_LIMIT = 56 * 1024 * 1024

_NT = (((1,), (1,)), ((), ()))
_TN = (((0,), (0,)), ((), ()))
_NEG = -1e30


def _dot(a, b):
    return jnp.dot(a, b, preferred_element_type=F32)


def _dot_nt(a, b):
    return lax.dot_general(a, b, _NT, preferred_element_type=F32)


def _dot_tn(a, b):
    return lax.dot_general(a, b, _TN, preferred_element_type=F32)


def _split2(x):
    hi = x.astype(BF16)
    lo = (x - hi.astype(F32)).astype(BF16)
    return hi, lo


def _split3(x):
    h1 = x.astype(BF16)
    r1 = x - h1.astype(F32)
    h2 = r1.astype(BF16)
    h3 = (r1 - h2.astype(F32)).astype(BF16)
    return h1, h2, h3


def _dot_exact_lhs(m_bf16, x):
    h1, h2, h3 = _split3(x)
    return _dot(m_bf16, h1) + _dot(m_bf16, h2) + _dot(m_bf16, h3)


def _mm(a, b, kind, passes):
    f = {"nn": _dot, "nt": _dot_nt, "tn": _dot_tn}[kind]
    if passes == 1:
        return f(a.astype(BF16), b.astype(BF16))
    ah, al = _split2(a)
    bh, bl = _split2(b)
    return f(ah, bh) + f(ah, bl) + f(al, bh)


def _rms(x, g):
    ms = jnp.mean(x * x, axis=-1, keepdims=True)
    return x * lax.rsqrt(ms + NORM_EPS) * g


def _sigmoid(x):
    return 1.0 / (1.0 + jnp.exp(-x))


def _softplus(x):
    return jnp.maximum(x, 0.0) + jnp.log(1.0 + jnp.exp(-jnp.abs(x)))


def _seg_ones(width):
    r = lax.broadcasted_iota(jnp.int32, (LANES, LANES), 0) // width
    c = lax.broadcasted_iota(jnp.int32, (LANES, LANES), 1) // width
    return jnp.where(r == c, 1.0, 0.0).astype(BF16)


def _resident(shape, layer=None):
    nd = len(shape)
    if layer is None:
        return pl.BlockSpec(shape, lambda *_: (0,) * nd, pipeline_mode=pl.Buffered(1))
    return pl.BlockSpec((None,) + tuple(shape), lambda *_: (layer,) + (0,) * nd, pipeline_mode=pl.Buffered(1))


def _params(sem):
    return pltpu.CompilerParams(dimension_semantics=sem, vmem_limit_bytes=VMEM_LIMIT)


def _mem_kv_kernel(mem_ref, g_ref, wk_ref, wv_ref, k_ref, v_ref):
    mn = _rms(mem_ref[...], g_ref[0]).astype(BF16)
    rows = mn.shape[0]
    for w_ref, o_ref in ((wk_ref, k_ref), (wv_ref, v_ref)):
        z = _dot(mn, w_ref[0])
        for h in range(MEM_HEADS):
            o_ref[0, pl.ds(h, rows, stride=MEM_HEADS), :] = z[:, h * MEM_HEAD:(h + 1) * MEM_HEAD]


def _mem_kv(mem2d, g, wk, wv):
    rows = mem2d.shape[0]
    out = jax.ShapeDtypeStruct((DEPTH, rows * MEM_HEADS, MEM_HEAD), F32)
    return pl.pallas_call(
        _mem_kv_kernel,
        grid=(DEPTH,),
        in_specs=[
            pl.BlockSpec((rows, D_MODEL), lambda l: (0, 0)),
            pl.BlockSpec((1, 1, D_MODEL), lambda l: (l, 0, 0)),
            pl.BlockSpec((1, D_MODEL, MEM_WIDTH), lambda l: (l, 0, 0)),
            pl.BlockSpec((1, D_MODEL, MEM_WIDTH), lambda l: (l, 0, 0)),
        ],
        out_specs=[pl.BlockSpec((1, rows * MEM_HEADS, MEM_HEAD), lambda l: (l, 0, 0))] * 2,
        out_shape=[out, out],
        compiler_params=_params(("arbitrary",)),
        name="mem_kv",
    )(mem2d, g, wk, wv)


def _mem_attend_kernel(q_ref, k_ref, v_ref, o_ref):
    M = k_ref.shape[1] // MEM_HEADS
    nb = q_ref.shape[0]
    chains = [(i, h) for i in range(nb) for h in range(MEM_HEADS)]
    q = [q_ref[i, :, h * MEM_HEAD:(h + 1) * MEM_HEAD] for i, h in chains]
    k = [k_ref[i, pl.ds(h, M, stride=MEM_HEADS), :].astype(BF16) for i, h in chains]
    v = [v_ref[i, pl.ds(h, M, stride=MEM_HEADS), :].astype(BF16) for i, h in chains]
    s = [_dot_nt(a, b) for a, b in zip(q, k)]
    p = [jnp.exp(z - jnp.max(z, axis=-1, keepdims=True)) for z in s]
    o = [_dot(a.astype(BF16), b) / jnp.sum(a, axis=-1, keepdims=True) for a, b in zip(p, v)]
    for i in range(nb):
        o_ref[i] = jnp.concatenate(o[i * MEM_HEADS:(i + 1) * MEM_HEADS], axis=-1).astype(o_ref.dtype)


def _mem_attend(q, mk, mv, tq, nb, layer):
    B, T, _ = q.shape
    MH = mk.shape[2]
    return pl.pallas_call(
        _mem_attend_kernel,
        grid=(B // nb, T // tq),
        in_specs=[
            pl.BlockSpec((nb, tq, MEM_WIDTH), lambda b, i: (b, i, 0)),
            pl.BlockSpec((None, nb, MH, MEM_HEAD), lambda b, i: (layer, b, 0, 0)),
            pl.BlockSpec((None, nb, MH, MEM_HEAD), lambda b, i: (layer, b, 0, 0)),
        ],
        out_specs=pl.BlockSpec((nb, tq, MEM_WIDTH), lambda b, i: (b, i, 0)),
        out_shape=jax.ShapeDtypeStruct((B, T, MEM_WIDTH), BF16),
        compiler_params=_params(("arbitrary", "arbitrary")),
        name="mem_attend",
    )(q, mk, mv)


def _out_ffn_kernel(x_ref, yt_ref, ym_ref, wo_ref, g_ref, wg_ref, wu_ref, wd_ref, fg_ref, o_ref,
                    *, final, ft):
    x1 = x_ref[0] + (_dot(yt_ref[0], wo_ref[:D_MODEL, :]) + _dot(ym_ref[0], wo_ref[D_MODEL:, :]))
    h = _rms(x1, g_ref[...]).astype(BF16)
    acc = jnp.zeros_like(x1)
    for f in range(0, D_FF, ft):
        gt = _dot(h, wg_ref[:, f:f + ft])
        up = _dot(h, wu_ref[:, f:f + ft])
        act = (gt * _sigmoid(gt) * up).astype(BF16)
        acc = acc + _dot(act, wd_ref[f:f + ft, :])
    acc = x1 + acc
    if final:
        acc = _rms(acc, fg_ref[...])
    o_ref[0] = acc


def _out_ffn(x, ytok, ymem, wo, g, wg, wu, wd, fg, tm, final, layer):
    B, T, _ = x.shape
    row = lambda w: pl.BlockSpec((1, tm, w), lambda b, i: (b, i, 0))
    return pl.pallas_call(
        functools.partial(_out_ffn_kernel, final=final, ft=256),
        grid=(B, T // tm),
        in_specs=[
            row(D_MODEL), row(D_MODEL), row(MEM_WIDTH),
            _resident((D_MODEL + MEM_WIDTH, D_MODEL), layer),
            _resident((1, D_MODEL)),
            _resident((D_MODEL, D_FF), layer), _resident((D_MODEL, D_FF), layer), _resident((D_FF, D_MODEL), layer),
            _resident((1, D_MODEL)),
        ],
        out_specs=row(D_MODEL),
        out_shape=jax.ShapeDtypeStruct((B, T, D_MODEL), F32),
        compiler_params=_params(("arbitrary", "arbitrary")),
        name="out_ffn",
    )(x, ytok, ymem, wo, g, wg, wu, wd, fg)


def _diff_proj_kernel(x_ref, g_ref, w_ref, cos_ref, sin_ref, k_all_ref, v_all_ref,
                      q_ref, k_ref, kb_ref, v_ref, vb_ref, mq_ref):
    del k_all_ref, v_all_ref
    xn = _rms(x_ref[0], g_ref[...]).astype(BF16)
    tm = xn.shape[0]
    cos = cos_ref[...]
    sin = sin_ref[...]
    lane = lax.broadcasted_iota(jnp.int32, (tm, LANES), 1)
    first = (lane % ATTN_HEAD) < (ATTN_HEAD // 2)

    def rope(z):
        rot = jnp.where(first, pltpu.roll(z, LANES - ATTN_HEAD // 2, 1), pltpu.roll(z, ATTN_HEAD // 2, 1))
        return z * cos + rot * sin

    half = D_MODEL // 2
    for c in range(2):
        z = _dot(xn, w_ref[:, c * half:(c + 1) * half])
        for j in range(half // LANES):
            col = c * half + j * LANES
            q_ref[0, :, col:col + LANES] = (rope(z[:, j * LANES:(j + 1) * LANES]) * Q_SCALE).astype(BF16)
    for c in range(2):
        z = _dot(xn, w_ref[:, D_MODEL + c * half:D_MODEL + (c + 1) * half])
        for j in range(half // LANES):
            col = c * half + j * LANES
            kr = rope(z[:, j * LANES:(j + 1) * LANES])
            k_ref[0, pl.ds(col // LANES, tm, stride=ATTN_HEADS), :] = kr
            kb_ref[0, :, col:col + LANES] = kr.astype(BF16)
    for c in range(2):
        z = _dot(xn, w_ref[:, 2 * D_MODEL + c * half:2 * D_MODEL + (c + 1) * half])
        for j in range(half // LANES):
            head = (c * half) // LANES + j
            v_ref[0, pl.ds(head, tm, stride=ATTN_HEADS), :] = z[:, j * LANES:(j + 1) * LANES]
        vb_ref[0, :, c * half:(c + 1) * half] = z.astype(BF16)
    z = _dot(xn, w_ref[:, 3 * D_MODEL:])
    mq_ref[0] = (z * (MEM_HEAD ** -0.5)).astype(BF16)


def _diff_proj(x, g, w, cos_t, sin_t, tm, layer, k_all, v_all, slab):
    B, T, _ = x.shape
    row = lambda w_: pl.BlockSpec((1, tm, w_), lambda b, i: (b, i, 0))
    sd = lambda w_, dt: jax.ShapeDtypeStruct((B, T, w_), dt)
    by_head = pl.BlockSpec((None, 1, tm * ATTN_HEADS, LANES), lambda b, i: (slab, b, i, 0))
    sd_head = jax.ShapeDtypeStruct(k_all.shape, F32)
    in_hbm = pl.BlockSpec(memory_space=pl.ANY)
    return pl.pallas_call(
        _diff_proj_kernel,
        grid=(B, T // tm),
        in_specs=[
            row(D_MODEL), _resident((1, D_MODEL)), _resident((D_MODEL, 3 * D_MODEL + MEM_WIDTH), layer),
            pl.BlockSpec((tm, LANES), lambda b, i: (i, 0)),
            pl.BlockSpec((tm, LANES), lambda b, i: (i, 0)),
            in_hbm, in_hbm,
        ],
        out_specs=[row(D_MODEL), by_head, row(D_MODEL), by_head, row(D_MODEL), row(MEM_WIDTH)],
        out_shape=[sd(D_MODEL, BF16), sd_head, sd(D_MODEL, BF16), sd_head,
                   sd(D_MODEL, BF16), sd(MEM_WIDTH, BF16)],
        input_output_aliases={5: 1, 6: 3},
        compiler_params=_params(("arbitrary", "arbitrary")),
        name="diff_proj",
    )(x, g, w, cos_t, sin_t, k_all, v_all)


def _lambda(lam_ref, lam_init):
    lp = lam_ref[...]
    s1 = jnp.sum(lp[0:1] * lp[1:2], axis=-1, keepdims=True)
    s2 = jnp.sum(lp[2:3] * lp[3:4], axis=-1, keepdims=True)
    return jnp.exp(s1) - jnp.exp(s2) + lam_init


def _subln(o, sg, lam_init):
    return o * lax.rsqrt(jnp.mean(o * o, axis=-1, keepdims=True) + SUBLN_EPS) * sg * (1.0 - lam_init)


def _flash_kernel(q_ref, k_ref, v_ref, lam_ref, sg_ref, o_ref, s_sc, p_sc, m_sc, l_sc, acc_sc,
                  *, tq, tk, nh, rb, lam_init):
    qi = pl.program_id(2)
    assert tq == tk
    rows = 2 * tq
    nfull = (qi * tq) // tk
    fb = LANES
    heads = range(nh)
    hcol = [slice(h * LANES, (h + 1) * LANES) for h in heads]
    lane = lax.broadcasted_iota(jnp.int32, (tq, LANES), 1)

    def stacked(q):
        zero = jnp.zeros_like(q)
        return jnp.concatenate([jnp.where(lane < ATTN_HEAD, q, zero), jnp.where(lane < ATTN_HEAD, zero, q)], axis=0)

    qs = [stacked(q_ref[0, :, hcol[h]]) for h in heads]

    def kv(ref, i, h):
        return ref[0, pl.ds(pl.multiple_of(i * tk, tk), tk), hcol[h]]

    def fold(parts, op):
        while len(parts) > 1:
            parts = [op(parts[i], parts[i + 1]) for i in range(0, len(parts) - 1, 2)] + parts[len(parts) & ~1:]
        return parts[0]

    blocks =[slice(r0, r0 + rb) for r0 in range(0, rows, rb)]

    def softmax_rows(h, rs, s, pv):
        cols = [s[:, c:c + LANES] for c in range(0, s.shape[1], LANES)]
        m_old = m_sc[h, rs]
        m_new = jnp.maximum(m_old, jnp.max(fold(cols, jnp.maximum), axis=-1, keepdims=True))
        alpha = jnp.exp2(m_old - m_new)
        p = [jnp.exp2(z - m_new) for z in cols]
        l_sc[h, rs] = alpha * l_sc[h, rs] + fold(p, jnp.add)
        acc_sc[h, rs] = alpha * (acc_sc[h, rs] + pv)
        m_sc[h, rs] = m_new
        return jnp.concatenate(p, axis=-1).astype(BF16)

    m_sc[...] = jnp.full(m_sc.shape, _NEG, F32)
    l_sc[...] = jnp.zeros(l_sc.shape, F32)
    acc_sc[...] = jnp.zeros(acc_sc.shape, F32)
    for h in heads:
        p_sc[2 * h + 1] = jnp.zeros(p_sc.shape[1:], BF16)
    for h in heads:
        s_sc[2 * h] = _dot_nt(qs[h], kv(k_ref, 0, h))

    def stage(cur, k):
        for h in heads:
            k_next = kv(k_ref, k + 1, h)
            v_prev = kv(v_ref, jnp.maximum(k - 1, 0), h)
            for rs in blocks:
                s_sc[2 * h + 1 - cur, rs] = _dot_nt(qs[h][rs], k_next)
                pv = _dot(p_sc[2 * h + 1 - cur, rs], v_prev)
                p_sc[2 * h + cur, rs] = softmax_rows(h, rs, s_sc[2 * h + cur, rs], pv)

    def body4(j, carry):
        for i in range(4):
            stage(i % 2, 4 * j + i)
        return carry

    n4 = nfull // 4
    lax.fori_loop(0, n4, body4, 0)

    def body2(j, carry):
        stage(0, 4 * n4 + 2 * j)
        stage(1, 4 * n4 + 2 * j + 1)
        return carry

    lax.fori_loop(0, (nfull - 4 * n4) // 2, body2, 0)

    def finish(cur):
        lam = _lambda(lam_ref, lam_init)
        for h in heads:
            v_prev = kv(v_ref, jnp.maximum(nfull - 1, 0), h)
            v_last = kv(v_ref, nfull, h)
            for r0 in range(0, tq, fb):
                cw = r0 + fb
                visible = (lax.broadcasted_iota(jnp.int32, (fb, cw), 1)
                           <= lax.broadcasted_iota(jnp.int32, (fb, cw), 0) + r0)
                on = []
                for rs in (slice(r0, r0 + fb), slice(tq + r0, tq + r0 + fb)):
                    pv = _dot(p_sc[2 * h + 1 - cur, rs], v_prev)
                    p = softmax_rows(h, rs, jnp.where(visible, s_sc[2 * h + cur, rs, :cw], _NEG), pv)
                    on.append((acc_sc[h, rs] + _dot(p, v_last[:cw])) / jnp.sum(l_sc[h, rs], axis=-1, keepdims=True))
                o = on[0] - lam * on[1]
                o_ref[0, r0:r0 + fb, hcol[h]] = _subln(o, sg_ref[...], lam_init).astype(o_ref.dtype)

    @pl.when(nfull % 2 == 0)
    def _():
        finish(0)

    @pl.when(nfull % 2 == 1)
    def _():
        stage(0, nfull - 1)
        finish(1)


def _flash(q, k, v, lam_p, sg, lam_init, tq, tk, nh, rb):
    B, T, _ = q.shape
    rows = 2 * tq
    width = nh * LANES
    return pl.pallas_call(
        functools.partial(_flash_kernel, tq=tq, tk=tk, nh=nh, rb=rb, lam_init=lam_init),
        grid=(B, ATTN_HEADS // nh, T // tq),
        in_specs=[
            pl.BlockSpec((1, tq, width), lambda b, h, i: (b, i, h)),
            pl.BlockSpec((1, T, width), lambda b, h, i: (b, 0, h)),
            pl.BlockSpec((1, T, width), lambda b, h, i: (b, 0, h)),
            _resident((4, ATTN_HEAD)), _resident((1, LANES)),
        ],
        out_specs=pl.BlockSpec((1, tq, width), lambda b, h, i: (b, i, h)),
        out_shape=jax.ShapeDtypeStruct((B, T, D_MODEL), BF16),
        scratch_shapes=[pltpu.VMEM((2 * nh, rows, tk), F32), pltpu.VMEM((2 * nh, rows, tk), BF16),
                        pltpu.VMEM((nh, rows, LANES), F32), pltpu.VMEM((nh, rows, LANES), F32),
                        pltpu.VMEM((nh, rows, LANES), F32)],
        compiler_params=_params(("arbitrary", "arbitrary", "arbitrary")),
        name="diff_flash",
    )(q, k, v, lam_p, sg)


def _dec_attn_kernel(*refs, lam_init, n_pg):
    pt_ref, q_ref = refs[0], refs[1]
    k_refs = refs[2:2 + n_pg]
    v_refs = refs[2 + n_pg:2 + 2 * n_pg]
    kn_ref, vn_ref, lam_ref, sg_ref, o_ref, m_sc, l_sc, acc_sc = refs[2 + 2 * n_pg:]
    del pt_ref
    j = pl.program_id(1)
    H = ATTN_HEADS
    G = 2 * H

    @pl.when(j == 0)
    def _():
        m_sc[...] = jnp.full(m_sc.shape, _NEG, F32)
        l_sc[...] = jnp.zeros(l_sc.shape, F32)
        acc_sc[...] = jnp.zeros(acc_sc.shape, F32)

    q8 = q_ref[0]
    lane = lax.broadcasted_iota(jnp.int32, (H, LANES), 1)
    zero = jnp.zeros_like(q8)
    q16 = jnp.concatenate([jnp.where(lane < ATTN_HEAD, q8, zero), jnp.where(lane < ATTN_HEAD, zero, q8)], axis=0)

    n = PAGE_SIZE * H
    same_head = (lax.broadcasted_iota(jnp.int32, (G, n), 0) % H) == (lax.broadcasted_iota(jnp.int32, (G, n), 1) % H)
    s = [jnp.where(same_head, _dot_nt(q16, kr[...].reshape(n, LANES).astype(BF16)), _NEG) for kr in k_refs]
    m_old = m_sc[...]
    m_new = m_old
    for z in s:
        m_new = jnp.maximum(m_new, jnp.max(z, axis=-1, keepdims=True))
    alpha = jnp.exp2(m_old - m_new)
    p = [jnp.exp2(z - m_new) for z in s]
    l_sc[...] = alpha * l_sc[...] + sum(jnp.sum(z, axis=-1, keepdims=True) for z in p)
    pv = sum(_dot(z.astype(BF16), vr[...].reshape(n, LANES).astype(BF16)) for z, vr in zip(p, v_refs))
    acc_sc[...] = alpha * acc_sc[...] + pv
    m_sc[...] = m_new

    @pl.when(j == pl.num_programs(1) - 1)
    def _():
        kn = kn_ref[0]
        vn = vn_ref[0]
        kn16 = jnp.concatenate([kn, kn], axis=0)
        vn16 = jnp.concatenate([vn, vn], axis=0)
        sn = jnp.sum(q16.astype(F32) * kn16, axis=-1, keepdims=True)
        m0 = m_sc[...]
        m1 = jnp.maximum(m0, sn)
        a0 = jnp.exp2(m0 - m1)
        pn = jnp.exp2(sn - m1)
        l1 = a0 * l_sc[...] + pn
        acc1 = a0 * acc_sc[...] + pn * vn16
        on = acc1 / l1
        lam = _lambda(lam_ref, lam_init)
        o = on[:H] - lam * on[H:]
        o_ref[0] = _subln(o, sg_ref[...], lam_init).astype(o_ref.dtype)


def _dec_attn(page_table, q, cache_k, cache_v, kn, vn, lam_p, sg, lam_init, layer):
    B, n_pages = page_table.shape
    H = ATTN_HEADS
    n_pg = math.gcd(DEC_PAGES, n_pages)
    head = lambda: pl.BlockSpec((1, H, LANES), lambda b, j, pt: (b, 0, 0))
    page = lambda i: pl.BlockSpec((None, None, PAGE_SIZE, H, LANES),
                                  lambda b, j, pt: (layer, pt[b, j * n_pg + i], 0, 0, 0))
    pages = [page(i) for i in range(n_pg)]
    grid_spec = pltpu.PrefetchScalarGridSpec(
        num_scalar_prefetch=1,
        grid=(B, n_pages // n_pg),
        in_specs=[head()] + pages + pages + [head(), head(), _resident((4, ATTN_HEAD)), _resident((1, LANES))],
        out_specs=pl.BlockSpec((1, H, LANES), lambda b, j, pt: (b, 0, 0)),
        scratch_shapes=[pltpu.VMEM((2 * H, 1), F32), pltpu.VMEM((2 * H, 1), F32), pltpu.VMEM((2 * H, LANES), F32)],
    )
    return pl.pallas_call(
        functools.partial(_dec_attn_kernel, lam_init=lam_init, n_pg=n_pg),
        grid_spec=grid_spec,
        out_shape=jax.ShapeDtypeStruct((B, H, LANES), BF16),
        compiler_params=_params(("arbitrary", "arbitrary")),
        name="paged_diff_attn",
    )(page_table, q, *([cache_k] * n_pg), *([cache_v] * n_pg), kn, vn, lam_p, sg)


_V_MU, _V_W0, _V_A0, _V_V0, _V_KK, _V_KA, _V_G = 0, 6, 7, 8, 9, 10, 11
_N_VEC = 16


def _rwkv_proj_kernel(*refs, shifted, has_vfirst):
    it = iter(refs)
    x_ref = next(it)
    prev_ref = next(it)
    shift_ref = next(it) if shifted else None
    vf_ref = next(it) if has_vfirst else None
    vec_ref, w_ref, w1_ref, w2_ref, a1_ref, a2_ref = (next(it) for _ in range(6))
    v1_ref, v2_ref = (next(it), next(it)) if has_vfirst else (None, None)
    g1_ref, g2_ref = next(it), next(it)
    r_ref, lw_ref, k_ref, v_ref, a_ref, b_ref, g_ref, mq_ref, xs_ref = (next(it) for _ in range(9))

    vec = vec_ref[...]
    row = lambda i: vec[i:i + 1]
    gain = row(_V_G)
    xn = _rms(x_ref[0], gain)
    tm = xn.shape[0]
    if shifted:
        pr = _rms(prev_ref[0][7:8], gain)
        pr = jnp.where(pl.program_id(1) == 0, shift_ref[0], pr)
        ridx = lax.broadcasted_iota(jnp.int32, (tm, 1), 0)
        xprev = jnp.where(ridx == 0, pr, pltpu.roll(xn, 1, 0))
        xs_ref[0] = xn[tm - 1:tm]
    else:
        xprev = prev_ref[0]
        xs_ref[0] = xn
    xx = xprev - xn
    mix = lambda j: (xn + xx * row(_V_MU + j)).astype(BF16)

    D = D_MODEL
    r_ref[0] = _dot(mix(0), w_ref[:, :D])
    k = _dot(mix(2), w_ref[:, D:2 * D])
    seg = _seg_ones(RWKV_HEAD)
    kk = k * row(_V_KK)
    sq = kk * kk
    n2 = jnp.concatenate(
        [sum(_dot(p_, seg) for p_ in _split2(sq[:, c:c + LANES])) for c in range(0, D, LANES)], axis=-1)
    kk = kk / jnp.maximum(jnp.sqrt(n2), 1e-12)
    a_ref[0] = -kk
    a = _sigmoid(row(_V_A0) + _dot(_dot(mix(4), a1_ref[...]).astype(BF16), a2_ref[...]))
    b_ref[0] = kk * a
    k_ref[0] = k * (1.0 + (a - 1.0) * row(_V_KA))
    xv = mix(3)
    v = _dot(xv, w_ref[:, 2 * D:3 * D])
    if has_vfirst:
        gate = _sigmoid(row(_V_V0) + _dot(_dot(xv, v1_ref[...]).astype(BF16), v2_ref[...]))
        v = v + (vf_ref[0] - v) * gate
    v_ref[0] = v
    w_in = row(_V_W0) + _dot(jnp.tanh(_dot(mix(1), w1_ref[...])).astype(BF16), w2_ref[...])
    w_log = -_softplus(-w_in) - 0.5
    lw_ref[0] = -jnp.exp(w_log)
    g_ref[0] = _dot(_sigmoid(_dot(mix(5), g1_ref[...])).astype(BF16), g2_ref[...])
    mq_ref[0] = (_dot(xn.astype(BF16), w_ref[:, 3 * D:]) * (MEM_HEAD ** -0.5)).astype(BF16)


def _rwkv_proj(x, prev, shift, vfirst, vec, w, loras, tm, layer):
    B, T, _ = x.shape
    shifted = shift is not None
    has_vfirst = vfirst is not None
    row = lambda w_: pl.BlockSpec((1, tm, w_), lambda b, i: (b, i, 0))
    ins, specs = [x], [row(D_MODEL)]
    if shifted:
        ins += [prev, shift]
        specs += [pl.BlockSpec((1, 8, D_MODEL), lambda b, i: (b, jnp.maximum(i * (tm // 8) - 1, 0), 0)),
                  pl.BlockSpec((1, 1, D_MODEL), lambda b, i: (b, 0, 0))]
    else:
        ins += [prev]
        specs += [row(D_MODEL)]
    if has_vfirst:
        ins.append(vfirst)
        specs.append(row(D_MODEL))
    w1, w2, a1, a2, v1, v2, g1, g2 = loras
    small = [w1, w2, a1, a2] + ([v1, v2] if has_vfirst else []) + [g1, g2]
    ins += [vec, w] + small
    specs += [_resident(vec.shape), _resident(w.shape[1:], layer)] + [_resident(s.shape) for s in small]
    sd = lambda dt, w_=D_MODEL, t_=T: jax.ShapeDtypeStruct((B, t_, w_), dt)
    xs_rows = 1 if shifted else T
    xs_spec = (pl.BlockSpec((1, 1, D_MODEL), lambda b, i: (b, 0, 0)) if shifted else row(D_MODEL))
    return pl.pallas_call(
        functools.partial(_rwkv_proj_kernel, shifted=shifted, has_vfirst=has_vfirst),
        grid=(B, T // tm),
        in_specs=specs,
        out_specs=[row(D_MODEL)] * 7 + [row(MEM_WIDTH), xs_spec],
        out_shape=[sd(F32)] * 7 + [sd(BF16, MEM_WIDTH), sd(F32, D_MODEL, xs_rows)],
        compiler_params=_params(("arbitrary", "arbitrary")),
        name="rwkv_proj",
    )(*ins)


def _wkv_kernel(r_ref, lw_ref, k_ref, v_ref, a_ref, b_ref, g_ref, vec_ref, s0_ref, y_ref, so_ref, s_sc,
                *, L, NB, P, passes):
    c = pl.program_id(2)
    sel = [(bb, p, slice(p * LANES, (p + 1) * LANES)) for bb in range(NB) for p in range(P)]

    @pl.when(c == 0)
    def _():
        for n, (bb, p, _) in enumerate(sel):
            s_sc[n] = s0_ref[bb, p]

    L2 = 2 * L
    tri = jnp.where(lax.broadcasted_iota(jnp.int32, (L, L), 0) >= lax.broadcasted_iota(jnp.int32, (L, L), 1),
                    1.0, 0.0).astype(BF16)
    r2 = lax.broadcasted_iota(jnp.int32, (L2, L2), 0)
    c2 = lax.broadcasted_iota(jnp.int32, (L2, L2), 1)
    strict = (r2 % L) > (c2 % L)
    incl = (r2 % L) >= (c2 % L)
    eye = jnp.where(r2 == c2, 1.0, 0.0)
    m0 = lax.broadcasted_iota(jnp.int32, (L, LANES), 1) < RWKV_HEAD
    seg = _seg_ones(RWKV_HEAD)
    mm = functools.partial(_mm, passes=passes)

    def stack(z):
        return jnp.concatenate([jnp.where(m0, z, 0.0), jnp.where(m0, 0.0, z)], axis=0)

    def segsum(z):
        hi, lo = _split2(z)
        return _dot(hi, seg) + _dot(lo, seg)

    pairs = range(len(sel))
    r = [r_ref[bb, :, sl] for bb, _, sl in sel]
    lw = [lw_ref[bb, :, sl] for bb, _, sl in sel]
    k = [k_ref[bb, :, sl] for bb, _, sl in sel]
    v = [v_ref[bb, :, sl] for bb, _, sl in sel]
    a = [a_ref[bb, :, sl] for bb, _, sl in sel]
    b = [b_ref[bb, :, sl] for bb, _, sl in sel]
    cum = [_dot_exact_lhs(tri, z) for z in lw]
    c_end = [z[L - 1:L, :] for z in cum]
    e_neg = [jnp.exp(-z) for z in cum]
    at_s = [stack(a[p] * jnp.exp(cum[p] - lw[p])) for p in pairs]
    rt_s = [stack(r[p] * jnp.exp(cum[p])) for p in pairs]
    bt_s = [stack(b[p] * e_neg[p]) for p in pairs]
    kt_s = [stack(k[p] * e_neg[p]) for p in pairs]
    v_s = [stack(z) for z in v]
    S = [s_sc[p] for p in pairs]

    n_ab = [jnp.where(strict, mm(at_s[p], bt_s[p], "nt"), 0.0) for p in pairs]
    a_ak = [jnp.where(strict, mm(at_s[p], kt_s[p], "nt"), 0.0) for p in pairs]
    a_rb = [jnp.where(incl, mm(rt_s[p], bt_s[p], "nt"), 0.0) for p in pairs]
    a_rk = [jnp.where(incl, mm(rt_s[p], kt_s[p], "nt"), 0.0) for p in pairs]

    t_inv = [eye + z for z in n_ab]
    pw = n_ab
    for _ in range(int(math.log2(L)) - 1):
        pw = [mm(z, z, "nn") for z in pw]
        t_inv = [t_inv[p] + mm(t_inv[p], pw[p], "nn") for p in pairs]

    x_s = [mm(at_s[p], S[p], "nt") + mm(a_ak[p], v_s[p], "nn") for p in pairs]
    u_s = [mm(t_inv[p], x_s[p], "nn") for p in pairs]
    y_s = [mm(rt_s[p], S[p], "nt") + mm(a_rb[p], u_s[p], "nn") + mm(a_rk[p], v_s[p], "nn") for p in pairs]
    for p in pairs:
        e_end = jnp.exp(c_end[p] - cum[p])
        s_sc[p] = (S[p] * jnp.exp(c_end[p]) + mm(u_s[p], stack(b[p] * e_end), "tn")
                   + mm(v_s[p], stack(k[p] * e_end), "tn"))

    y = [z[:L] + z[L:] for z in y_s]
    yc = [z - segsum(z) * (1.0 / RWKV_HEAD) for z in y]
    var = [segsum(z * z) * (1.0 / RWKV_HEAD) for z in yc]
    for p, (bb, _, sl) in enumerate(sel):
        vec = vec_ref[:, sl]
        yn = yc[p] * lax.rsqrt(var[p] + GN_EPS) * vec[0:1] + vec[1:2]
        bonus = segsum(r[p] * k[p] * vec[2:3]) * v[p]
        y_ref[bb, :, sl] = ((yn + bonus) * g_ref[bb, :, sl]).astype(y_ref.dtype)

    @pl.when(c == pl.num_programs(2) - 1)
    def _():
        for n, (bb, p, _) in enumerate(sel):
            so_ref[bb, p] = s_sc[n]


def _wkv(r, lw, k, v, a, b, g, vec, s0, L, NB, P, passes):
    B, T, _ = r.shape
    npair = D_MODEL // LANES
    width = P * LANES
    tile = lambda: pl.BlockSpec((NB, L, width), lambda bb, pg, c: (bb, c, pg))
    state = lambda: pl.BlockSpec((NB, P, LANES, LANES), lambda bb, pg, c: (bb, pg, 0, 0))
    return pl.pallas_call(
        functools.partial(_wkv_kernel, L=L, NB=NB, P=P, passes=passes),
        grid=(B // NB, npair // P, T // L),
        in_specs=[tile() for _ in range(7)] + [pl.BlockSpec((8, width), lambda bb, pg, c: (0, pg)), state()],
        out_specs=[tile(), state()],
        out_shape=[jax.ShapeDtypeStruct((B, T, D_MODEL), BF16),
                   jax.ShapeDtypeStruct((B, npair, LANES, LANES), F32)],
        scratch_shapes=[pltpu.VMEM((NB * P, LANES, LANES), F32)],
        compiler_params=_params(("arbitrary", "arbitrary", "arbitrary")),
        name="wkv_chunked",
    )(r, lw, k, v, a, b, g, vec, s0)


def _wkv_step_kernel(r_ref, lw_ref, k_ref, v_ref, a_ref, b_ref, g_ref, vec_ref, s_ref, y_ref, so_ref, *, nb):
    hd = RWKV_HEAD
    npair = D_MODEL // LANES
    rid = lax.broadcasted_iota(jnp.int32, (8, LANES), 0)
    lid = lax.broadcasted_iota(jnp.int32, (8, LANES), 1)
    own = ((rid == 0) & (lid < hd)) | ((rid == 1) & (lid >= hd))
    seg = _seg_ones(hd)
    zpad = jnp.zeros((hd, hd), F32)
    cols = [slice(p * LANES, (p + 1) * LANES) for p in range(npair)]
    combos = [(i, p) for i in range(nb) for p in range(npair)]

    def vec_row(ref, i, p):
        return ref[i:i + 1, cols[p]]

    def at_row(z, i):
        return jnp.where(rid == i, z, 0.0).astype(BF16)

    def split_heads(z):
        return jnp.where(own, z, 0.0)

    S = [jnp.concatenate([jnp.concatenate([s_ref[i, 2 * p], zpad], axis=1),
                          jnp.concatenate([zpad, s_ref[i, 2 * p + 1]], axis=1)], axis=0) for i, p in combos]
    Sb = [z.astype(BF16) for z in S]
    u = [_dot_nt(at_row(vec_row(a_ref, i, p), 0), Sb[n])[0:1] for n, (i, p) in enumerate(combos)]
    lhs = [jnp.concatenate([split_heads(u[n]), split_heads(vec_row(v_ref, i, p))], axis=0).astype(BF16)
           for n, (i, p) in enumerate(combos)]
    rhs = [jnp.concatenate([split_heads(vec_row(b_ref, i, p)), split_heads(vec_row(k_ref, i, p))], axis=0).astype(BF16)
           for i, p in combos]
    S = [S[n] * jnp.exp(vec_row(lw_ref, i, p)) + _dot_tn(lhs[n], rhs[n]) for n, (i, p) in enumerate(combos)]
    for n, (i, p) in enumerate(combos):
        so_ref[i, 2 * p] = S[n][:hd, :hd]
        so_ref[i, 2 * p + 1] = S[n][hd:, hd:]
    yrow = [_dot_nt(at_row(vec_row(r_ref, i, p), i), S[n].astype(BF16)) for n, (i, p) in enumerate(combos)]

    def segsum(z):
        hi, lo = _split2(z)
        return _dot(hi, seg) + _dot(lo, seg)

    for p in range(npair):
        y = yrow[p]
        for i in range(1, nb):
            y = y + yrow[i * npair + p]
        yc = y - segsum(y) * (1.0 / hd)
        var = segsum(yc * yc) * (1.0 / hd)
        vec = vec_ref[:, cols[p]]
        yn = yc * lax.rsqrt(var + GN_EPS) * vec[0:1] + vec[1:2]
        bonus = segsum(r_ref[:, cols[p]] * k_ref[:, cols[p]] * vec[2:3]) * v_ref[:, cols[p]]
        y_ref[:, cols[p]] = (yn + bonus) * g_ref[:, cols[p]]


def _wkv_step(r, lw, k, v, a, b, g, vec, s0):
    n_seq = r.shape[0]
    nb = math.gcd(n_seq, STEP_SEQS)
    heads = D_MODEL // RWKV_HEAD
    tile = lambda: pl.BlockSpec((nb, D_MODEL), lambda i: (i, 0))
    state = lambda: pl.BlockSpec((nb, heads, RWKV_HEAD, RWKV_HEAD), lambda i: (i, 0, 0, 0))
    return pl.pallas_call(
        functools.partial(_wkv_step_kernel, nb=nb),
        grid=(n_seq // nb,),
        in_specs=[tile() for _ in range(7)] + [_resident((8, D_MODEL)), state()],
        out_specs=[tile(), state()],
        out_shape=[jax.ShapeDtypeStruct((n_seq, D_MODEL), F32),
                   jax.ShapeDtypeStruct((n_seq, heads, RWKV_HEAD, RWKV_HEAD), F32)],
        compiler_params=_params(("arbitrary",)),
        name="wkv_step",
    )(r, lw, k, v, a, b, g, vec, s0)


def _state_to_pairs(s):
    B = s.shape[0]
    s = s.reshape(B, -1, 2, RWKV_HEAD, RWKV_HEAD)
    z = jnp.zeros_like(s[:, :, 0])
    top = jnp.concatenate([s[:, :, 0], z], axis=-1)
    bot = jnp.concatenate([z, s[:, :, 1]], axis=-1)
    return jnp.concatenate([top, bot], axis=-2)


def _state_from_pairs(sp):
    B = sp.shape[0]
    h = RWKV_HEAD
    return jnp.stack([sp[:, :, :h, :h], sp[:, :, h:, h:]], axis=2).reshape(B, -1, h, h)


def _rope_tables(pos):
    half = ATTN_HEAD // 2
    inv = jnp.power(ROPE_THETA, -jnp.arange(half, dtype=F32) * 2.0 / ATTN_HEAD)
    ang = pos.astype(F32)[:, None] * inv[None, :]
    cos = jnp.cos(ang)
    sin = jnp.sin(ang)
    reps = LANES // ATTN_HEAD
    return jnp.tile(cos, (1, 2 * reps)), jnp.tile(jnp.concatenate([-sin, sin], axis=1), (1, reps))


def _pad_cols(w):
    return jnp.pad(w, ((0, 0), (0, LORA_PAD - w.shape[1])))


def _pad_rows(w):
    return jnp.pad(w, ((0, LORA_PAD - w.shape[0]), (0, 0)))


def _pad_tokens(z, t):
    return jnp.pad(z, ((0, 0), (0, t - z.shape[1]), (0, 0)))


def _trunk(x, pos, decode, shift0, wkv0, mem_k, mem_v, past, W):
    B, T, _ = x.shape
    if decode:
        xf = x.reshape(1, B, D_MODEL)
        tm = B
    else:
        xf = x
        tm = ROW_TILE
    cos_t, sin_t = _rope_tables(pos if not decode else jnp.broadcast_to(pos, (B,)))
    shifts, states = [], []
    xb, xt = xf.shape[:2]
    k_all = jnp.zeros((DEPTH // 2, xb, xt * ATTN_HEADS, LANES), F32)
    v_all = jnp.zeros_like(k_all)
    v_first = None
    for l in range(DEPTH):
        idx = l // 2
        if l % 2 == 0:
            vec = jnp.concatenate([
                W["mu"][idx], W["w0"][idx][None], W["a0"][idx][None],
                (W["v0"][idx - 1] if idx > 0 else jnp.zeros((D_MODEL,), F32))[None],
                W["k_k"][idx][None], W["k_a"][idx][None], W["mix_g"][l][None],
                jnp.zeros((_N_VEC - 12, D_MODEL), F32)], axis=0)
            loras = (W["w1"][idx], W["w2"][idx], W["a1"][idx], W["a2"][idx],
                     W["v1"][idx - 1] if idx > 0 else None, W["v2"][idx - 1] if idx > 0 else None,
                     W["g1"][idx], W["g2"][idx])
            if decode:
                outs = _rwkv_proj(xf, shift0[idx][None], None, v_first, vec, W["w_in"], loras, tm, l)
            else:
                outs = _rwkv_proj(xf, xf, shift0[idx][:, None], v_first, vec, W["w_in"], loras, RWKV_ROW_TILE, l)
            r, lw, k, v, a, b, g, mq, xs = outs
            if idx == 0:
                v_first = v
            shifts.append(xs.reshape(B, D_MODEL))
            scan_in = [r, lw, k, v, a, b, g]
            vec2 = jnp.concatenate([W["lnx_g"][idx][None], W["lnx_b"][idx][None], W["r_k"][idx][None],
                                    jnp.zeros((5, D_MODEL), F32)], axis=0)
            if decode:
                y, s_new = _wkv_step(*[z.reshape(B, D_MODEL) for z in scan_in], vec2, wkv0[idx])
                states.append(s_new)
                y_tok = y.astype(BF16).reshape(1, B, D_MODEL)
            else:
                y_tok, s_new = _wkv(*scan_in, vec2, _state_to_pairs(wkv0[idx]), WKV_CHUNK, B, WKV_PAIRS, 1)
                states.append(_state_from_pairs(s_new))
        else:
            lam_init = 0.8 - 0.6 * math.exp(-0.3 * l)
            q, k_all, kb, v_all, vb, mq = _diff_proj(xf, W["mix_g"][l][None], W["w_in"], cos_t, sin_t, tm, l,
                                                     k_all, v_all, idx)
            lam_p = jnp.stack([W["lam_q1"][idx], W["lam_k1"][idx], W["lam_q2"][idx], W["lam_k2"][idx]])
            sg = W["subln_g"][idx][None]
            if decode:
                cache_k, cache_v, page_table = past
                hd = lambda z: z.reshape(B, ATTN_HEADS, LANES)
                o = _dec_attn(page_table, hd(q), cache_k, cache_v, hd(k_all[idx]), hd(v_all[idx]), lam_p, sg,
                              lam_init, idx)
                y_tok = o.reshape(1, B, D_MODEL)
            else:
                y_tok = _flash(q, kb, vb, lam_p, sg, lam_init, ATTN_TILE, ATTN_TILE, 1, ATTN_ROW_BLOCK)
        if decode:
            mq8 = _pad_tokens(mq.reshape(B, 1, MEM_WIDTH), 8)
            y_mem = _mem_attend(mq8, mem_k, mem_v, 8, math.gcd(B, STEP_SEQS), l)[:, :1].reshape(1, B, MEM_WIDTH)
        else:
            y_mem = _mem_attend(mq, mem_k, mem_v, MEM_ATTN_TILE, 1, l)
        xf = _out_ffn(xf, y_tok, y_mem, W["w_out"], W["ffn_g"][l][None], W["w_gate"], W["w_up"],
                      W["w_down"], W["final_g"][None], tm, l == DEPTH - 1, l)
    kv_shape = (DEPTH // 2, B, T, ATTN_HEADS, LANES)
    return (xf.reshape(B, T, D_MODEL), jnp.stack(shifts), jnp.stack(states),
            k_all.reshape(kv_shape), v_all.reshape(kv_shape))


def kernel(x_prompt, x_sample, cache_k, cache_v, cache_mem_k, cache_mem_v, state_rwkv_wkv, state_rwkv_shift, page_table, mem_prompt, w_in, w_out, mix_norm_g, ffn_norm_g, w_gate, w_up, w_down, final_norm_g, mem_norm_g, w_mem_k, w_mem_v, rwkv_mu, rwkv_w0, rwkv_w1, rwkv_w2, rwkv_a0, rwkv_a1, rwkv_a2, rwkv_v0, rwkv_v1, rwkv_v2, rwkv_g1, rwkv_g2, rwkv_k_k, rwkv_k_a, rwkv_r_k, rwkv_lnx_g, rwkv_lnx_b, diff_lam_q1, diff_lam_k1, diff_lam_q2, diff_lam_k2, diff_subln_g):
    bf = lambda z: z.astype(BF16)
    n_rwkv = rwkv_mu.shape[0]
    W = dict(
        w_in=bf(w_in), w_out=bf(w_out), mix_g=mix_norm_g, ffn_g=ffn_norm_g,
        w_gate=bf(w_gate), w_up=bf(w_up), w_down=bf(w_down), final_g=final_norm_g,
        mu=rwkv_mu, w0=rwkv_w0, a0=rwkv_a0, v0=rwkv_v0, k_k=rwkv_k_k, k_a=rwkv_k_a,
        w1=[bf(_pad_cols(rwkv_w1[i])) for i in range(n_rwkv)],
        w2=[bf(_pad_rows(rwkv_w2[i])) for i in range(n_rwkv)],
        a1=[bf(_pad_cols(rwkv_a1[i])) for i in range(n_rwkv)],
        a2=[bf(_pad_rows(rwkv_a2[i])) for i in range(n_rwkv)],
        v1=[bf(_pad_cols(rwkv_v1[i])) for i in range(n_rwkv - 1)],
        v2=[bf(_pad_rows(rwkv_v2[i])) for i in range(n_rwkv - 1)],
        g1=bf(rwkv_g1), g2=bf(rwkv_g2),
        r_k=rwkv_r_k.reshape(n_rwkv, D_MODEL), lnx_g=rwkv_lnx_g, lnx_b=rwkv_lnx_b,
        lam_q1=diff_lam_q1, lam_k1=diff_lam_k1, lam_q2=diff_lam_q2, lam_k2=diff_lam_k2,
        subln_g=diff_subln_g,
    )

    B, T, _ = x_prompt.shape
    M = mem_prompt.shape[1]
    mk, mv = _mem_kv(mem_prompt.reshape(B * M, D_MODEL), mem_norm_g[:, None], bf(w_mem_k), bf(w_mem_v))
    mk = mk.reshape(DEPTH, B, M * MEM_HEADS, MEM_HEAD)
    mv = mv.reshape(DEPTH, B, M * MEM_HEADS, MEM_HEAD)
    p_mem_k = mk.reshape(DEPTH, B, M, MEM_HEADS, MEM_HEAD)
    p_mem_v = mv.reshape(DEPTH, B, M, MEM_HEADS, MEM_HEAD)

    pos_p = jnp.arange(T, dtype=jnp.int32)
    shift0 = jnp.zeros((n_rwkv, B, D_MODEL), F32)
    wkv0 = jnp.zeros((n_rwkv, B, D_MODEL // RWKV_HEAD, RWKV_HEAD, RWKV_HEAD), F32)
    y_prompt, p_shift, p_wkv, p_k, p_v = _trunk(x_prompt, pos_p, False, shift0, wkv0, mk, mv, None, W)

    Bs = x_sample.shape[0]
    past_len = page_table.shape[1] * PAGE_SIZE
    pos_s = past_len + jnp.arange(x_sample.shape[1], dtype=jnp.int32)
    smk = cache_mem_k.reshape(DEPTH, Bs, -1, MEM_HEAD)
    smv = cache_mem_v.reshape(DEPTH, Bs, -1, MEM_HEAD)
    y_sample, s_shift, s_wkv, s_k, s_v = _trunk(
        x_sample, pos_s, True, state_rwkv_shift, state_rwkv_wkv, smk, smv, (cache_k, cache_v, page_table), W)

    return (y_prompt, y_sample, p_wkv, p_shift, p_k, p_v, p_mem_k, p_mem_v, s_wkv, s_shift, s_k, s_v)
```

```python
import functools
import math

import jax
import jax.numpy as jnp
from jax import lax
from jax.experimental import pallas as pl
from jax.experimental.pallas import tpu as pltpu

F32 = jnp.float32
BF16 = jnp.bfloat16

D_MODEL = 1024
DEPTH = 4
PAGE_SIZE = 128
RWKV_HEAD = 64
ATTN_HEAD = 64
ATTN_HEADS = D_MODEL // (2 * ATTN_HEAD)
MEM_HEADS = 4
MEM_HEAD = 128
MEM_WIDTH = MEM_HEADS * MEM_HEAD
D_FF = 2816
ROPE_THETA = 10000.0
NORM_EPS = 1e-6
SUBLN_EPS = 1e-5
GN_EPS = 1e-5 * RWKV_HEAD

Q_SCALE = ATTN_HEAD ** -0.5 * math.log2(math.e)
LANES = 128
LORA_PAD = 128
ROW_TILE = 512
RWKV_ROW_TILE = 256
ATTN_TILE = 512
ATTN_ROW_BLOCK = 256
MEM_ATTN_TILE = 1024
STEP_SEQS = 8
WKV_CHUNK = 64
WKV_PAIRS = 8
DEC_PAGES = 16
VMEM_LIMIT = 56 * 1024 * 1024

_NT = (((1,), (1,)), ((), ()))
_TN = (((0,), (0,)), ((), ()))
_NEG = -1e30


def _dot(a, b):
    return jnp.dot(a, b, preferred_element_type=F32)


def _dot_nt(a, b):
    return lax.dot_general(a, b, _NT, preferred_element_type=F32)


def _dot_tn(a, b):
    return lax.dot_general(a, b, _TN, preferred_element_type=F32)


def _split2(x):
    hi = x.astype(BF16)
    lo = (x - hi.astype(F32)).astype(BF16)
    return hi, lo


def _split3(x):
    h1 = x.astype(BF16)
    r1 = x - h1.astype(F32)
    h2 = r1.astype(BF16)
    h3 = (r1 - h2.astype(F32)).astype(BF16)
    return h1, h2, h3


def _dot_exact_lhs(m_bf16, x):
    h1, h2, h3 = _split3(x)
    return _dot(m_bf16, h1) + _dot(m_bf16, h2) + _dot(m_bf16, h3)


def _mm(a, b, kind, passes):
    f = {"nn": _dot, "nt": _dot_nt, "tn": _dot_tn}[kind]
    if passes == 1:
        return f(a.astype(BF16), b.astype(BF16))
    ah, al = _split2(a)
    bh, bl = _split2(b)
    return f(ah, bh) + f(ah, bl) + f(al, bh)


def _rms(x, g):
    ms = jnp.mean(x * x, axis=-1, keepdims=True)
    return x * lax.rsqrt(ms + NORM_EPS) * g


def _sigmoid(x):
    return 1.0 / (1.0 + jnp.exp(-x))


def _softplus(x):
    return jnp.maximum(x, 0.0) + jnp.log(1.0 + jnp.exp(-jnp.abs(x)))


def _seg_ones(width):
    r = lax.broadcasted_iota(jnp.int32, (LANES, LANES), 0) // width
    c = lax.broadcasted_iota(jnp.int32, (LANES, LANES), 1) // width
    return jnp.where(r == c, 1.0, 0.0).astype(BF16)


def _resident(shape, layer=None):
    nd = len(shape)
    if layer is None:
        return pl.BlockSpec(shape, lambda *_: (0,) * nd, pipeline_mode=pl.Buffered(1))
    return pl.BlockSpec((None,) + tuple(shape), lambda *_: (layer,) + (0,) * nd, pipeline_mode=pl.Buffered(1))


def _params(sem):
    return pltpu.CompilerParams(dimension_semantics=sem, vmem_limit_bytes=VMEM_LIMIT)


def _mem_kv_kernel(mem_ref, g_ref, wk_ref, wv_ref, k_ref, v_ref):
    mn = _rms(mem_ref[...], g_ref[0]).astype(BF16)
    rows = mn.shape[0]
    for w_ref, o_ref in ((wk_ref, k_ref), (wv_ref, v_ref)):
        z = _dot(mn, w_ref[0])
        for h in range(MEM_HEADS):
            o_ref[0, pl.ds(h, rows, stride=MEM_HEADS), :] = z[:, h * MEM_HEAD:(h + 1) * MEM_HEAD]


def _mem_kv(mem2d, g, wk, wv):
    rows = mem2d.shape[0]
    out = jax.ShapeDtypeStruct((DEPTH, rows * MEM_HEADS, MEM_HEAD), F32)
    return pl.pallas_call(
        _mem_kv_kernel,
        grid=(DEPTH,),
        in_specs=[
            pl.BlockSpec((rows, D_MODEL), lambda l: (0, 0)),
            pl.BlockSpec((1, 1, D_MODEL), lambda l: (l, 0, 0)),
            pl.BlockSpec((1, D_MODEL, MEM_WIDTH), lambda l: (l, 0, 0)),
            pl.BlockSpec((1, D_MODEL, MEM_WIDTH), lambda l: (l, 0, 0)),
        ],
        out_specs=[pl.BlockSpec((1, rows * MEM_HEADS, MEM_HEAD), lambda l: (l, 0, 0))] * 2,
        out_shape=[out, out],
        compiler_params=_params(("arbitrary",)),
        name="mem_kv",
    )(mem2d, g, wk, wv)


def _mem_attend_kernel(q_ref, k_ref, v_ref, o_ref):
    M = k_ref.shape[1] // MEM_HEADS
    nb = q_ref.shape[0]
    chains = [(i, h) for i in range(nb) for h in range(MEM_HEADS)]
    q = [q_ref[i, :, h * MEM_HEAD:(h + 1) * MEM_HEAD] for i, h in chains]
    k = [k_ref[i, pl.ds(h, M, stride=MEM_HEADS), :].astype(BF16) for i, h in chains]
    v = [v_ref[i, pl.ds(h, M, stride=MEM_HEADS), :].astype(BF16) for i, h in chains]
    s = [_dot_nt(a, b) for a, b in zip(q, k)]
    p = [jnp.exp(z - jnp.max(z, axis=-1, keepdims=True)) for z in s]
    o = [_dot(a.astype(BF16), b) / jnp.sum(a, axis=-1, keepdims=True) for a, b in zip(p, v)]
    for i in range(nb):
        o_ref[i] = jnp.concatenate(o[i * MEM_HEADS:(i + 1) * MEM_HEADS], axis=-1).astype(o_ref.dtype)


def _mem_attend(q, mk, mv, tq, nb, layer):
    B, T, _ = q.shape
    MH = mk.shape[2]
    return pl.pallas_call(
        _mem_attend_kernel,
        grid=(B // nb, T // tq),
        in_specs=[
            pl.BlockSpec((nb, tq, MEM_WIDTH), lambda b, i: (b, i, 0)),
            pl.BlockSpec((None, nb, MH, MEM_HEAD), lambda b, i: (layer, b, 0, 0)),
            pl.BlockSpec((None, nb, MH, MEM_HEAD), lambda b, i: (layer, b, 0, 0)),
        ],
        out_specs=pl.BlockSpec((nb, tq, MEM_WIDTH), lambda b, i: (b, i, 0)),
        out_shape=jax.ShapeDtypeStruct((B, T, MEM_WIDTH), BF16),
        compiler_params=_params(("arbitrary", "arbitrary")),
        name="mem_attend",
    )(q, mk, mv)


def _out_ffn_kernel(x_ref, yt_ref, ym_ref, wo_ref, g_ref, wg_ref, wu_ref, wd_ref, fg_ref, o_ref,
                    *, final, ft):
    x1 = x_ref[0] + (_dot(yt_ref[0], wo_ref[:D_MODEL, :]) + _dot(ym_ref[0], wo_ref[D_MODEL:, :]))
    h = _rms(x1, g_ref[...]).astype(BF16)
    acc = jnp.zeros_like(x1)
    for f in range(0, D_FF, ft):
        gt = _dot(h, wg_ref[:, f:f + ft])
        up = _dot(h, wu_ref[:, f:f + ft])
        act = (gt * _sigmoid(gt) * up).astype(BF16)
        acc = acc + _dot(act, wd_ref[f:f + ft, :])
    acc = x1 + acc
    if final:
        acc = _rms(acc, fg_ref[...])
    o_ref[0] = acc


def _out_ffn(x, ytok, ymem, wo, g, wg, wu, wd, fg, tm, final, layer):
    B, T, _ = x.shape
    row = lambda w: pl.BlockSpec((1, tm, w), lambda b, i: (b, i, 0))
    return pl.pallas_call(
        functools.partial(_out_ffn_kernel, final=final, ft=256),
        grid=(B, T // tm),
        in_specs=[
            row(D_MODEL), row(D_MODEL), row(MEM_WIDTH),
            _resident((D_MODEL + MEM_WIDTH, D_MODEL), layer),
            _resident((1, D_MODEL)),
            _resident((D_MODEL, D_FF), layer), _resident((D_MODEL, D_FF), layer), _resident((D_FF, D_MODEL), layer),
            _resident((1, D_MODEL)),
        ],
        out_specs=row(D_MODEL),
        out_shape=jax.ShapeDtypeStruct((B, T, D_MODEL), F32),
        compiler_params=_params(("arbitrary", "arbitrary")),
        name="out_ffn",
    )(x, ytok, ymem, wo, g, wg, wu, wd, fg)


def _diff_proj_kernel(x_ref, g_ref, w_ref, cos_ref, sin_ref, k_all_ref, v_all_ref,
                      q_ref, k_ref, kb_ref, v_ref, vb_ref, mq_ref):
    del k_all_ref, v_all_ref
    xn = _rms(x_ref[0], g_ref[...]).astype(BF16)
    tm = xn.shape[0]
    cos = cos_ref[...]
    sin = sin_ref[...]
    lane = lax.broadcasted_iota(jnp.int32, (tm, LANES), 1)
    first = (lane % ATTN_HEAD) < (ATTN_HEAD // 2)

    def rope(z):
        rot = jnp.where(first, pltpu.roll(z, LANES - ATTN_HEAD // 2, 1), pltpu.roll(z, ATTN_HEAD // 2, 1))
        return z * cos + rot * sin

    half = D_MODEL // 2
    for c in range(2):
        z = _dot(xn, w_ref[:, c * half:(c + 1) * half])
        for j in range(half // LANES):
            col = c * half + j * LANES
            q_ref[0, :, col:col + LANES] = (rope(z[:, j * LANES:(j + 1) * LANES]) * Q_SCALE).astype(BF16)
    for c in range(2):
        z = _dot(xn, w_ref[:, D_MODEL + c * half:D_MODEL + (c + 1) * half])
        for j in range(half // LANES):
            col = c * half + j * LANES
            kr = rope(z[:, j * LANES:(j + 1) * LANES])
            k_ref[0, pl.ds(col // LANES, tm, stride=ATTN_HEADS), :] = kr
            kb_ref[0, :, col:col + LANES] = kr.astype(BF16)
    for c in range(2):
        z = _dot(xn, w_ref[:, 2 * D_MODEL + c * half:2 * D_MODEL + (c + 1) * half])
        for j in range(half // LANES):
            head = (c * half) // LANES + j
            v_ref[0, pl.ds(head, tm, stride=ATTN_HEADS), :] = z[:, j * LANES:(j + 1) * LANES]
        vb_ref[0, :, c * half:(c + 1) * half] = z.astype(BF16)
    z = _dot(xn, w_ref[:, 3 * D_MODEL:])
    mq_ref[0] = (z * (MEM_HEAD ** -0.5)).astype(BF16)


def _diff_proj(x, g, w, cos_t, sin_t, tm, layer, k_all, v_all, slab):
    B, T, _ = x.shape
    row = lambda w_: pl.BlockSpec((1, tm, w_), lambda b, i: (b, i, 0))
    sd = lambda w_, dt: jax.ShapeDtypeStruct((B, T, w_), dt)
    by_head = pl.BlockSpec((None, 1, tm * ATTN_HEADS, LANES), lambda b, i: (slab, b, i, 0))
    sd_head = jax.ShapeDtypeStruct(k_all.shape, F32)
    in_hbm = pl.BlockSpec(memory_space=pl.ANY)
    return pl.pallas_call(
        _diff_proj_kernel,
        grid=(B, T // tm),
        in_specs=[
            row(D_MODEL), _resident((1, D_MODEL)), _resident((D_MODEL, 3 * D_MODEL + MEM_WIDTH), layer),
            pl.BlockSpec((tm, LANES), lambda b, i: (i, 0)),
            pl.BlockSpec((tm, LANES), lambda b, i: (i, 0)),
            in_hbm, in_hbm,
        ],
        out_specs=[row(D_MODEL), by_head, row(D_MODEL), by_head, row(D_MODEL), row(MEM_WIDTH)],
        out_shape=[sd(D_MODEL, BF16), sd_head, sd(D_MODEL, BF16), sd_head,
                   sd(D_MODEL, BF16), sd(MEM_WIDTH, BF16)],
        input_output_aliases={5: 1, 6: 3},
        compiler_params=_params(("arbitrary", "arbitrary")),
        name="diff_proj",
    )(x, g, w, cos_t, sin_t, k_all, v_all)


def _lambda(lam_ref, lam_init):
    lp = lam_ref[...]
    s1 = jnp.sum(lp[0:1] * lp[1:2], axis=-1, keepdims=True)
    s2 = jnp.sum(lp[2:3] * lp[3:4], axis=-1, keepdims=True)
    return jnp.exp(s1) - jnp.exp(s2) + lam_init


def _subln(o, sg, lam_init):
    return o * lax.rsqrt(jnp.mean(o * o, axis=-1, keepdims=True) + SUBLN_EPS) * sg * (1.0 - lam_init)


def _flash_kernel(q_ref, k_ref, v_ref, lam_ref, sg_ref, o_ref, s_sc, p_sc, m_sc, l_sc, acc_sc,
                  *, tq, tk, nh, rb, lam_init):
    qi = pl.program_id(2)
    assert tq == tk
    rows = 2 * tq
    nfull = (qi * tq) // tk
    fb = LANES
    heads = range(nh)
    hcol = [slice(h * LANES, (h + 1) * LANES) for h in heads]
    lane = lax.broadcasted_iota(jnp.int32, (tq, LANES), 1)

    def stacked(q):
        zero = jnp.zeros_like(q)
        return jnp.concatenate([jnp.where(lane < ATTN_HEAD, q, zero), jnp.where(lane < ATTN_HEAD, zero, q)], axis=0)

    qs = [stacked(q_ref[0, :, hcol[h]]) for h in heads]

    def kv(ref, i, h):
        return ref[0, pl.ds(pl.multiple_of(i * tk, tk), tk), hcol[h]]

    def fold(parts, op):
        while len(parts) > 1:
            parts = [op(parts[i], parts[i + 1]) for i in range(0, len(parts) - 1, 2)] + parts[len(parts) & ~1:]
        return parts[0]

    blocks =[slice(r0, r0 + rb) for r0 in range(0, rows, rb)]

    def softmax_rows(h, rs, s, pv):
        cols = [s[:, c:c + LANES] for c in range(0, s.shape[1], LANES)]
        m_old = m_sc[h, rs]
        m_new = jnp.maximum(m_old, jnp.max(fold(cols, jnp.maximum), axis=-1, keepdims=True))
        alpha = jnp.exp2(m_old - m_new)
        p = [jnp.exp2(z - m_new) for z in cols]
        l_sc[h, rs] = alpha * l_sc[h, rs] + fold(p, jnp.add)
        acc_sc[h, rs] = alpha * (acc_sc[h, rs] + pv)
        m_sc[h, rs] = m_new
        return jnp.concatenate(p, axis=-1).astype(BF16)

    m_sc[...] = jnp.full(m_sc.shape, _NEG, F32)
    l_sc[...] = jnp.zeros(l_sc.shape, F32)
    acc_sc[...] = jnp.zeros(acc_sc.shape, F32)
    for h in heads:
        p_sc[2 * h + 1] = jnp.zeros(p_sc.shape[1:], BF16)
    for h in heads:
        s_sc[2 * h] = _dot_nt(qs[h], kv(k_ref, 0, h))

    def stage(cur, k):
        for h in heads:
            k_next = kv(k_ref, k + 1, h)
            v_prev = kv(v_ref, jnp.maximum(k - 1, 0), h)
            for rs in blocks:
                s_sc[2 * h + 1 - cur, rs] = _dot_nt(qs[h][rs], k_next)
                pv = _dot(p_sc[2 * h + 1 - cur, rs], v_prev)
                p_sc[2 * h + cur, rs] = softmax_rows(h, rs, s_sc[2 * h + cur, rs], pv)

    def body4(j, carry):
        for i in range(4):
            stage(i % 2, 4 * j + i)
        return carry

    n4 = nfull // 4
    lax.fori_loop(0, n4, body4, 0)

    def body2(j, carry):
        stage(0, 4 * n4 + 2 * j)
        stage(1, 4 * n4 + 2 * j + 1)
        return carry

    lax.fori_loop(0, (nfull - 4 * n4) // 2, body2, 0)

    def finish(cur):
        lam = _lambda(lam_ref, lam_init)
        for h in heads:
            v_prev = kv(v_ref, jnp.maximum(nfull - 1, 0), h)
            v_last = kv(v_ref, nfull, h)
            for r0 in range(0, tq, fb):
                cw = r0 + fb
                visible = (lax.broadcasted_iota(jnp.int32, (fb, cw), 1)
                           <= lax.broadcasted_iota(jnp.int32, (fb, cw), 0) + r0)
                on = []
                for rs in (slice(r0, r0 + fb), slice(tq + r0, tq + r0 + fb)):
                    pv = _dot(p_sc[2 * h + 1 - cur, rs], v_prev)
                    p = softmax_rows(h, rs, jnp.where(visible, s_sc[2 * h + cur, rs, :cw], _NEG), pv)
                    on.append((acc_sc[h, rs] + _dot(p, v_last[:cw])) / jnp.sum(l_sc[h, rs], axis=-1, keepdims=True))
                o = on[0] - lam * on[1]
                o_ref[0, r0:r0 + fb, hcol[h]] = _subln(o, sg_ref[...], lam_init).astype(o_ref.dtype)

    @pl.when(nfull % 2 == 0)
    def _():
        finish(0)

    @pl.when(nfull % 2 == 1)
    def _():
        stage(0, nfull - 1)
        finish(1)


def _flash(q, k, v, lam_p, sg, lam_init, tq, tk, nh, rb):
    B, T, _ = q.shape
    rows = 2 * tq
    width = nh * LANES
    return pl.pallas_call(
        functools.partial(_flash_kernel, tq=tq, tk=tk, nh=nh, rb=rb, lam_init=lam_init),
        grid=(B, ATTN_HEADS // nh, T // tq),
        in_specs=[
            pl.BlockSpec((1, tq, width), lambda b, h, i: (b, i, h)),
            pl.BlockSpec((1, T, width), lambda b, h, i: (b, 0, h)),
            pl.BlockSpec((1, T, width), lambda b, h, i: (b, 0, h)),
            _resident((4, ATTN_HEAD)), _resident((1, LANES)),
        ],
        out_specs=pl.BlockSpec((1, tq, width), lambda b, h, i: (b, i, h)),
        out_shape=jax.ShapeDtypeStruct((B, T, D_MODEL), BF16),
        scratch_shapes=[pltpu.VMEM((2 * nh, rows, tk), F32), pltpu.VMEM((2 * nh, rows, tk), BF16),
                        pltpu.VMEM((nh, rows, LANES), F32), pltpu.VMEM((nh, rows, LANES), F32),
                        pltpu.VMEM((nh, rows, LANES), F32)],
        compiler_params=_params(("arbitrary", "arbitrary", "arbitrary")),
        name="diff_flash",
    )(q, k, v, lam_p, sg)


def _dec_attn_kernel(*refs, lam_init, n_pg):
    pt_ref, q_ref = refs[0], refs[1]
    k_refs = refs[2:2 + n_pg]
    v_refs = refs[2 + n_pg:2 + 2 * n_pg]
    kn_ref, vn_ref, lam_ref, sg_ref, o_ref, m_sc, l_sc, acc_sc = refs[2 + 2 * n_pg:]
    del pt_ref
    j = pl.program_id(1)
    H = ATTN_HEADS
    G = 2 * H

    @pl.when(j == 0)
    def _():
        m_sc[...] = jnp.full(m_sc.shape, _NEG, F32)
        l_sc[...] = jnp.zeros(l_sc.shape, F32)
        acc_sc[...] = jnp.zeros(acc_sc.shape, F32)

    q8 = q_ref[0]
    lane = lax.broadcasted_iota(jnp.int32, (H, LANES), 1)
    zero = jnp.zeros_like(q8)
    q16 = jnp.concatenate([jnp.where(lane < ATTN_HEAD, q8, zero), jnp.where(lane < ATTN_HEAD, zero, q8)], axis=0)

    n = PAGE_SIZE * H
    same_head = (lax.broadcasted_iota(jnp.int32, (G, n), 0) % H) == (lax.broadcasted_iota(jnp.int32, (G, n), 1) % H)
    s = [jnp.where(same_head, _dot_nt(q16, kr[...].reshape(n, LANES).astype(BF16)), _NEG) for kr in k_refs]
    m_old = m_sc[...]
    m_new = m_old
    for z in s:
        m_new = jnp.maximum(m_new, jnp.max(z, axis=-1, keepdims=True))
    alpha = jnp.exp2(m_old - m_new)
    p = [jnp.exp2(z - m_new) for z in s]
    l_sc[...] = alpha * l_sc[...] + sum(jnp.sum(z, axis=-1, keepdims=True) for z in p)
    pv = sum(_dot(z.astype(BF16), vr[...].reshape(n, LANES).astype(BF16)) for z, vr in zip(p, v_refs))
    acc_sc[...] = alpha * acc_sc[...] + pv
    m_sc[...] = m_new

    @pl.when(j == pl.num_programs(1) - 1)
    def _():
        kn = kn_ref[0]
        vn = vn_ref[0]
        kn16 = jnp.concatenate([kn, kn], axis=0)
        vn16 = jnp.concatenate([vn, vn], axis=0)
        sn = jnp.sum(q16.astype(F32) * kn16, axis=-1, keepdims=True)
        m0 = m_sc[...]
        m1 = jnp.maximum(m0, sn)
        a0 = jnp.exp2(m0 - m1)
        pn = jnp.exp2(sn - m1)
        l1 = a0 * l_sc[...] + pn
        acc1 = a0 * acc_sc[...] + pn * vn16
        on = acc1 / l1
        lam = _lambda(lam_ref, lam_init)
        o = on[:H] - lam * on[H:]
        o_ref[0] = _subln(o, sg_ref[...], lam_init).astype(o_ref.dtype)


def _dec_attn(page_table, q, cache_k, cache_v, kn, vn, lam_p, sg, lam_init, layer):
    B, n_pages = page_table.shape
    H = ATTN_HEADS
    n_pg = math.gcd(DEC_PAGES, n_pages)
    head = lambda: pl.BlockSpec((1, H, LANES), lambda b, j, pt: (b, 0, 0))
    page = lambda i: pl.BlockSpec((None, None, PAGE_SIZE, H, LANES),
                                  lambda b, j, pt: (layer, pt[b, j * n_pg + i], 0, 0, 0))
    pages = [page(i) for i in range(n_pg)]
    grid_spec = pltpu.PrefetchScalarGridSpec(
        num_scalar_prefetch=1,
        grid=(B, n_pages // n_pg),
        in_specs=[head()] + pages + pages + [head(), head(), _resident((4, ATTN_HEAD)), _resident((1, LANES))],
        out_specs=pl.BlockSpec((1, H, LANES), lambda b, j, pt: (b, 0, 0)),
        scratch_shapes=[pltpu.VMEM((2 * H, 1), F32), pltpu.VMEM((2 * H, 1), F32), pltpu.VMEM((2 * H, LANES), F32)],
    )
    return pl.pallas_call(
        functools.partial(_dec_attn_kernel, lam_init=lam_init, n_pg=n_pg),
        grid_spec=grid_spec,
        out_shape=jax.ShapeDtypeStruct((B, H, LANES), BF16),
        compiler_params=_params(("arbitrary", "arbitrary")),
        name="paged_diff_attn",
    )(page_table, q, *([cache_k] * n_pg), *([cache_v] * n_pg), kn, vn, lam_p, sg)


_V_MU, _V_W0, _V_A0, _V_V0, _V_KK, _V_KA, _V_G = 0, 6, 7, 8, 9, 10, 11
_N_VEC = 16


def _rwkv_proj_kernel(*refs, shifted, has_vfirst):
    it = iter(refs)
    x_ref = next(it)
    prev_ref = next(it)
    shift_ref = next(it) if shifted else None
    vf_ref = next(it) if has_vfirst else None
    vec_ref, w_ref, w1_ref, w2_ref, a1_ref, a2_ref = (next(it) for _ in range(6))
    v1_ref, v2_ref = (next(it), next(it)) if has_vfirst else (None, None)
    g1_ref, g2_ref = next(it), next(it)
    r_ref, lw_ref, k_ref, v_ref, a_ref, b_ref, g_ref, mq_ref, xs_ref = (next(it) for _ in range(9))

    vec = vec_ref[...]
    row = lambda i: vec[i:i + 1]
    gain = row(_V_G)
    xn = _rms(x_ref[0], gain)
    tm = xn.shape[0]
    if shifted:
        pr = _rms(prev_ref[0][7:8], gain)
        pr = jnp.where(pl.program_id(1) == 0, shift_ref[0], pr)
        ridx = lax.broadcasted_iota(jnp.int32, (tm, 1), 0)
        xprev = jnp.where(ridx == 0, pr, pltpu.roll(xn, 1, 0))
        xs_ref[0] = xn[tm - 1:tm]
    else:
        xprev = prev_ref[0]
        xs_ref[0] = xn
    xx = xprev - xn
    mix = lambda j: (xn + xx * row(_V_MU + j)).astype(BF16)

    D = D_MODEL
    r_ref[0] = _dot(mix(0), w_ref[:, :D])
    k = _dot(mix(2), w_ref[:, D:2 * D])
    seg = _seg_ones(RWKV_HEAD)
    kk = k * row(_V_KK)
    sq = kk * kk
    n2 = jnp.concatenate(
        [sum(_dot(p_, seg) for p_ in _split2(sq[:, c:c + LANES])) for c in range(0, D, LANES)], axis=-1)
    kk = kk / jnp.maximum(jnp.sqrt(n2), 1e-12)
    a_ref[0] = -kk
    a = _sigmoid(row(_V_A0) + _dot(_dot(mix(4), a1_ref[...]).astype(BF16), a2_ref[...]))
    b_ref[0] = kk * a
    k_ref[0] = k * (1.0 + (a - 1.0) * row(_V_KA))
    xv = mix(3)
    v = _dot(xv, w_ref[:, 2 * D:3 * D])
    if has_vfirst:
        gate = _sigmoid(row(_V_V0) + _dot(_dot(xv, v1_ref[...]).astype(BF16), v2_ref[...]))
        v = v + (vf_ref[0] - v) * gate
    v_ref[0] = v
    w_in = row(_V_W0) + _dot(jnp.tanh(_dot(mix(1), w1_ref[...])).astype(BF16), w2_ref[...])
    w_log = -_softplus(-w_in) - 0.5
    lw_ref[0] = -jnp.exp(w_log)
    g_ref[0] = _dot(_sigmoid(_dot(mix(5), g1_ref[...])).astype(BF16), g2_ref[...])
    mq_ref[0] = (_dot(xn.astype(BF16), w_ref[:, 3 * D:]) * (MEM_HEAD ** -0.5)).astype(BF16)


def _rwkv_proj(x, prev, shift, vfirst, vec, w, loras, tm, layer):
    B, T, _ = x.shape
    shifted = shift is not None
    has_vfirst = vfirst is not None
    row = lambda w_: pl.BlockSpec((1, tm, w_), lambda b, i: (b, i, 0))
    ins, specs = [x], [row(D_MODEL)]
    if shifted:
        ins += [prev, shift]
        specs += [pl.BlockSpec((1, 8, D_MODEL), lambda b, i: (b, jnp.maximum(i * (tm // 8) - 1, 0), 0)),
                  pl.BlockSpec((1, 1, D_MODEL), lambda b, i: (b, 0, 0))]
    else:
        ins += [prev]
        specs += [row(D_MODEL)]
    if has_vfirst:
        ins.append(vfirst)
        specs.append(row(D_MODEL))
    w1, w2, a1, a2, v1, v2, g1, g2 = loras
    small = [w1, w2, a1, a2] + ([v1, v2] if has_vfirst else []) + [g1, g2]
    ins += [vec, w] + small
    specs += [_resident(vec.shape), _resident(w.shape[1:], layer)] + [_resident(s.shape) for s in small]
    sd = lambda dt, w_=D_MODEL, t_=T: jax.ShapeDtypeStruct((B, t_, w_), dt)
    xs_rows = 1 if shifted else T
    xs_spec = (pl.BlockSpec((1, 1, D_MODEL), lambda b, i: (b, 0, 0)) if shifted else row(D_MODEL))
    return pl.pallas_call(
        functools.partial(_rwkv_proj_kernel, shifted=shifted, has_vfirst=has_vfirst),
        grid=(B, T // tm),
        in_specs=specs,
        out_specs=[row(D_MODEL)] * 7 + [row(MEM_WIDTH), xs_spec],
        out_shape=[sd(F32)] * 7 + [sd(BF16, MEM_WIDTH), sd(F32, D_MODEL, xs_rows)],
        compiler_params=_params(("arbitrary", "arbitrary")),
        name="rwkv_proj",
    )(*ins)


def _wkv_kernel(r_ref, lw_ref, k_ref, v_ref, a_ref, b_ref, g_ref, vec_ref, s0_ref, y_ref, so_ref, s_sc,
                *, L, NB, P, passes):
    c = pl.program_id(2)
    sel = [(bb, p, slice(p * LANES, (p + 1) * LANES)) for bb in range(NB) for p in range(P)]

    @pl.when(c == 0)
    def _():
        for n, (bb, p, _) in enumerate(sel):
            s_sc[n] = s0_ref[bb, p]

    L2 = 2 * L
    tri = jnp.where(lax.broadcasted_iota(jnp.int32, (L, L), 0) >= lax.broadcasted_iota(jnp.int32, (L, L), 1),
                    1.0, 0.0).astype(BF16)
    r2 = lax.broadcasted_iota(jnp.int32, (L2, L2), 0)
    c2 = lax.broadcasted_iota(jnp.int32, (L2, L2), 1)
    strict = (r2 % L) > (c2 % L)
    incl = (r2 % L) >= (c2 % L)
    eye = jnp.where(r2 == c2, 1.0, 0.0)
    m0 = lax.broadcasted_iota(jnp.int32, (L, LANES), 1) < RWKV_HEAD
    seg = _seg_ones(RWKV_HEAD)
    mm = functools.partial(_mm, passes=passes)

    def stack(z):
        return jnp.concatenate([jnp.where(m0, z, 0.0), jnp.where(m0, 0.0, z)], axis=0)

    def segsum(z):
        hi, lo = _split2(z)
        return _dot(hi, seg) + _dot(lo, seg)

    pairs = range(len(sel))
    r = [r_ref[bb, :, sl] for bb, _, sl in sel]
    lw = [lw_ref[bb, :, sl] for bb, _, sl in sel]
    k = [k_ref[bb, :, sl] for bb, _, sl in sel]
    v = [v_ref[bb, :, sl] for bb, _, sl in sel]
    a = [a_ref[bb, :, sl] for bb, _, sl in sel]
    b = [b_ref[bb, :, sl] for bb, _, sl in sel]
    cum = [_dot_exact_lhs(tri, z) for z in lw]
    c_end = [z[L - 1:L, :] for z in cum]
    e_neg = [jnp.exp(-z) for z in cum]
    at_s = [stack(a[p] * jnp.exp(cum[p] - lw[p])) for p in pairs]
    rt_s = [stack(r[p] * jnp.exp(cum[p])) for p in pairs]
    bt_s = [stack(b[p] * e_neg[p]) for p in pairs]
    kt_s = [stack(k[p] * e_neg[p]) for p in pairs]
    v_s = [stack(z) for z in v]
    S = [s_sc[p] for p in pairs]

    n_ab = [jnp.where(strict, mm(at_s[p], bt_s[p], "nt"), 0.0) for p in pairs]
    a_ak = [jnp.where(strict, mm(at_s[p], kt_s[p], "nt"), 0.0) for p in pairs]
    a_rb = [jnp.where(incl, mm(rt_s[p], bt_s[p], "nt"), 0.0) for p in pairs]
    a_rk = [jnp.where(incl, mm(rt_s[p], kt_s[p], "nt"), 0.0) for p in pairs]

    t_inv = [eye + z for z in n_ab]
    pw = n_ab
    for _ in range(int(math.log2(L)) - 1):
        pw = [mm(z, z, "nn") for z in pw]
        t_inv = [t_inv[p] + mm(t_inv[p], pw[p], "nn") for p in pairs]

    x_s = [mm(at_s[p], S[p], "nt") + mm(a_ak[p], v_s[p], "nn") for p in pairs]
    u_s = [mm(t_inv[p], x_s[p], "nn") for p in pairs]
    y_s = [mm(rt_s[p], S[p], "nt") + mm(a_rb[p], u_s[p], "nn") + mm(a_rk[p], v_s[p], "nn") for p in pairs]
    for p in pairs:
        e_end = jnp.exp(c_end[p] - cum[p])
        s_sc[p] = (S[p] * jnp.exp(c_end[p]) + mm(u_s[p], stack(b[p] * e_end), "tn")
                   + mm(v_s[p], stack(k[p] * e_end), "tn"))

    y = [z[:L] + z[L:] for z in y_s]
    yc = [z - segsum(z) * (1.0 / RWKV_HEAD) for z in y]
    var = [segsum(z * z) * (1.0 / RWKV_HEAD) for z in yc]
    for p, (bb, _, sl) in enumerate(sel):
        vec = vec_ref[:, sl]
        yn = yc[p] * lax.rsqrt(var[p] + GN_EPS) * vec[0:1] + vec[1:2]
        bonus = segsum(r[p] * k[p] * vec[2:3]) * v[p]
        y_ref[bb, :, sl] = ((yn + bonus) * g_ref[bb, :, sl]).astype(y_ref.dtype)

    @pl.when(c == pl.num_programs(2) - 1)
    def _():
        for n, (bb, p, _) in enumerate(sel):
            so_ref[bb, p] = s_sc[n]


def _wkv(r, lw, k, v, a, b, g, vec, s0, L, NB, P, passes):
    B, T, _ = r.shape
    npair = D_MODEL // LANES
    width = P * LANES
    tile = lambda: pl.BlockSpec((NB, L, width), lambda bb, pg, c: (bb, c, pg))
    state = lambda: pl.BlockSpec((NB, P, LANES, LANES), lambda bb, pg, c: (bb, pg, 0, 0))
    return pl.pallas_call(
        functools.partial(_wkv_kernel, L=L, NB=NB, P=P, passes=passes),
        grid=(B // NB, npair // P, T // L),
        in_specs=[tile() for _ in range(7)] + [pl.BlockSpec((8, width), lambda bb, pg, c: (0, pg)), state()],
        out_specs=[tile(), state()],
        out_shape=[jax.ShapeDtypeStruct((B, T, D_MODEL), BF16),
                   jax.ShapeDtypeStruct((B, npair, LANES, LANES), F32)],
        scratch_shapes=[pltpu.VMEM((NB * P, LANES, LANES), F32)],
        compiler_params=_params(("arbitrary", "arbitrary", "arbitrary")),
        name="wkv_chunked",
    )(r, lw, k, v, a, b, g, vec, s0)


def _wkv_step_kernel(r_ref, lw_ref, k_ref, v_ref, a_ref, b_ref, g_ref, vec_ref, s_ref, y_ref, so_ref, *, nb):
    hd = RWKV_HEAD
    npair = D_MODEL // LANES
    rid = lax.broadcasted_iota(jnp.int32, (8, LANES), 0)
    lid = lax.broadcasted_iota(jnp.int32, (8, LANES), 1)
    own = ((rid == 0) & (lid < hd)) | ((rid == 1) & (lid >= hd))
    seg = _seg_ones(hd)
    zpad = jnp.zeros((hd, hd), F32)
    cols = [slice(p * LANES, (p + 1) * LANES) for p in range(npair)]
    combos = [(i, p) for i in range(nb) for p in range(npair)]

    def vec_row(ref, i, p):
        return ref[i:i + 1, cols[p]]

    def at_row(z, i):
        return jnp.where(rid == i, z, 0.0).astype(BF16)

    def split_heads(z):
        return jnp.where(own, z, 0.0)

    S = [jnp.concatenate([jnp.concatenate([s_ref[i, 2 * p], zpad], axis=1),
                          jnp.concatenate([zpad, s_ref[i, 2 * p + 1]], axis=1)], axis=0) for i, p in combos]
    Sb = [z.astype(BF16) for z in S]
    u = [_dot_nt(at_row(vec_row(a_ref, i, p), 0), Sb[n])[0:1] for n, (i, p) in enumerate(combos)]
    lhs = [jnp.concatenate([split_heads(u[n]), split_heads(vec_row(v_ref, i, p))], axis=0).astype(BF16)
           for n, (i, p) in enumerate(combos)]
    rhs = [jnp.concatenate([split_heads(vec_row(b_ref, i, p)), split_heads(vec_row(k_ref, i, p))], axis=0).astype(BF16)
           for i, p in combos]
    S = [S[n] * jnp.exp(vec_row(lw_ref, i, p)) + _dot_tn(lhs[n], rhs[n]) for n, (i, p) in enumerate(combos)]
    for n, (i, p) in enumerate(combos):
        so_ref[i, 2 * p] = S[n][:hd, :hd]
        so_ref[i, 2 * p + 1] = S[n][hd:, hd:]
    yrow = [_dot_nt(at_row(vec_row(r_ref, i, p), i), S[n].astype(BF16)) for n, (i, p) in enumerate(combos)]

    def segsum(z):
        hi, lo = _split2(z)
        return _dot(hi, seg) + _dot(lo, seg)

    for p in range(npair):
        y = yrow[p]
        for i in range(1, nb):
            y = y + yrow[i * npair + p]
        yc = y - segsum(y) * (1.0 / hd)
        var = segsum(yc * yc) * (1.0 / hd)
        vec = vec_ref[:, cols[p]]
        yn = yc * lax.rsqrt(var + GN_EPS) * vec[0:1] + vec[1:2]
        bonus = segsum(r_ref[:, cols[p]] * k_ref[:, cols[p]] * vec[2:3]) * v_ref[:, cols[p]]
        y_ref[:, cols[p]] = (yn + bonus) * g_ref[:, cols[p]]


def _wkv_step(r, lw, k, v, a, b, g, vec, s0):
    n_seq = r.shape[0]
    nb = math.gcd(n_seq, STEP_SEQS)
    heads = D_MODEL // RWKV_HEAD
    tile = lambda: pl.BlockSpec((nb, D_MODEL), lambda i: (i, 0))
    state = lambda: pl.BlockSpec((nb, heads, RWKV_HEAD, RWKV_HEAD), lambda i: (i, 0, 0, 0))
    return pl.pallas_call(
        functools.partial(_wkv_step_kernel, nb=nb),
        grid=(n_seq // nb,),
        in_specs=[tile() for _ in range(7)] + [_resident((8, D_MODEL)), state()],
        out_specs=[tile(), state()],
        out_shape=[jax.ShapeDtypeStruct((n_seq, D_MODEL), F32),
                   jax.ShapeDtypeStruct((n_seq, heads, RWKV_HEAD, RWKV_HEAD), F32)],
        compiler_params=_params(("arbitrary",)),
        name="wkv_step",
    )(r, lw, k, v, a, b, g, vec, s0)


def _state_to_pairs(s):
    B = s.shape[0]
    s = s.reshape(B, -1, 2, RWKV_HEAD, RWKV_HEAD)
    z = jnp.zeros_like(s[:, :, 0])
    top = jnp.concatenate([s[:, :, 0], z], axis=-1)
    bot = jnp.concatenate([z, s[:, :, 1]], axis=-1)
    return jnp.concatenate([top, bot], axis=-2)


def _state_from_pairs(sp):
    B = sp.shape[0]
    h = RWKV_HEAD
    return jnp.stack([sp[:, :, :h, :h], sp[:, :, h:, h:]], axis=2).reshape(B, -1, h, h)


def _rope_tables(pos):
    half = ATTN_HEAD // 2
    inv = jnp.power(ROPE_THETA, -jnp.arange(half, dtype=F32) * 2.0 / ATTN_HEAD)
    ang = pos.astype(F32)[:, None] * inv[None, :]
    cos = jnp.cos(ang)
    sin = jnp.sin(ang)
    reps = LANES // ATTN_HEAD
    return jnp.tile(cos, (1, 2 * reps)), jnp.tile(jnp.concatenate([-sin, sin], axis=1), (1, reps))


def _pad_cols(w):
    return jnp.pad(w, ((0, 0), (0, LORA_PAD - w.shape[1])))


def _pad_rows(w):
    return jnp.pad(w, ((0, LORA_PAD - w.shape[0]), (0, 0)))


def _pad_tokens(z, t):
    return jnp.pad(z, ((0, 0), (0, t - z.shape[1]), (0, 0)))


def _trunk(x, pos, decode, shift0, wkv0, mem_k, mem_v, past, W):
    B, T, _ = x.shape
    if decode:
        xf = x.reshape(1, B, D_MODEL)
        tm = B
    else:
        xf = x
        tm = ROW_TILE
    cos_t, sin_t = _rope_tables(pos if not decode else jnp.broadcast_to(pos, (B,)))
    shifts, states = [], []
    xb, xt = xf.shape[:2]
    k_all = jnp.zeros((DEPTH // 2, xb, xt * ATTN_HEADS, LANES), F32)
    v_all = jnp.zeros_like(k_all)
    v_first = None
    for l in range(DEPTH):
        idx = l // 2
        if l % 2 == 0:
            vec = jnp.concatenate([
                W["mu"][idx], W["w0"][idx][None], W["a0"][idx][None],
                (W["v0"][idx - 1] if idx > 0 else jnp.zeros((D_MODEL,), F32))[None],
                W["k_k"][idx][None], W["k_a"][idx][None], W["mix_g"][l][None],
                jnp.zeros((_N_VEC - 12, D_MODEL), F32)], axis=0)
            loras = (W["w1"][idx], W["w2"][idx], W["a1"][idx], W["a2"][idx],
                     W["v1"][idx - 1] if idx > 0 else None, W["v2"][idx - 1] if idx > 0 else None,
                     W["g1"][idx], W["g2"][idx])
            if decode:
                outs = _rwkv_proj(xf, shift0[idx][None], None, v_first, vec, W["w_in"], loras, tm, l)
            else:
                outs = _rwkv_proj(xf, xf, shift0[idx][:, None], v_first, vec, W["w_in"], loras, RWKV_ROW_TILE, l)
            r, lw, k, v, a, b, g, mq, xs = outs
            if idx == 0:
                v_first = v
            shifts.append(xs.reshape(B, D_MODEL))
            scan_in = [r, lw, k, v, a, b, g]
            vec2 = jnp.concatenate([W["lnx_g"][idx][None], W["lnx_b"][idx][None], W["r_k"][idx][None],
                                    jnp.zeros((5, D_MODEL), F32)], axis=0)
            if decode:
                y, s_new = _wkv_step(*[z.reshape(B, D_MODEL) for z in scan_in], vec2, wkv0[idx])
                states.append(s_new)
                y_tok = y.astype(BF16).reshape(1, B, D_MODEL)
            else:
                y_tok, s_new = _wkv(*scan_in, vec2, _state_to_pairs(wkv0[idx]), WKV_CHUNK, B, WKV_PAIRS, 1)
                states.append(_state_from_pairs(s_new))
        else:
            lam_init = 0.8 - 0.6 * math.exp(-0.3 * l)
            q, k_all, kb, v_all, vb, mq = _diff_proj(xf, W["mix_g"][l][None], W["w_in"], cos_t, sin_t, tm, l,
                                                     k_all, v_all, idx)
            lam_p = jnp.stack([W["lam_q1"][idx], W["lam_k1"][idx], W["lam_q2"][idx], W["lam_k2"][idx]])
            sg = W["subln_g"][idx][None]
            if decode:
                cache_k, cache_v, page_table = past
                hd = lambda z: z.reshape(B, ATTN_HEADS, LANES)
                o = _dec_attn(page_table, hd(q), cache_k, cache_v, hd(k_all[idx]), hd(v_all[idx]), lam_p, sg,
                              lam_init, idx)
                y_tok = o.reshape(1, B, D_MODEL)
            else:
                y_tok = _flash(q, kb, vb, lam_p, sg, lam_init, ATTN_TILE, ATTN_TILE, 1, ATTN_ROW_BLOCK)
        if decode:
            mq8 = _pad_tokens(mq.reshape(B, 1, MEM_WIDTH), 8)
            y_mem = _mem_attend(mq8, mem_k, mem_v, 8, math.gcd(B, STEP_SEQS), l)[:, :1].reshape(1, B, MEM_WIDTH)
        else:
            y_mem = _mem_attend(mq, mem_k, mem_v, MEM_ATTN_TILE, 1, l)
        xf = _out_ffn(xf, y_tok, y_mem, W["w_out"], W["ffn_g"][l][None], W["w_gate"], W["w_up"],
                      W["w_down"], W["final_g"][None], tm, l == DEPTH - 1, l)
    kv_shape = (DEPTH // 2, B, T, ATTN_HEADS, LANES)
    return (xf.reshape(B, T, D_MODEL), jnp.stack(shifts), jnp.stack(states),
            k_all.reshape(kv_shape), v_all.reshape(kv_shape))


def kernel(x_prompt, x_sample, cache_k, cache_v, cache_mem_k, cache_mem_v, state_rwkv_wkv, state_rwkv_shift, page_table, mem_prompt, w_in, w_out, mix_norm_g, ffn_norm_g, w_gate, w_up, w_down, final_norm_g, mem_norm_g, w_mem_k, w_mem_v, rwkv_mu, rwkv_w0, rwkv_w1, rwkv_w2, rwkv_a0, rwkv_a1, rwkv_a2, rwkv_v0, rwkv_v1, rwkv_v2, rwkv_g1, rwkv_g2, rwkv_k_k, rwkv_k_a, rwkv_r_k, rwkv_lnx_g, rwkv_lnx_b, diff_lam_q1, diff_lam_k1, diff_lam_q2, diff_lam_k2, diff_subln_g):
    bf = lambda z: z.astype(BF16)
    n_rwkv = rwkv_mu.shape[0]
    W = dict(
        w_in=bf(w_in), w_out=bf(w_out), mix_g=mix_norm_g, ffn_g=ffn_norm_g,
        w_gate=bf(w_gate), w_up=bf(w_up), w_down=bf(w_down), final_g=final_norm_g,
        mu=rwkv_mu, w0=rwkv_w0, a0=rwkv_a0, v0=rwkv_v0, k_k=rwkv_k_k, k_a=rwkv_k_a,
        w1=[bf(_pad_cols(rwkv_w1[i])) for i in range(n_rwkv)],
        w2=[bf(_pad_rows(rwkv_w2[i])) for i in range(n_rwkv)],
        a1=[bf(_pad_cols(rwkv_a1[i])) for i in range(n_rwkv)],
        a2=[bf(_pad_rows(rwkv_a2[i])) for i in range(n_rwkv)],
        v1=[bf(_pad_cols(rwkv_v1[i])) for i in range(n_rwkv - 1)],
        v2=[bf(_pad_rows(rwkv_v2[i])) for i in range(n_rwkv - 1)],
        g1=bf(rwkv_g1), g2=bf(rwkv_g2),
        r_k=rwkv_r_k.reshape(n_rwkv, D_MODEL), lnx_g=rwkv_lnx_g, lnx_b=rwkv_lnx_b,
        lam_q1=diff_lam_q1, lam_k1=diff_lam_k1, lam_q2=diff_lam_q2, lam_k2=diff_lam_k2,
        subln_g=diff_subln_g,
    )

    B, T, _ = x_prompt.shape
    M = mem_prompt.shape[1]
    mk, mv = _mem_kv(mem_prompt.reshape(B * M, D_MODEL), mem_norm_g[:, None], bf(w_mem_k), bf(w_mem_v))
    mk = mk.reshape(DEPTH, B, M * MEM_HEADS, MEM_HEAD)
    mv = mv.reshape(DEPTH, B, M * MEM_HEADS, MEM_HEAD)
    p_mem_k = mk.reshape(DEPTH, B, M, MEM_HEADS, MEM_HEAD)
    p_mem_v = mv.reshape(DEPTH, B, M, MEM_HEADS, MEM_HEAD)

    pos_p = jnp.arange(T, dtype=jnp.int32)
    shift0 = jnp.zeros((n_rwkv, B, D_MODEL), F32)
    wkv0 = jnp.zeros((n_rwkv, B, D_MODEL // RWKV_HEAD, RWKV_HEAD, RWKV_HEAD), F32)
    y_prompt, p_shift, p_wkv, p_k, p_v = _trunk(x_prompt, pos_p, False, shift0, wkv0, mk, mv, None, W)

    Bs = x_sample.shape[0]
    past_len = page_table.shape[1] * PAGE_SIZE
    pos_s = past_len + jnp.arange(x_sample.shape[1], dtype=jnp.int32)
    smk = cache_mem_k.reshape(DEPTH, Bs, -1, MEM_HEAD)
    smv = cache_mem_v.reshape(DEPTH, Bs, -1, MEM_HEAD)
    y_sample, s_shift, s_wkv, s_k, s_v = _trunk(
        x_sample, pos_s, True, state_rwkv_shift, state_rwkv_wkv, smk, smv, (cache_k, cache_v, page_table), W)

    return (y_prompt, y_sample, p_wkv, p_shift, p_k, p_v, p_mem_k, p_mem_v, s_wkv, s_shift, s_k, s_v)
```
